```python
import jax, jax.numpy as jnp
from jax import lax
import numpy as np

D_MODEL = 1024
BATCH = 4
SEQ = 8192
DEPTH = 2

HEAD_DIM = 64
GRID_W = 64
NA_HEADS = 4
NA_WIN_ROWS = 8
NA_WIN_COLS = 16
DIL_HEADS = 6
DIL_PATTERNS = ((128, 1), (512, 4), (2048, 16))
DIL_Q_BLOCK = 128
CONV_CH = 384
CONV_WIDTH = 31
ROPE_THETA = 10000.0
NORM_EPS = 1e-6
NEG_INF = -1e30
NA_W = NA_HEADS * HEAD_DIM
DIL_W = DIL_HEADS * HEAD_DIM
D_MIX = NA_W + DIL_W + CONV_CH
D_IN = 4 * NA_W + 4 * DIL_W + 3 * CONV_CH

kernel_name = "hybrid_na_dilated_conformer_block"


def rms_norm(x, g):
    xf = x.astype(jnp.float32)
    y = xf * lax.rsqrt(jnp.mean(xf * xf, axis=-1, keepdims=True) + NORM_EPS)
    return (y * g.astype(jnp.float32)).astype(x.dtype)


def rotary(x, pos):
    half = HEAD_DIM // 2
    inv_freq = jnp.power(ROPE_THETA, -jnp.arange(half, dtype=jnp.float32) * 2.0 / HEAD_DIM)
    ang = pos.astype(jnp.float32)[:, None] * inv_freq[None, :]
    cos = jnp.cos(ang)[None, :, None, :]
    sin = jnp.sin(ang)[None, :, None, :]
    xf = x.astype(jnp.float32)
    x1, x2 = xf[..., :half], xf[..., half:]
    return jnp.concatenate([x1 * cos - x2 * sin, x2 * cos + x1 * sin], axis=-1).astype(x.dtype)


def neighbourhood_attention(q, k, v, rpb):
    B, S, H, dh = q.shape
    rows = S // GRID_W
    kh = min(NA_WIN_ROWS, rows)
    kw = NA_WIN_COLS

    def to_grid(t):
        return t.reshape(B, rows, GRID_W, H, dh).transpose(1, 0, 3, 2, 4)

    qg, kg, vg = to_grid(q), to_grid(k), to_grid(v)
    cols = np.arange(GRID_W)
    col_start = np.clip(cols - kw // 2, 0, GRID_W - kw)
    col_idx = col_start[:, None] + np.arange(kw)[None, :]
    col_off = col_idx - cols[:, None] + (NA_WIN_COLS - 1)
    bias_cols = rpb[:, :, col_off]
    scale = HEAD_DIM ** -0.5

    def row_step(args):
        q_row, r = args
        r0 = jnp.clip(r - kh // 2, 0, rows - kh)
        k_rows = lax.dynamic_slice_in_dim(kg, r0, kh, axis=0)
        v_rows = lax.dynamic_slice_in_dim(vg, r0, kh, axis=0)
        k_win = k_rows[:, :, :, col_idx]
        v_win = v_rows[:, :, :, col_idx]
        row_off = r0 + jnp.arange(kh) - r + (NA_WIN_ROWS - 1)
        bias = bias_cols[:, row_off].transpose(0, 2, 1, 3)
        s = jnp.einsum('bhcd,ibhcjd->bhcij', q_row, k_win).astype(jnp.float32) * scale
        s = s + bias[None].astype(jnp.float32)
        p = jax.nn.softmax(s.reshape(B, H, GRID_W, kh * kw), axis=-1)
        p = p.reshape(B, H, GRID_W, kh, kw).astype(v.dtype)
        return jnp.einsum('bhcij,ibhcjd->bhcd', p, v_win)

    out = lax.map(row_step, (qg, jnp.arange(rows)))
    return out.transpose(1, 0, 3, 2, 4).reshape(B, S, H * dh)


def dilated_attention(q, k, v):
    B, S, H, dh = q.shape
    nblk = S // DIL_Q_BLOCK
    kt = k.transpose(0, 2, 1, 3)
    vt = v.transpose(0, 2, 1, 3)
    qb = q.transpose(0, 2, 1, 3).reshape(B, H, nblk, DIL_Q_BLOCK, dh).transpose(2, 0, 1, 3, 4)
    offsets = [d * np.arange(-((w // 2) // d), (w // 2) // d + 1) for (w, d) in DIL_PATTERNS]
    scale = HEAD_DIM ** -0.5

    def block_step(args):
        q_blk, start = args
        pos = start + jnp.arange(DIL_Q_BLOCK)
        outs, lses = [], []
        for off in offsets:
            idx = pos[:, None] + off[None, :]
            valid = (idx >= 0) & (idx < S)
            idx_c = jnp.clip(idx, 0, S - 1)
            k_sel = jnp.take(kt, idx_c, axis=2)
            v_sel = jnp.take(vt, idx_c, axis=2)
            s = jnp.einsum('bhqd,bhqnd->bhqn', q_blk, k_sel).astype(jnp.float32) * scale
            s = jnp.where(valid, s, NEG_INF)
            lse = jax.nn.logsumexp(s, axis=-1, keepdims=True)
            p = jnp.exp(s - lse).astype(v.dtype)
            outs.append(jnp.einsum('bhqn,bhqnd->bhqd', p, v_sel).astype(jnp.float32))
            lses.append(lse)
        wts = jax.nn.softmax(jnp.stack(lses, axis=0), axis=0)
        return jnp.sum(wts * jnp.stack(outs, axis=0), axis=0).astype(q.dtype)

    out = lax.map(block_step, (qb, jnp.arange(nblk) * DIL_Q_BLOCK))
    return out.transpose(1, 0, 3, 2, 4).reshape(B, S, H * dh)


def conformer_conv(a, b, w_dw, b_dw, ln_g, ln_b, w_pw, b_pw):
    u = a * jax.nn.sigmoid(b)
    y = lax.conv_general_dilated(
        u, w_dw[:, None, :].astype(u.dtype), window_strides=(1,),
        padding=[(CONV_WIDTH // 2, CONV_WIDTH // 2)],
        dimension_numbers=('NWC', 'WIO', 'NWC'), feature_group_count=CONV_CH) + b_dw
    yf = y.astype(jnp.float32)
    mu = jnp.mean(yf, axis=-1, keepdims=True)
    var = jnp.mean(jnp.square(yf - mu), axis=-1, keepdims=True)
    yn = ((yf - mu) * lax.rsqrt(var + NORM_EPS) * ln_g.astype(jnp.float32) + ln_b.astype(jnp.float32)).astype(u.dtype)
    return jax.nn.silu(yn) @ w_pw + b_pw


def hybrid_layer(x, g, w_in, rpb, w_dw, b_dw, ln_g, ln_b, w_pw, b_pw, w_out, pos):
    B, S, _ = x.shape
    h = rms_norm(x, g)
    z = h @ w_in
    splits = [int(s) for s in np.cumsum([NA_W] * 4 + [DIL_W] * 4 + [CONV_CH] * 2)]
    aq, ak, av, ag, bq, bk, bv, bg, ca, cb, cg = jnp.split(z, splits, axis=-1)

    def heads(t, n):
        return t.reshape(B, S, n, HEAD_DIM)

    ya = neighbourhood_attention(heads(aq, NA_HEADS), heads(ak, NA_HEADS), heads(av, NA_HEADS), rpb)
    ya = ya * jax.nn.silu(ag)
    yb = dilated_attention(rotary(heads(bq, DIL_HEADS), pos), rotary(heads(bk, DIL_HEADS), pos),
                           heads(bv, DIL_HEADS))
    yb = yb * jax.nn.silu(bg)
    yc = conformer_conv(ca, cb, w_dw, b_dw, ln_g, ln_b, w_pw, b_pw) * jax.nn.silu(cg)
    return x + jnp.concatenate([ya, yb, yc], axis=-1) @ w_out


def setup_inputs(seed: int = 0) -> dict:
    key = jax.random.key(seed)
    ks = jax.random.split(key, 13)
    f32 = jnp.float32
    x = jax.random.normal(ks[0], (BATCH, SEQ, D_MODEL), f32)
    norm_g = 1.0 + 0.01 * jax.random.normal(ks[1], (DEPTH, D_MODEL), f32)
    w_in = jax.random.normal(ks[2], (DEPTH, D_MODEL, D_IN), f32) * D_MODEL ** -0.5
    na_rpb = 0.02 * jax.random.normal(ks[3], (DEPTH, NA_HEADS, 2 * NA_WIN_ROWS - 1, 2 * NA_WIN_COLS - 1), f32)
    conv_w = jax.random.normal(ks[4], (DEPTH, CONV_WIDTH, CONV_CH), f32) * CONV_WIDTH ** -0.5
    conv_b = 0.01 * jax.random.normal(ks[5], (DEPTH, CONV_CH), f32)
    conv_ln_g = 1.0 + 0.01 * jax.random.normal(ks[6], (DEPTH, CONV_CH), f32)
    conv_ln_b = 0.01 * jax.random.normal(ks[7], (DEPTH, CONV_CH), f32)
    pw_w = jax.random.normal(ks[8], (DEPTH, CONV_CH, CONV_CH), f32) * CONV_CH ** -0.5
    pw_b = 0.01 * jax.random.normal(ks[9], (DEPTH, CONV_CH), f32)
    w_out = jax.random.normal(ks[10], (DEPTH, D_MIX, D_MODEL), f32) * D_MIX ** -0.5
    final_g = 1.0 + 0.01 * jax.random.normal(ks[11], (D_MODEL,), f32)
    return {"x": x, "norm_g": norm_g, "w_in": w_in, "na_rpb": na_rpb, "conv_w": conv_w,
            "conv_b": conv_b, "conv_ln_g": conv_ln_g, "conv_ln_b": conv_ln_b, "pw_w": pw_w,
            "pw_b": pw_b, "w_out": w_out, "final_g": final_g}


def reference(x, norm_g, w_in, na_rpb, conv_w, conv_b, conv_ln_g, conv_ln_b, pw_w, pw_b, w_out, final_g):
    pos = jnp.arange(x.shape[1])
    h = x
    for l in range(DEPTH):
        h = hybrid_layer(h, norm_g[l], w_in[l], na_rpb[l], conv_w[l], conv_b[l], conv_ln_g[l],
                         conv_ln_b[l], pw_w[l], pw_b[l], w_out[l], pos)
    return rms_norm(h, final_g)
```

```python
import functools

import numpy as np
import jax
import jax.numpy as jnp
from jax import lax
from jax.experimental import pallas as pl
from jax.experimental.pallas import tpu as pltpu

D_MODEL = 1024
HEAD_DIM = 64
GRID_W = 64
NA_HEADS = 4
NA_WIN_ROWS = 8
NA_WIN_COLS = 16
DIL_HEADS = 6
DIL_DILATIONS = (1, 4, 16)
DIL_HALF = 64
CONV_CH = 384
CONV_WIDTH = 31
ROPE_THETA = 10000.0
NORM_EPS = 1e-6
NEG_INF = -1e30
NA_W = NA_HEADS * HEAD_DIM
DIL_W = DIL_HEADS * HEAD_DIM
D_MIX = NA_W + DIL_W + CONV_CH
SCALE = HEAD_DIM ** -0.5

LANES = 128
VMEM_LIMIT = 56 * 1024 * 1024

_C = np.cumsum([0] + [NA_W] * 4 + [DIL_W] * 4 + [CONV_CH] * 3)
(C_AQ, C_AK, C_AV, C_AG, C_BQ, C_BK, C_BV, C_BG, C_CA, C_CB, C_CG, C_END) = [int(c) for c in _C]

BF16 = jnp.bfloat16
F32 = jnp.float32


def _params(*sem):
    return pltpu.CompilerParams(dimension_semantics=sem, vmem_limit_bytes=VMEM_LIMIT)


def _silu(v):
    return v * (1.0 / (1.0 + jnp.exp(-v)))


def _lane_is_first_head(shape):
    return lax.broadcasted_iota(jnp.int32, shape, len(shape) - 1) < HEAD_DIM


IN_TM = 512


def _in_proj_kernel(x_ref, g_ref, w_ref, cos_ref, sa_ref, sb_ref,
                    naq_ref, nak_ref, nav_ref, dq_ref, dk_ref, dv_ref, u_ref, gate_ref):
    x = x_ref[...]
    ms = jnp.mean(x * x, axis=-1, keepdims=True)
    h = (x * lax.rsqrt(ms + NORM_EPS) * g_ref[...]).astype(BF16)

    def mm(c0, c1):
        return jnp.dot(h, w_ref[:, c0:c1], preferred_element_type=F32)

    cos = cos_ref[...]
    sa = sa_ref[...]
    sb = sb_ref[...]

    def rope(z):
        parts = []
        for s in range(z.shape[1] // LANES):
            zs = z[:, s * LANES:(s + 1) * LANES]
            up = pltpu.roll(zs, LANES - HEAD_DIM // 2, 1)
            dn = pltpu.roll(zs, HEAD_DIM // 2, 1)
            parts.append(zs * cos + up * sa + dn * sb)
        return jnp.concatenate(parts, axis=1)

    naq_ref[...] = (mm(C_AQ, C_AK) * SCALE).astype(BF16)
    nak_ref[...] = mm(C_AK, C_AV).astype(BF16)
    nav_ref[...] = mm(C_AV, C_AG).astype(BF16)
    dq_ref[...] = (rope(mm(C_BQ, C_BK)) * SCALE).astype(BF16)
    dk_ref[...] = rope(mm(C_BK, C_BV)).astype(BF16)
    dv_ref[...] = mm(C_BV, C_BG).astype(BF16)
    ca = mm(C_CA, C_CB)
    cb = mm(C_CB, C_CG)
    u_ref[...] = (ca * (1.0 / (1.0 + jnp.exp(-cb)))).astype(BF16)
    gate_ref[:, 0:NA_W] = _silu(mm(C_AG, C_BQ)).astype(BF16)
    gate_ref[:, NA_W:NA_W + DIL_W] = _silu(mm(C_BG, C_CA)).astype(BF16)
    gate_ref[:, NA_W + DIL_W:D_MIX] = _silu(mm(C_CG, C_END)).astype(BF16)


def _in_proj(x2d, g, w_bf16, cos_t, sa_t, sb_t, seq):
    n = x2d.shape[0]
    tm = IN_TM
    seq_blocks = seq // tm
    row = lambda i: (i, 0)
    tab = lambda i: (i % seq_blocks, 0)
    const = lambda i: (0, 0)
    out_shapes = [
        jax.ShapeDtypeStruct((n, NA_W), BF16), jax.ShapeDtypeStruct((n, NA_W), BF16),
        jax.ShapeDtypeStruct((n, NA_W), BF16),
        jax.ShapeDtypeStruct((n, DIL_W), BF16), jax.ShapeDtypeStruct((n, DIL_W), BF16),
        jax.ShapeDtypeStruct((n, DIL_W), BF16),
        jax.ShapeDtypeStruct((n, CONV_CH), BF16),
        jax.ShapeDtypeStruct((n, D_MIX), BF16),
    ]
    out_specs = [
        pl.BlockSpec((tm, NA_W), row), pl.BlockSpec((tm, NA_W), row), pl.BlockSpec((tm, NA_W), row),
        pl.BlockSpec((tm, DIL_W), row), pl.BlockSpec((tm, DIL_W), row), pl.BlockSpec((tm, DIL_W), row),
        pl.BlockSpec((tm, CONV_CH), row),
        pl.BlockSpec((tm, D_MIX), row),
    ]
    return pl.pallas_call(
        _in_proj_kernel,
        out_shape=out_shapes,
        grid=(n // tm,),
        in_specs=[
            pl.BlockSpec((tm, D_MODEL), row),
            pl.BlockSpec((1, D_MODEL), const),
            pl.BlockSpec((D_MODEL, C_END), const),
            pl.BlockSpec((tm, LANES), tab),
            pl.BlockSpec((tm, LANES), tab),
            pl.BlockSpec((tm, LANES), tab),
        ],
        out_specs=out_specs,
        compiler_params=_params("parallel"),
        name="in_proj",
    )(x2d, g, w_bf16, cos_t, sa_t, sb_t)


NA_ROWS_PER_STEP = 8
NA_KEYS = NA_WIN_ROWS * GRID_W


def _na_kernel(q_ref, k_ref, v_ref, bias_ref, o_ref, *, rows):
    blk = pl.program_id(1)
    first = _lane_is_first_head((GRID_W, LANES))
    for rr in range(NA_ROWS_PER_STEP):
        r = blk * NA_ROWS_PER_STEP + rr
        r0 = jnp.clip(r - NA_WIN_ROWS // 2, 0, rows - NA_WIN_ROWS)
        delta = r0 - r + (NA_WIN_ROWS - 1)
        start = pl.multiple_of(r0 * GRID_W, GRID_W)
        q_row = q_ref[0, rr * GRID_W:(rr + 1) * GRID_W, :]
        k_win = k_ref[0, pl.ds(start, NA_KEYS), :]
        v_win = v_ref[0, pl.ds(start, NA_KEYS), :]
        for p in range(NA_W // LANES):
            qp = q_row[:, p * LANES:(p + 1) * LANES]
            zero = jnp.zeros_like(qp)
            lhs = jnp.concatenate([jnp.where(first, qp, zero), jnp.where(first, zero, qp)], axis=0)
            kp = k_win[:, p * LANES:(p + 1) * LANES]
            vp = v_win[:, p * LANES:(p + 1) * LANES]
            s = lax.dot_general(lhs, kp, (((1,), (1,)), ((), ())), preferred_element_type=F32)
            s = s + bias_ref[delta, p]
            m = jnp.max(s, axis=-1, keepdims=True)
            e = jnp.exp(s - m)
            l = jnp.sum(e, axis=-1, keepdims=True)
            o = jnp.dot(e.astype(BF16), vp, preferred_element_type=F32) * (1.0 / l)
            out = jnp.where(first, o[0:GRID_W], o[GRID_W:2 * GRID_W])
            o_ref[0, rr * GRID_W:(rr + 1) * GRID_W, p * LANES:(p + 1) * LANES] = out.astype(BF16)


def _na_bias_table(rpb):
    cols = np.arange(GRID_W)
    col_start = np.clip(cols - NA_WIN_COLS // 2, 0, GRID_W - NA_WIN_COLS)
    kc = np.arange(GRID_W)
    in_win = (kc[None, :] >= col_start[:, None]) & (kc[None, :] < col_start[:, None] + NA_WIN_COLS)
    col_off = np.clip(kc[None, :] - cols[:, None] + (NA_WIN_COLS - 1), 0, 2 * NA_WIN_COLS - 2)
    delta = np.arange(NA_WIN_ROWS)
    row_off = delta[:, None] + np.arange(NA_WIN_ROWS)[None, :]
    t = rpb[:, row_off]
    t = t[:, :, :, col_off]
    t = jnp.where(jnp.asarray(in_win)[None, None, None], t, NEG_INF)
    t = t.transpose(1, 0, 3, 2, 4)
    return t.reshape(NA_WIN_ROWS, NA_HEADS // 2, 2 * GRID_W, NA_KEYS).astype(F32)


def _na_attention(q, k, v, bias, batch, seq):
    rows = seq // GRID_W
    tq = NA_ROWS_PER_STEP * GRID_W
    q3 = q.reshape(batch, seq, NA_W)
    k3 = k.reshape(batch, seq, NA_W)
    v3 = v.reshape(batch, seq, NA_W)
    out = pl.pallas_call(
        functools.partial(_na_kernel, rows=rows),
        out_shape=jax.ShapeDtypeStruct((batch, seq, NA_W), BF16),
        grid=(batch, seq // tq),
        in_specs=[
            pl.BlockSpec((1, tq, NA_W), lambda b, i: (b, i, 0)),
            pl.BlockSpec((1, seq, NA_W), lambda b, i: (b, 0, 0)),
            pl.BlockSpec((1, seq, NA_W), lambda b, i: (b, 0, 0)),
            pl.BlockSpec(bias.shape, lambda b, i: (0, 0, 0, 0)),
        ],
        out_specs=pl.BlockSpec((1, tq, NA_W), lambda b, i: (b, i, 0)),
        compiler_params=_params("parallel", "arbitrary"),
        name="na_attention",
    )(q3, k3, v3, bias)
    return out.reshape(batch * seq, NA_W)


BAND_CHUNK = 128
BAND_KEYS = BAND_CHUNK + 2 * DIL_HALF
BAND_TQ = 512


def _band_kernel(q_ref, kp_ref, kc_ref, kn_ref, vp_ref, vc_ref, vn_ref, o_ref, lse_ref, *, length):
    blk = pl.program_id(1)
    n_chunks = BAND_TQ // BAND_CHUNK
    first = _lane_is_first_head((BAND_CHUNK, LANES))
    qi = lax.broadcasted_iota(jnp.int32, (BAND_CHUNK, BAND_KEYS), 0)
    kj = lax.broadcasted_iota(jnp.int32, (BAND_CHUNK, BAND_KEYS), 1)
    rel = kj - DIL_HALF - qi
    band = (rel >= -DIL_HALF) & (rel <= DIL_HALF)
    for c in range(n_chunks):
        a0 = blk * BAND_TQ + c * BAND_CHUNK
        kpos = a0 - DIL_HALF + kj
        ok = band & (kpos >= 0) & (kpos < length)
        mask_bias = jnp.where(ok, 0.0, NEG_INF).astype(F32)
        mask_bias2 = jnp.concatenate([mask_bias, mask_bias], axis=0)
        lo = c * BAND_CHUNK - DIL_HALF
        hi = lo + BAND_KEYS

        def window(prev_ref, cur_ref, next_ref):
            pieces = []
            if lo < 0:
                pieces.append(prev_ref[0, :, :])
            pieces.append(cur_ref[0, max(lo, 0):min(hi, BAND_TQ), :])
            if hi > BAND_TQ:
                pieces.append(next_ref[0, :, :])
            return pieces[0] if len(pieces) == 1 else jnp.concatenate(pieces, axis=0)

        k_win = window(kp_ref, kc_ref, kn_ref)
        v_win = window(vp_ref, vc_ref, vn_ref)
        q_chunk = q_ref[0, c * BAND_CHUNK:(c + 1) * BAND_CHUNK, :]
        for p in range(DIL_W // LANES):
            qp = q_chunk[:, p * LANES:(p + 1) * LANES]
            zero = jnp.zeros_like(qp)
            lhs = jnp.concatenate([jnp.where(first, qp, zero), jnp.where(first, zero, qp)], axis=0)
            kp = k_win[:, p * LANES:(p + 1) * LANES]
            vp = v_win[:, p * LANES:(p + 1) * LANES]
            s = lax.dot_general(lhs, kp, (((1,), (1,)), ((), ())), preferred_element_type=F32)
            s = s + mask_bias2
            m = jnp.max(s, axis=-1, keepdims=True)
            e = jnp.exp(s - m)
            l = jnp.sum(e, axis=-1, keepdims=True)
            o = jnp.dot(e.astype(BF16), vp, preferred_element_type=F32) * (1.0 / l)
            lse = m + jnp.log(l)
            rows = slice(c * BAND_CHUNK, (c + 1) * BAND_CHUNK)
            lanes = slice(p * LANES, (p + 1) * LANES)
            o_ref[0, rows, lanes] = jnp.where(first, o[0:BAND_CHUNK], o[BAND_CHUNK:])
            lse_ref[0, rows, lanes] = jnp.where(
                first,
                jnp.broadcast_to(lse[0:BAND_CHUNK], (BAND_CHUNK, LANES)),
                jnp.broadcast_to(lse[BAND_CHUNK:], (BAND_CHUNK, LANES)))


def _band_attention(q, k, v):
    groups, length, _ = q.shape
    halo_per_blk = BAND_TQ // DIL_HALF
    n_halo = length // DIL_HALF
    cur = lambda g, i: (g, i, 0)
    prev = lambda g, i: (g, jnp.maximum(i * halo_per_blk - 1, 0), 0)
    nxt = lambda g, i: (g, jnp.minimum((i + 1) * halo_per_blk, n_halo - 1), 0)
    blk = pl.BlockSpec((1, BAND_TQ, DIL_W), cur)
    halo_p = pl.BlockSpec((1, DIL_HALF, DIL_W), prev)
    halo_n = pl.BlockSpec((1, DIL_HALF, DIL_W), nxt)
    return pl.pallas_call(
        functools.partial(_band_kernel, length=length),
        out_shape=[jax.ShapeDtypeStruct((groups, length, DIL_W), F32),
                   jax.ShapeDtypeStruct((groups, length, DIL_W), F32)],
        grid=(groups, length // BAND_TQ),
        in_specs=[blk, halo_p, blk, halo_n, halo_p, blk, halo_n],
        out_specs=[pl.BlockSpec((1, BAND_TQ, DIL_W), cur), pl.BlockSpec((1, BAND_TQ, DIL_W), cur)],
        compiler_params=_params("parallel", "arbitrary"),
        name="band_attention",
    )(q, k, k, k, v, v, v)


def _deinterleave(t, batch, seq, d):
    w = t.shape[-1]
    return t.reshape(batch, seq // d, d, w).transpose(0, 2, 1, 3).reshape(batch * d, seq // d, w)


def _interleave(t, batch, seq, d):
    w = t.shape[-1]
    return t.reshape(batch, d, seq // d, w).transpose(0, 2, 1, 3).reshape(batch * seq, w)


CONV_T = 512
CONV_HALO = 16
CONV_PAD = CONV_WIDTH // 2


def _conv_kernel(up_ref, uc_ref, un_ref, w_ref, b_ref, lg_ref, lb_ref, pw_ref, pb_ref, o_ref, ubuf):
    i = pl.program_id(1)
    n = pl.num_programs(1)
    prev_scale = jnp.where(i > 0, 1.0, 0.0).astype(F32)
    next_scale = jnp.where(i < n - 1, 1.0, 0.0).astype(F32)
    ubuf[0:CONV_HALO, :] = up_ref[0].astype(F32) * prev_scale
    ubuf[CONV_HALO:CONV_HALO + CONV_T, :] = uc_ref[0].astype(F32)
    ubuf[CONV_HALO + CONV_T:, :] = un_ref[0].astype(F32) * next_scale
    acc = jnp.zeros((CONV_T, CONV_CH), F32)
    for k in range(CONV_WIDTH):
        off = CONV_HALO - CONV_PAD + k
        acc = acc + ubuf[off:off + CONV_T, :] * w_ref[k:k + 1, :]
    y = acc + b_ref[...]
    mu = jnp.mean(y, axis=-1, keepdims=True)
    yc = y - mu
    var = jnp.mean(yc * yc, axis=-1, keepdims=True)
    yn = yc * lax.rsqrt(var + NORM_EPS) * lg_ref[...] + lb_ref[...]
    act = _silu(yn).astype(BF16)
    o_ref[0] = (jnp.dot(act, pw_ref[...], preferred_element_type=F32) + pb_ref[...]).astype(BF16)


def _conv_module(u, w_dw, b_dw, ln_g, ln_b, pw_bf16, pw_b, batch, seq):
    u3 = u.reshape(batch, seq, CONV_CH)
    per = CONV_T // CONV_HALO
    n_halo = seq // CONV_HALO
    cur = lambda b, i: (b, i, 0)
    prev = lambda b, i: (b, jnp.maximum(i * per - 1, 0), 0)
    nxt = lambda b, i: (b, jnp.minimum((i + 1) * per, n_halo - 1), 0)
    const = lambda b, i: (0, 0)
    vec = pl.BlockSpec((1, CONV_CH), const)
    out = pl.pallas_call(
        _conv_kernel,
        out_shape=jax.ShapeDtypeStruct((batch, seq, CONV_CH), BF16),
        grid=(batch, seq // CONV_T),
        in_specs=[
            pl.BlockSpec((1, CONV_HALO, CONV_CH), prev),
            pl.BlockSpec((1, CONV_T, CONV_CH), cur),
            pl.BlockSpec((1, CONV_HALO, CONV_CH), nxt),
            pl.BlockSpec((CONV_WIDTH, CONV_CH), const),
            vec, vec, vec,
            pl.BlockSpec((CONV_CH, CONV_CH), const),
            vec,
        ],
        out_specs=pl.BlockSpec((1, CONV_T, CONV_CH), cur),
        scratch_shapes=[pltpu.VMEM((CONV_T + 2 * CONV_HALO, CONV_CH), F32)],
        compiler_params=_params("parallel", "arbitrary"),
        name="conv_module",
    )(u3, u3, u3, w_dw, b_dw, ln_g, ln_b, pw_bf16, pw_b)
    return out.reshape(batch * seq, CONV_CH)


OUT_TM = 512


def _out_proj_kernel(ya_ref, o1_ref, o4_ref, o16_ref, l1_ref, l4_ref, l16_ref, yc_ref, gate_ref,
                     x_ref, w_ref, fg_ref, out_ref, *, final_norm):
    l1 = l1_ref[...]
    l4 = l4_ref[...]
    l16 = l16_ref[...]
    m = jnp.maximum(jnp.maximum(l1, l4), l16)
    w1 = jnp.exp(l1 - m)
    w4 = jnp.exp(l4 - m)
    w16 = jnp.exp(l16 - m)
    yb = (w1 * o1_ref[...] + w4 * o4_ref[...] + w16 * o16_ref[...]) * (1.0 / (w1 + w4 + w16))
    gate = gate_ref[...].astype(F32)
    mix = jnp.concatenate([ya_ref[...].astype(F32), yb, yc_ref[...].astype(F32)], axis=1) * gate
    y = x_ref[...] + jnp.dot(mix.astype(BF16), w_ref[...], preferred_element_type=F32)
    if final_norm:
        ms = jnp.mean(y * y, axis=-1, keepdims=True)
        y = y * lax.rsqrt(ms + NORM_EPS) * fg_ref[...]
    out_ref[...] = y


def _out_proj(ya, o1, o4, o16, l1, l4, l16, yc, gate, x2d, w_bf16, final_g, final_norm):
    n = x2d.shape[0]
    tm = OUT_TM
    row = lambda i: (i, 0)
    const = lambda i: (0, 0)
    dil = pl.BlockSpec((tm, DIL_W), row)
    return pl.pallas_call(
        functools.partial(_out_proj_kernel, final_norm=final_norm),
        out_shape=jax.ShapeDtypeStruct((n, D_MODEL), F32),
        grid=(n // tm,),
        in_specs=[
            pl.BlockSpec((tm, NA_W), row),
            dil, dil, dil, dil, dil, dil,
            pl.BlockSpec((tm, CONV_CH), row),
            pl.BlockSpec((tm, D_MIX), row),
            pl.BlockSpec((tm, D_MODEL), row),
            pl.BlockSpec((D_MIX, D_MODEL), const),
            pl.BlockSpec((1, D_MODEL), const),
        ],
        out_specs=pl.BlockSpec((tm, D_MODEL), row),
        compiler_params=_params("parallel"),
        name="out_proj",
    )(ya, o1, o4, o16, l1, l4, l16, yc, gate, x2d, w_bf16, final_g)


def _rotary_tables(seq):
    half = HEAD_DIM // 2
    inv_freq = jnp.power(ROPE_THETA, -jnp.arange(half, dtype=F32) * 2.0 / HEAD_DIM)
    ang = jnp.arange(seq).astype(F32)[:, None] * inv_freq[None, :]
    cos = jnp.cos(ang)
    sin = jnp.sin(ang)
    zero = jnp.zeros_like(sin)
    reps = LANES // HEAD_DIM
    cos_t = jnp.tile(jnp.concatenate([cos, cos], axis=1), (1, reps))
    sa_t = jnp.tile(jnp.concatenate([-sin, zero], axis=1), (1, reps))
    sb_t = jnp.tile(jnp.concatenate([zero, sin], axis=1), (1, reps))
    return cos_t, sa_t, sb_t


def kernel(x, norm_g, w_in, na_rpb, conv_w, conv_b, conv_ln_g, conv_ln_b, pw_w, pw_b, w_out, final_g):
    batch, seq, _ = x.shape
    depth = w_in.shape[0]
    cos_t, sa_t, sb_t = _rotary_tables(seq)
    h = x.reshape(batch * seq, D_MODEL)
    fg = final_g.reshape(1, D_MODEL)
    for l in range(depth):
        naq, nak, nav, dq, dk, dv, u, gate = _in_proj(
            h, norm_g[l].reshape(1, D_MODEL), w_in[l].astype(BF16), cos_t, sa_t, sb_t, seq)
        ya = _na_attention(naq, nak, nav, _na_bias_table(na_rpb[l]), batch, seq)
        outs, lses = [], []
        for d in DIL_DILATIONS:
            o, lse = _band_attention(_deinterleave(dq, batch, seq, d), _deinterleave(dk, batch, seq, d),
                                     _deinterleave(dv, batch, seq, d))
            outs.append(_interleave(o, batch, seq, d))
            lses.append(_interleave(lse, batch, seq, d))
        yc = _conv_module(u, conv_w[l], conv_b[l].reshape(1, CONV_CH), conv_ln_g[l].reshape(1, CONV_CH),
                          conv_ln_b[l].reshape(1, CONV_CH), pw_w[l].astype(BF16),
                          pw_b[l].reshape(1, CONV_CH), batch, seq)
        h = _out_proj(ya, outs[0], outs[1], outs[2], lses[0], lses[1], lses[2], yc, gate, h,
                      w_out[l].astype(BF16), fg, final_norm=(l == depth - 1))
    return h.reshape(batch, seq, D_MODEL)
```

```python
import functools

import numpy as np
import jax
import jax.numpy as jnp
from jax import lax
from jax.experimental import pallas as pl
from jax.experimental.pallas import tpu as pltpu

D_MODEL = 1024
HEAD_DIM = 64
GRID_W = 64
NA_HEADS = 4
NA_WIN_ROWS = 8
NA_WIN_COLS = 16
DIL_HEADS = 6
DIL_DILATIONS = (1, 4, 16)
DIL_HALF = 64
CONV_CH = 384
CONV_WIDTH = 31
ROPE_THETA = 10000.0
NORM_EPS = 1e-6
NEG_INF = -1e30
NA_W = NA_HEADS * HEAD_DIM
DIL_W = DIL_HEADS * HEAD_DIM
D_MIX = NA_W + DIL_W + CONV_CH
SCALE = HEAD_DIM ** -0.5

LANES = 128
SUBLANES = 8
VMEM_LIMIT = 56 * 1024 * 1024
DIL_SLABS = DIL_W // LANES

_C = np.cumsum([0] + [NA_W] * 4 + [DIL_W] * 4 + [CONV_CH] * 3)
(C_AQ, C_AK, C_AV, C_AG, C_BQ, C_BK, C_BV, C_BG, C_CA, C_CB, C_CG, C_END) = [int(c) for c in _C]

BF16 = jnp.bfloat16
F32 = jnp.float32

TOKEN_TILE = 512


def _params(*sem):
    return pltpu.CompilerParams(dimension_semantics=sem, vmem_limit_bytes=VMEM_LIMIT)


def _silu(v):
    return v * (1.0 / (1.0 + jnp.exp(-v)))


def _lane_is_first_head(shape):
    return lax.broadcasted_iota(jnp.int32, shape, len(shape) - 1) < HEAD_DIM


def _in_proj_kernel(x_ref, g_ref, w_ref, cos_ref, sa_ref, sb_ref,
                    naq_ref, nak_ref, nav_ref,
                    q1_ref, q4_ref, q16_ref, k1_ref, k4_ref, k16_ref, v1_ref, v4_ref, v16_ref,
                    u_ref, gate_ref, perm):
    tm = x_ref.shape[0]
    x = x_ref[...]
    ms = jnp.mean(x * x, axis=-1, keepdims=True)
    h = (x * lax.rsqrt(ms + NORM_EPS) * g_ref[...]).astype(BF16)

    def mm(c0, c1):
        return jnp.dot(h, w_ref[:, c0:c1], preferred_element_type=F32)

    cos = cos_ref[...]
    sa = sa_ref[...]
    sb = sb_ref[...]

    def rope(zs):
        up = pltpu.roll(zs, LANES - HEAD_DIM // 2, 1)
        dn = pltpu.roll(zs, HEAD_DIM // 2, 1)
        return zs * cos + up * sa + dn * sb

    def emit_dilated(z, refs, rotary, scale):
        ref1, ref4, ref16 = refs
        for s in range(DIL_SLABS):
            zs = z[:, s * LANES:(s + 1) * LANES]
            if rotary:
                zs = rope(zs)
            if scale != 1.0:
                zs = zs * scale
            perm[s] = zs
            ref1[:, s * LANES:(s + 1) * LANES] = zs.astype(BF16)
        for d, ref in ((4, ref4), (16, ref16)):
            for rho in range(d):
                for s in range(DIL_SLABS):
                    ref[0, rho, :, s * LANES:(s + 1) * LANES] = (
                        perm[s, pl.ds(rho, tm // d, stride=d), :].astype(BF16))

    za = mm(C_AQ, C_BQ)
    naq_ref[...] = (za[:, 0:NA_W] * SCALE).astype(BF16)
    nak_ref[...] = za[:, NA_W:2 * NA_W].astype(BF16)
    nav_ref[...] = za[:, 2 * NA_W:3 * NA_W].astype(BF16)
    gate_ref[:, 0:NA_W] = _silu(za[:, 3 * NA_W:4 * NA_W]).astype(BF16)

    zqk = mm(C_BQ, C_BV)
    emit_dilated(zqk[:, 0:DIL_W], (q1_ref, q4_ref, q16_ref), True, SCALE)
    emit_dilated(zqk[:, DIL_W:2 * DIL_W], (k1_ref, k4_ref, k16_ref), True, 1.0)

    zvg = mm(C_BV, C_CA)
    emit_dilated(zvg[:, 0:DIL_W], (v1_ref, v4_ref, v16_ref), False, 1.0)
    gate_ref[:, NA_W:NA_W + DIL_W] = _silu(zvg[:, DIL_W:2 * DIL_W]).astype(BF16)

    zc = mm(C_CA, C_CG)
    u_ref[...] = (zc[:, 0:CONV_CH] * (1.0 / (1.0 + jnp.exp(-zc[:, CONV_CH:2 * CONV_CH])))).astype(BF16)
    gate_ref[:, NA_W + DIL_W:D_MIX] = _silu(mm(C_CG, C_END)).astype(BF16)


def _in_proj(x2d, g, w_bf16, cos_t, sa_t, sb_t, batch, seq):
    n = x2d.shape[0]
    tm = TOKEN_TILE
    seq_blocks = seq // tm
    row = lambda i: (i, 0)
    tab = lambda i: (i % seq_blocks, 0)
    const = lambda i: (0, 0)
    perm_idx = lambda i: (i // seq_blocks, 0, i % seq_blocks, 0)

    def dil_shapes():
        return [jax.ShapeDtypeStruct((n, DIL_W), BF16)] + [
            jax.ShapeDtypeStruct((batch, d, seq // d, DIL_W), BF16) for d in DIL_DILATIONS[1:]]

    def dil_specs():
        return [pl.BlockSpec((tm, DIL_W), row)] + [
            pl.BlockSpec((1, d, tm // d, DIL_W), perm_idx) for d in DIL_DILATIONS[1:]]

    out_shapes = ([jax.ShapeDtypeStruct((n, NA_W), BF16)] * 3 + dil_shapes() + dil_shapes() + dil_shapes()
                  + [jax.ShapeDtypeStruct((n, CONV_CH), BF16), jax.ShapeDtypeStruct((n, D_MIX), BF16)])
    out_specs = ([pl.BlockSpec((tm, NA_W), row)] * 3 + dil_specs() + dil_specs() + dil_specs()
                 + [pl.BlockSpec((tm, CONV_CH), row), pl.BlockSpec((tm, D_MIX), row)])
    return pl.pallas_call(
        _in_proj_kernel,
        out_shape=out_shapes,
        grid=(n // tm,),
        in_specs=[
            pl.BlockSpec((tm, D_MODEL), row),
            pl.BlockSpec((1, D_MODEL), const),
            pl.BlockSpec((D_MODEL, C_END), const),
            pl.BlockSpec((tm, LANES), tab),
            pl.BlockSpec((tm, LANES), tab),
            pl.BlockSpec((tm, LANES), tab),
        ],
        out_specs=out_specs,
        scratch_shapes=[pltpu.VMEM((DIL_SLABS, tm, LANES), F32)],
        compiler_params=_params("parallel"),
        name="in_proj",
    )(x2d, g, w_bf16, cos_t, sa_t, sb_t)


NA_ROWS_PER_STEP = 8
NA_KEYS = NA_WIN_ROWS * GRID_W


def _na_kernel(q_ref, k_ref, v_ref, bias_ref, o_ref, *, rows):
    blk = pl.program_id(1)
    first = _lane_is_first_head((GRID_W, LANES))
    for rr in range(NA_ROWS_PER_STEP):
        r = blk * NA_ROWS_PER_STEP + rr
        r0 = jnp.clip(r - NA_WIN_ROWS // 2, 0, rows - NA_WIN_ROWS)
        delta = r0 - r + (NA_WIN_ROWS - 1)
        start = pl.multiple_of(r0 * GRID_W, GRID_W)
        q_row = q_ref[0, rr * GRID_W:(rr + 1) * GRID_W, :]
        k_win = k_ref[0, pl.ds(start, NA_KEYS), :]
        v_win = v_ref[0, pl.ds(start, NA_KEYS), :]
        for p in range(NA_W // LANES):
            qp = q_row[:, p * LANES:(p + 1) * LANES]
            zero = jnp.zeros_like(qp)
            lhs = jnp.concatenate([jnp.where(first, qp, zero), jnp.where(first, zero, qp)], axis=0)
            kp = k_win[:, p * LANES:(p + 1) * LANES]
            vp = v_win[:, p * LANES:(p + 1) * LANES]
            s = lax.dot_general(lhs, kp, (((1,), (1,)), ((), ())), preferred_element_type=F32)
            s = s + bias_ref[delta, p]
            m = jnp.max(s, axis=-1, keepdims=True)
            e = jnp.exp(s - m)
            l = jnp.sum(e, axis=-1, keepdims=True)
            o = jnp.dot(e.astype(BF16), vp, preferred_element_type=F32) * (1.0 / l)
            out = jnp.where(first, o[0:GRID_W], o[GRID_W:2 * GRID_W])
            o_ref[0, rr * GRID_W:(rr + 1) * GRID_W, p * LANES:(p + 1) * LANES] = out.astype(BF16)


def _na_bias_table(rpb):
    cols = np.arange(GRID_W)
    col_start = np.clip(cols - NA_WIN_COLS // 2, 0, GRID_W - NA_WIN_COLS)
    kc = np.arange(GRID_W)
    in_win = (kc[None, :] >= col_start[:, None]) & (kc[None, :] < col_start[:, None] + NA_WIN_COLS)
    col_off = kc[None, :] - cols[:, None] + (NA_WIN_COLS - 1)
    n_off = 2 * NA_WIN_COLS - 1
    onehot = (col_off[None] == np.arange(n_off)[:, None, None]) & in_win[None]
    t = jnp.einsum('hrj,jck->hrck', rpb.astype(F32), jnp.asarray(onehot, F32),
                   precision=lax.Precision.HIGHEST)
    t = jnp.where(jnp.asarray(in_win)[None, None], t, NEG_INF)
    t = jnp.stack([t[:, dlt:dlt + NA_WIN_ROWS] for dlt in range(NA_WIN_ROWS)], axis=0)
    t = t.transpose(0, 1, 3, 2, 4)
    return t.reshape(NA_WIN_ROWS, NA_HEADS // 2, 2 * GRID_W, NA_KEYS)


def _na_attention(q, k, v, bias, batch, seq):
    rows = seq // GRID_W
    tq = NA_ROWS_PER_STEP * GRID_W
    q3 = q.reshape(batch, seq, NA_W)
    k3 = k.reshape(batch, seq, NA_W)
    v3 = v.reshape(batch, seq, NA_W)
    out = pl.pallas_call(
        functools.partial(_na_kernel, rows=rows),
        out_shape=jax.ShapeDtypeStruct((batch, seq, NA_W), BF16),
        grid=(batch, seq // tq),
        in_specs=[
            pl.BlockSpec((1, tq, NA_W), lambda b, i: (b, i, 0)),
            pl.BlockSpec((1, seq, NA_W), lambda b, i: (b, 0, 0)),
            pl.BlockSpec((1, seq, NA_W), lambda b, i: (b, 0, 0)),
            pl.BlockSpec(bias.shape, lambda b, i: (0, 0, 0, 0)),
        ],
        out_specs=pl.BlockSpec((1, tq, NA_W), lambda b, i: (b, i, 0)),
        compiler_params=_params("parallel", "arbitrary"),
        name="na_attention",
    )(q3, k3, v3, bias)
    return out.reshape(batch * seq, NA_W)


BAND_CHUNK = 128
BAND_KEYS = BAND_CHUNK + 2 * DIL_HALF
BAND_TQ = 512


def _band_kernel(q_ref, kp_ref, kc_ref, kn_ref, vp_ref, vc_ref, vn_ref, o_ref, lse_ref, *, length):
    blk = pl.program_id(1)
    n_chunks = BAND_TQ // BAND_CHUNK
    first = _lane_is_first_head((BAND_CHUNK, LANES))
    lane = lax.broadcasted_iota(jnp.int32, (BAND_CHUNK, LANES), 1)
    qi = lax.broadcasted_iota(jnp.int32, (BAND_CHUNK, BAND_KEYS), 0)
    kj = lax.broadcasted_iota(jnp.int32, (BAND_CHUNK, BAND_KEYS), 1)
    rel = kj - DIL_HALF - qi
    band = (rel >= -DIL_HALF) & (rel <= DIL_HALF)
    for c in range(n_chunks):
        a0 = blk * BAND_TQ + c * BAND_CHUNK
        kpos = a0 - DIL_HALF + kj
        ok = band & (kpos >= 0) & (kpos < length)
        mask_bias = jnp.where(ok, 0.0, NEG_INF).astype(F32)
        mask_bias2 = jnp.concatenate([mask_bias, mask_bias], axis=0)
        lo = c * BAND_CHUNK - DIL_HALF
        hi = lo + BAND_KEYS

        def window(prev_ref, cur_ref, next_ref):
            pieces = []
            if lo < 0:
                pieces.append(prev_ref[0, :, :])
            pieces.append(cur_ref[0, max(lo, 0):min(hi, BAND_TQ), :])
            if hi > BAND_TQ:
                pieces.append(next_ref[0, :, :])
            return pieces[0] if len(pieces) == 1 else jnp.concatenate(pieces, axis=0)

        k_win = window(kp_ref, kc_ref, kn_ref)
        v_win = window(vp_ref, vc_ref, vn_ref)
        q_chunk = q_ref[0, c * BAND_CHUNK:(c + 1) * BAND_CHUNK, :]
        rows = slice(c * BAND_CHUNK, (c + 1) * BAND_CHUNK)
        lse_tile = jnp.zeros((BAND_CHUNK, LANES), F32)
        for p in range(DIL_SLABS):
            qp = q_chunk[:, p * LANES:(p + 1) * LANES]
            zero = jnp.zeros_like(qp)
            lhs = jnp.concatenate([jnp.where(first, qp, zero), jnp.where(first, zero, qp)], axis=0)
            kp = k_win[:, p * LANES:(p + 1) * LANES]
            vp = v_win[:, p * LANES:(p + 1) * LANES]
            s = lax.dot_general(lhs, kp, (((1,), (1,)), ((), ())), preferred_element_type=F32)
            s = s + mask_bias2
            m = jnp.max(s, axis=-1, keepdims=True)
            e = jnp.exp(s - m)
            l = jnp.sum(e, axis=-1, keepdims=True)
            o = jnp.dot(e.astype(BF16), vp, preferred_element_type=F32) * (1.0 / l)
            lse = m + jnp.log(l)
            o_ref[0, rows, p * LANES:(p + 1) * LANES] = jnp.where(
                first, o[0:BAND_CHUNK], o[BAND_CHUNK:]).astype(BF16)
            lse_tile = jnp.where(lane == 2 * p, lse[0:BAND_CHUNK], lse_tile)
            lse_tile = jnp.where(lane == 2 * p + 1, lse[BAND_CHUNK:], lse_tile)
        lse_ref[0, rows, :] = lse_tile


def _band_attention(q, k, v):
    groups, length, _ = q.shape
    halo_per_blk = BAND_TQ // DIL_HALF
    n_halo = length // DIL_HALF
    cur = lambda g, i: (g, i, 0)
    prev = lambda g, i: (g, jnp.maximum(i * halo_per_blk - 1, 0), 0)
    nxt = lambda g, i: (g, jnp.minimum((i + 1) * halo_per_blk, n_halo - 1), 0)
    blk = pl.BlockSpec((1, BAND_TQ, DIL_W), cur)
    halo_p = pl.BlockSpec((1, DIL_HALF, DIL_W), prev)
    halo_n = pl.BlockSpec((1, DIL_HALF, DIL_W), nxt)
    return pl.pallas_call(
        functools.partial(_band_kernel, length=length),
        out_shape=[jax.ShapeDtypeStruct((groups, length, DIL_W), BF16),
                   jax.ShapeDtypeStruct((groups, length, LANES), F32)],
        grid=(groups, length // BAND_TQ),
        in_specs=[blk, halo_p, blk, halo_n, halo_p, blk, halo_n],
        out_specs=[pl.BlockSpec((1, BAND_TQ, DIL_W), cur), pl.BlockSpec((1, BAND_TQ, LANES), cur)],
        compiler_params=_params("parallel", "arbitrary"),
        name="band_attention",
    )(q, k, k, k, v, v, v)


CONV_T = 512
CONV_HALO = 16
CONV_PAD = CONV_WIDTH // 2
CONV_ROWS = 128


def _conv_kernel(up_ref, uc_ref, un_ref, w_ref, b_ref, lg_ref, lb_ref, pw_ref, pb_ref, o_ref, ubuf, ybuf):
    i = pl.program_id(1)
    n = pl.num_programs(1)
    prev_scale = jnp.where(i > 0, 1.0, 0.0).astype(F32)
    next_scale = jnp.where(i < n - 1, 1.0, 0.0).astype(F32)
    ubuf[0:CONV_HALO, :] = up_ref[0].astype(F32) * prev_scale
    ubuf[CONV_HALO:CONV_HALO + CONV_T, :] = uc_ref[0].astype(F32)
    ubuf[CONV_HALO + CONV_T:, :] = un_ref[0].astype(F32) * next_scale

    first_off = CONV_HALO - CONV_PAD

    def body(c, carry):
        r0 = pl.multiple_of(c * CONV_ROWS, CONV_ROWS)
        for sl in range(CONV_CH // LANES):
            lanes = slice(sl * LANES, (sl + 1) * LANES)
            y = None
            for s in range(SUBLANES):
                part = None
                for k in range(CONV_WIDTH):
                    off = first_off + k
                    if off % SUBLANES != s:
                        continue
                    win = ubuf[pl.ds(r0 + (off - s), CONV_ROWS + SUBLANES), lanes]
                    term = win * w_ref[k:k + 1, lanes]
                    part = term if part is None else part + term
                shifted = part[s:s + CONV_ROWS, :]
                y = shifted if y is None else y + shifted
            ybuf[pl.ds(r0, CONV_ROWS), lanes] = y
        return carry

    lax.fori_loop(0, CONV_T // CONV_ROWS, body, 0)

    y = ybuf[...] + b_ref[...]
    mu = jnp.mean(y, axis=-1, keepdims=True)
    yc = y - mu
    var = jnp.mean(yc * yc, axis=-1, keepdims=True)
    yn = yc * lax.rsqrt(var + NORM_EPS) * lg_ref[...] + lb_ref[...]
    act = _silu(yn).astype(BF16)
    o_ref[0] = (jnp.dot(act, pw_ref[...], preferred_element_type=F32) + pb_ref[...]).astype(BF16)


def _conv_module(u, w_dw, b_dw, ln_g, ln_b, pw_bf16, pw_b, batch, seq):
    u3 = u.reshape(batch, seq, CONV_CH)
    per = CONV_T // CONV_HALO
    n_halo = seq // CONV_HALO
    cur = lambda b, i: (b, i, 0)
    prev = lambda b, i: (b, jnp.maximum(i * per - 1, 0), 0)
    nxt = lambda b, i: (b, jnp.minimum((i + 1) * per, n_halo - 1), 0)
    const = lambda b, i: (0, 0)
    vec = pl.BlockSpec((1, CONV_CH), const)
    out = pl.pallas_call(
        _conv_kernel,
        out_shape=jax.ShapeDtypeStruct((batch, seq, CONV_CH), BF16),
        grid=(batch, seq // CONV_T),
        in_specs=[
            pl.BlockSpec((1, CONV_HALO, CONV_CH), prev),
            pl.BlockSpec((1, CONV_T, CONV_CH), cur),
            pl.BlockSpec((1, CONV_HALO, CONV_CH), nxt),
            pl.BlockSpec((CONV_WIDTH, CONV_CH), const),
            vec, vec, vec,
            pl.BlockSpec((CONV_CH, CONV_CH), const),
            vec,
        ],
        out_specs=pl.BlockSpec((1, CONV_T, CONV_CH), cur),
        scratch_shapes=[pltpu.VMEM((CONV_T + 2 * CONV_HALO, CONV_CH), F32),
                        pltpu.VMEM((CONV_T, CONV_CH), F32)],
        compiler_params=_params("parallel", "arbitrary"),
        name="conv_module",
    )(u3, u3, u3, w_dw, b_dw, ln_g, ln_b, pw_bf16, pw_b)
    return out.reshape(batch * seq, CONV_CH)


def _out_proj_kernel(ya_ref, o1_ref, o4_ref, o16_ref, l1_ref, l4_ref, l16_ref, yc_ref, gate_ref,
                     x_ref, w_ref, fg_ref, out_ref, onat, lnat, *, final_norm):
    tm = x_ref.shape[0]
    first = _lane_is_first_head((tm, LANES))

    def natural_stats(l_ref, d, slot):
        if d == 1:
            return l_ref[...]
        for rho in range(d):
            lnat[slot, pl.ds(rho, tm // d, stride=d), :] = l_ref[0, rho]
        return lnat[slot]

    l1 = natural_stats(l1_ref, 1, 0)
    l4 = natural_stats(l4_ref, 4, 0)
    l16 = natural_stats(l16_ref, 16, 1)
    m = jnp.maximum(jnp.maximum(l1, l4), l16)
    e1 = jnp.exp(l1 - m)
    e4 = jnp.exp(l4 - m)
    e16 = jnp.exp(l16 - m)
    inv = 1.0 / (e1 + e4 + e16)
    weights = (e1 * inv, e4 * inv, e16 * inv)

    for slot, (d, o_ref) in enumerate(((4, o4_ref), (16, o16_ref))):
        for rho in range(d):
            for s in range(DIL_SLABS):
                onat[slot, s, pl.ds(rho, tm // d, stride=d), :] = (
                    o_ref[0, rho, :, s * LANES:(s + 1) * LANES].astype(F32))

    gate = gate_ref[...].astype(F32)
    pieces = [ya_ref[...].astype(F32) * gate[:, 0:NA_W]]
    for s in range(DIL_SLABS):
        outs = (o1_ref[:, s * LANES:(s + 1) * LANES].astype(F32), onat[0, s], onat[1, s])
        yb = None
        for w, o in zip(weights, outs):
            wb = jnp.where(first, w[:, 2 * s:2 * s + 1], w[:, 2 * s + 1:2 * s + 2])
            yb = wb * o if yb is None else yb + wb * o
        pieces.append(yb * gate[:, NA_W + s * LANES:NA_W + (s + 1) * LANES])
    pieces.append(yc_ref[...].astype(F32) * gate[:, NA_W + DIL_W:D_MIX])
    mix = jnp.concatenate(pieces, axis=1).astype(BF16)
    y = x_ref[...] + jnp.dot(mix, w_ref[...], preferred_element_type=F32)
    if final_norm:
        ms = jnp.mean(y * y, axis=-1, keepdims=True)
        y = y * lax.rsqrt(ms + NORM_EPS) * fg_ref[...]
    out_ref[...] = y


def _out_proj(ya, outs, lses, yc, gate, x2d, w_bf16, final_g, final_norm, batch, seq):
    n = x2d.shape[0]
    tm = TOKEN_TILE
    seq_blocks = seq // tm
    row = lambda i: (i, 0)
    const = lambda i: (0, 0)
    perm_idx = lambda i: (i // seq_blocks, 0, i % seq_blocks, 0)
    o_specs = [pl.BlockSpec((tm, DIL_W), row)] + [
        pl.BlockSpec((1, d, tm // d, DIL_W), perm_idx) for d in DIL_DILATIONS[1:]]
    l_specs = [pl.BlockSpec((tm, LANES), row)] + [
        pl.BlockSpec((1, d, tm // d, LANES), perm_idx) for d in DIL_DILATIONS[1:]]
    return pl.pallas_call(
        functools.partial(_out_proj_kernel, final_norm=final_norm),
        out_shape=jax.ShapeDtypeStruct((n, D_MODEL), F32),
        grid=(n // tm,),
        in_specs=[pl.BlockSpec((tm, NA_W), row)] + o_specs + l_specs + [
            pl.BlockSpec((tm, CONV_CH), row),
            pl.BlockSpec((tm, D_MIX), row),
            pl.BlockSpec((tm, D_MODEL), row),
            pl.BlockSpec((D_MIX, D_MODEL), const),
            pl.BlockSpec((1, D_MODEL), const),
        ],
        out_specs=pl.BlockSpec((tm, D_MODEL), row),
        scratch_shapes=[pltpu.VMEM((2, DIL_SLABS, tm, LANES), F32),
                        pltpu.VMEM((2, tm, LANES), F32)],
        compiler_params=_params("parallel"),
        name="out_proj",
    )(ya, *outs, *lses, yc, gate, x2d, w_bf16, final_g)


def _rotary_tables(seq):
    half = HEAD_DIM // 2
    inv_freq = jnp.power(ROPE_THETA, -jnp.arange(half, dtype=F32) * 2.0 / HEAD_DIM)
    ang = jnp.arange(seq).astype(F32)[:, None] * inv_freq[None, :]
    cos = jnp.cos(ang)
    sin = jnp.sin(ang)
    zero = jnp.zeros_like(sin)
    reps = LANES // HEAD_DIM
    cos_t = jnp.tile(jnp.concatenate([cos, cos], axis=1), (1, reps))
    sa_t = jnp.tile(jnp.concatenate([-sin, zero], axis=1), (1, reps))
    sb_t = jnp.tile(jnp.concatenate([zero, sin], axis=1), (1, reps))
    return cos_t, sa_t, sb_t


def kernel(x, norm_g, w_in, na_rpb, conv_w, conv_b, conv_ln_g, conv_ln_b, pw_w, pw_b, w_out, final_g):
    batch, seq, _ = x.shape
    depth = w_in.shape[0]
    n = batch * seq
    cos_t, sa_t, sb_t = _rotary_tables(seq)
    h = x.reshape(n, D_MODEL)
    fg = final_g.reshape(1, D_MODEL)
    for l in range(depth):
        (naq, nak, nav, q1, q4, q16, k1, k4, k16, v1, v4, v16, u, gate) = _in_proj(
            h, norm_g[l].reshape(1, D_MODEL), w_in[l].astype(BF16), cos_t, sa_t, sb_t, batch, seq)
        ya = _na_attention(naq, nak, nav, _na_bias_table(na_rpb[l]), batch, seq)
        outs, lses = [], []
        for d, (q, k, v) in zip(DIL_DILATIONS, ((q1, k1, v1), (q4, k4, v4), (q16, k16, v16))):
            groups = (batch * d, seq // d)
            o, lse = _band_attention(q.reshape(*groups, DIL_W), k.reshape(*groups, DIL_W),
                                     v.reshape(*groups, DIL_W))
            if d == 1:
                outs.append(o.reshape(n, DIL_W))
                lses.append(lse.reshape(n, LANES))
            else:
                outs.append(o.reshape(batch, d, seq // d, DIL_W))
                lses.append(lse.reshape(batch, d, seq // d, LANES))
        yc = _conv_module(u, conv_w[l], conv_b[l].reshape(1, CONV_CH), conv_ln_g[l].reshape(1, CONV_CH),
                          conv_ln_b[l].reshape(1, CONV_CH), pw_w[l].astype(BF16),
                          pw_b[l].reshape(1, CONV_CH), batch, seq)
        h = _out_proj(ya, outs, lses, yc, gate, h, w_out[l].astype(BF16), fg,
                      final_norm=(l == depth - 1), batch=batch, seq=seq)
    return h.reshape(batch, seq, D_MODEL)
```

```python
import functools

import numpy as np
import jax
import jax.numpy as jnp
from jax import lax
from jax.experimental import pallas as pl
from jax.experimental.pallas import tpu as pltpu

D_MODEL = 1024
HEAD_DIM = 64
GRID_W = 64
NA_HEADS = 4
NA_WIN_ROWS = 8
NA_WIN_COLS = 16
DIL_HEADS = 6
DIL_DILATIONS = (1, 4, 16)
DIL_HALF = 64
CONV_CH = 384
CONV_WIDTH = 31
ROPE_THETA = 10000.0
NORM_EPS = 1e-6
NEG_INF = -1e30
NA_W = NA_HEADS * HEAD_DIM
DIL_W = DIL_HEADS * HEAD_DIM
D_MIX = NA_W + DIL_W + CONV_CH
SCALE = HEAD_DIM ** -0.5
LOG2E = 1.4426950408889634
Q_SCALE = SCALE * LOG2E

LANES = 128
SUBLANES = 8
VMEM_LIMIT = 56 * 1024 * 1024
DIL_SLABS = DIL_W // LANES

_C = np.cumsum([0] + [NA_W] * 4 + [DIL_W] * 4 + [CONV_CH] * 3)
(C_AQ, C_AK, C_AV, C_AG, C_BQ, C_BK, C_BV, C_BG, C_CA, C_CB, C_CG, C_END) = [int(c) for c in _C]

BF16 = jnp.bfloat16
F32 = jnp.float32

TOKEN_TILE = 512


def _params(*sem):
    return pltpu.CompilerParams(dimension_semantics=sem, vmem_limit_bytes=VMEM_LIMIT)


def _silu(v):
    return v * (1.0 / (1.0 + jnp.exp(-v)))


def _lane_is_first_head(shape):
    return lax.broadcasted_iota(jnp.int32, shape, len(shape) - 1) < HEAD_DIM


def _in_proj_kernel(x_ref, g_ref, w_ref, cos_ref, sa_ref, sb_ref,
                    naq_ref, nak_ref, nav_ref,
                    q1_ref, q4_ref, q16_ref, k1_ref, k4_ref, k16_ref, v1_ref, v4_ref, v16_ref,
                    u_ref, gate_ref, perm):
    tm = x_ref.shape[0]
    x = x_ref[...]
    ms = jnp.mean(x * x, axis=-1, keepdims=True)
    h = (x * lax.rsqrt(ms + NORM_EPS) * g_ref[...]).astype(BF16)

    def mm(c0, c1):
        return jnp.dot(h, w_ref[:, c0:c1], preferred_element_type=F32)

    cos = cos_ref[...]
    sa = sa_ref[...]
    sb = sb_ref[...]

    def rope(zs):
        up = pltpu.roll(zs, LANES - HEAD_DIM // 2, 1)
        dn = pltpu.roll(zs, HEAD_DIM // 2, 1)
        return zs * cos + up * sa + dn * sb

    def emit_dilated(z, refs, rotary, scale):
        ref1, ref4, ref16 = refs
        for s in range(DIL_SLABS):
            zs = z[:, s * LANES:(s + 1) * LANES]
            if rotary:
                zs = rope(zs)
            if scale != 1.0:
                zs = zs * scale
            perm[s] = zs
            ref1[:, s * LANES:(s + 1) * LANES] = zs.astype(BF16)
        for d, ref in ((4, ref4), (16, ref16)):
            for rho in range(d):
                for s in range(DIL_SLABS):
                    ref[0, rho, :, s * LANES:(s + 1) * LANES] = (
                        perm[s, pl.ds(rho, tm // d, stride=d), :].astype(BF16))

    za = mm(C_AQ, C_BQ)
    naq_ref[...] = (za[:, 0:NA_W] * Q_SCALE).astype(BF16)
    nak_ref[...] = za[:, NA_W:2 * NA_W].astype(BF16)
    nav_ref[...] = za[:, 2 * NA_W:3 * NA_W].astype(BF16)
    gate_ref[:, 0:NA_W] = _silu(za[:, 3 * NA_W:4 * NA_W]).astype(BF16)

    zqk = mm(C_BQ, C_BV)
    emit_dilated(zqk[:, 0:DIL_W], (q1_ref, q4_ref, q16_ref), True, Q_SCALE)
    emit_dilated(zqk[:, DIL_W:2 * DIL_W], (k1_ref, k4_ref, k16_ref), True, 1.0)

    zvg = mm(C_BV, C_CA)
    emit_dilated(zvg[:, 0:DIL_W], (v1_ref, v4_ref, v16_ref), False, 1.0)
    gate_ref[:, NA_W:NA_W + DIL_W] = _silu(zvg[:, DIL_W:2 * DIL_W]).astype(BF16)

    zc = mm(C_CA, C_CG)
    u_ref[...] = (zc[:, 0:CONV_CH] * (1.0 / (1.0 + jnp.exp(-zc[:, CONV_CH:2 * CONV_CH])))).astype(BF16)
    gate_ref[:, NA_W + DIL_W:D_MIX] = _silu(mm(C_CG, C_END)).astype(BF16)


def _in_proj(x2d, g, w_bf16, cos_t, sa_t, sb_t, batch, seq):
    n = x2d.shape[0]
    tm = TOKEN_TILE
    seq_blocks = seq // tm
    row = lambda i: (i, 0)
    tab = lambda i: (i % seq_blocks, 0)
    const = lambda i: (0, 0)
    perm_idx = lambda i: (i // seq_blocks, 0, i % seq_blocks, 0)

    def dil_shapes():
        return [jax.ShapeDtypeStruct((n, DIL_W), BF16)] + [
            jax.ShapeDtypeStruct((batch, d, seq // d, DIL_W), BF16) for d in DIL_DILATIONS[1:]]

    def dil_specs():
        return [pl.BlockSpec((tm, DIL_W), row)] + [
            pl.BlockSpec((1, d, tm // d, DIL_W), perm_idx) for d in DIL_DILATIONS[1:]]

    out_shapes = ([jax.ShapeDtypeStruct((n, NA_W), BF16)] * 3 + dil_shapes() + dil_shapes() + dil_shapes()
                  + [jax.ShapeDtypeStruct((n, CONV_CH), BF16), jax.ShapeDtypeStruct((n, D_MIX), BF16)])
    out_specs = ([pl.BlockSpec((tm, NA_W), row)] * 3 + dil_specs() + dil_specs() + dil_specs()
                 + [pl.BlockSpec((tm, CONV_CH), row), pl.BlockSpec((tm, D_MIX), row)])
    return pl.pallas_call(
        _in_proj_kernel,
        out_shape=out_shapes,
        grid=(n // tm,),
        in_specs=[
            pl.BlockSpec((tm, D_MODEL), row),
            pl.BlockSpec((1, D_MODEL), const),
            pl.BlockSpec((D_MODEL, C_END), const),
            pl.BlockSpec((tm, LANES), tab),
            pl.BlockSpec((tm, LANES), tab),
            pl.BlockSpec((tm, LANES), tab),
        ],
        out_specs=out_specs,
        scratch_shapes=[pltpu.VMEM((DIL_SLABS, tm, LANES), F32)],
        compiler_params=_params("parallel"),
        name="in_proj",
    )(x2d, g, w_bf16, cos_t, sa_t, sb_t)


NA_ROWS_PER_STEP = 8
NA_KEYS = NA_WIN_ROWS * GRID_W


def _na_kernel(q_ref, k_ref, v_ref, bias_ref, o_ref, sbuf, ebuf, *, rows):
    blk = pl.program_id(1)
    first = _lane_is_first_head((GRID_W, LANES))
    ones = jnp.ones((NA_KEYS, LANES), BF16)
    n_pairs = NA_W // LANES
    starts, maxes = [], []
    for rr in range(NA_ROWS_PER_STEP):
        r = blk * NA_ROWS_PER_STEP + rr
        r0 = jnp.clip(r - NA_WIN_ROWS // 2, 0, rows - NA_WIN_ROWS)
        delta = r0 - r + (NA_WIN_ROWS - 1)
        start = pl.multiple_of(r0 * GRID_W, GRID_W)
        starts.append(start)
        q_row = q_ref[0, rr * GRID_W:(rr + 1) * GRID_W, :]
        k_win = k_ref[0, pl.ds(start, NA_KEYS), :]
        for p in range(n_pairs):
            qp = q_row[:, p * LANES:(p + 1) * LANES]
            zero = jnp.zeros_like(qp)
            lhs = jnp.concatenate([jnp.where(first, qp, zero), jnp.where(first, zero, qp)], axis=0)
            kp = k_win[:, p * LANES:(p + 1) * LANES]
            s = lax.dot_general(lhs, kp, (((1,), (1,)), ((), ())), preferred_element_type=F32)
            s = s + bias_ref[delta, p]
            sbuf[rr * n_pairs + p] = s
            maxes.append(jnp.max(s, axis=-1, keepdims=True))
    for c in range(NA_ROWS_PER_STEP * n_pairs):
        ebuf[c] = jnp.exp2(sbuf[c] - maxes[c]).astype(BF16)
    for rr in range(NA_ROWS_PER_STEP):
        v_win = v_ref[0, pl.ds(starts[rr], NA_KEYS), :]
        for p in range(n_pairs):
            v_aug = jnp.concatenate([v_win[:, p * LANES:(p + 1) * LANES], ones], axis=1)
            o = jnp.dot(ebuf[rr * n_pairs + p], v_aug, preferred_element_type=F32)
            o = o[:, 0:LANES] * (1.0 / o[:, LANES:2 * LANES])
            out = jnp.where(first, o[0:GRID_W], o[GRID_W:2 * GRID_W])
            o_ref[0, rr * GRID_W:(rr + 1) * GRID_W, p * LANES:(p + 1) * LANES] = out.astype(BF16)


def _na_bias_table(rpb):
    cols = np.arange(GRID_W)
    col_start = np.clip(cols - NA_WIN_COLS // 2, 0, GRID_W - NA_WIN_COLS)
    kc = np.arange(GRID_W)
    in_win = (kc[None, :] >= col_start[:, None]) & (kc[None, :] < col_start[:, None] + NA_WIN_COLS)
    col_off = kc[None, :] - cols[:, None] + (NA_WIN_COLS - 1)
    n_off = 2 * NA_WIN_COLS - 1
    onehot = (col_off[None] == np.arange(n_off)[:, None, None]) & in_win[None]
    t = jnp.einsum('hrj,jck->hrck', rpb.astype(F32), jnp.asarray(onehot, F32),
                   precision=lax.Precision.HIGHEST)
    t = jnp.where(jnp.asarray(in_win)[None, None], t * LOG2E, NEG_INF)
    t = jnp.stack([t[:, dlt:dlt + NA_WIN_ROWS] for dlt in range(NA_WIN_ROWS)], axis=0)
    t = t.transpose(0, 1, 3, 2, 4)
    return t.reshape(NA_WIN_ROWS, NA_HEADS // 2, 2 * GRID_W, NA_KEYS)


def _na_attention(q, k, v, bias, batch, seq):
    rows = seq // GRID_W
    tq = NA_ROWS_PER_STEP * GRID_W
    q3 = q.reshape(batch, seq, NA_W)
    k3 = k.reshape(batch, seq, NA_W)
    v3 = v.reshape(batch, seq, NA_W)
    out = pl.pallas_call(
        functools.partial(_na_kernel, rows=rows),
        out_shape=jax.ShapeDtypeStruct((batch, seq, NA_W), BF16),
        grid=(batch, seq // tq),
        in_specs=[
            pl.BlockSpec((1, tq, NA_W), lambda b, i: (b, i, 0)),
            pl.BlockSpec((1, seq, NA_W), lambda b, i: (b, 0, 0)),
            pl.BlockSpec((1, seq, NA_W), lambda b, i: (b, 0, 0)),
            pl.BlockSpec(bias.shape, lambda b, i: (0, 0, 0, 0)),
        ],
        out_specs=pl.BlockSpec((1, tq, NA_W), lambda b, i: (b, i, 0)),
        scratch_shapes=[pltpu.VMEM((NA_ROWS_PER_STEP * NA_W // LANES, 2 * GRID_W, NA_KEYS), F32),
                        pltpu.VMEM((NA_ROWS_PER_STEP * NA_W // LANES, 2 * GRID_W, NA_KEYS), BF16)],
        compiler_params=_params("parallel", "arbitrary"),
        name="na_attention",
    )(q3, k3, v3, bias)
    return out.reshape(batch * seq, NA_W)


BAND_CHUNK = 128
BAND_KEYS = BAND_CHUNK + 2 * DIL_HALF
BAND_TQ = 512


def _band_kernel(q_ref, kp_ref, kc_ref, kn_ref, vp_ref, vc_ref, vn_ref, o_ref, lse_ref, *, length):
    blk = pl.program_id(1)
    n_chunks = BAND_TQ // BAND_CHUNK
    first = _lane_is_first_head((BAND_CHUNK, LANES))
    lane = lax.broadcasted_iota(jnp.int32, (BAND_CHUNK, LANES), 1)
    qi = lax.broadcasted_iota(jnp.int32, (BAND_CHUNK, BAND_KEYS), 0)
    kj = lax.broadcasted_iota(jnp.int32, (BAND_CHUNK, BAND_KEYS), 1)
    rel = kj - DIL_HALF - qi
    band = (rel >= -DIL_HALF) & (rel <= DIL_HALF)
    ones = jnp.ones((BAND_KEYS, LANES), BF16)
    for c in range(n_chunks):
        a0 = blk * BAND_TQ + c * BAND_CHUNK
        kpos = a0 - DIL_HALF + kj
        ok = band & (kpos >= 0) & (kpos < length)
        mask_bias = jnp.where(ok, 0.0, NEG_INF).astype(F32)
        mask_bias2 = jnp.concatenate([mask_bias, mask_bias], axis=0)
        lo = c * BAND_CHUNK - DIL_HALF
        hi = lo + BAND_KEYS

        def window(prev_ref, cur_ref, next_ref):
            pieces = []
            if lo < 0:
                pieces.append(prev_ref[0, :, :])
            pieces.append(cur_ref[0, max(lo, 0):min(hi, BAND_TQ), :])
            if hi > BAND_TQ:
                pieces.append(next_ref[0, :, :])
            return pieces[0] if len(pieces) == 1 else jnp.concatenate(pieces, axis=0)

        k_win = window(kp_ref, kc_ref, kn_ref)
        v_win = window(vp_ref, vc_ref, vn_ref)
        q_chunk = q_ref[0, c * BAND_CHUNK:(c + 1) * BAND_CHUNK, :]
        rows = slice(c * BAND_CHUNK, (c + 1) * BAND_CHUNK)
        lse_tile = jnp.zeros((BAND_CHUNK, LANES), F32)
        for p in range(DIL_SLABS):
            qp = q_chunk[:, p * LANES:(p + 1) * LANES]
            zero = jnp.zeros_like(qp)
            lhs = jnp.concatenate([jnp.where(first, qp, zero), jnp.where(first, zero, qp)], axis=0)
            kp = k_win[:, p * LANES:(p + 1) * LANES]
            vp = v_win[:, p * LANES:(p + 1) * LANES]
            s = lax.dot_general(lhs, kp, (((1,), (1,)), ((), ())), preferred_element_type=F32)
            s = s + mask_bias2
            m = jnp.max(s, axis=-1, keepdims=True)
            e = jnp.exp2(s - m).astype(BF16)
            o = jnp.dot(e, jnp.concatenate([vp, ones], axis=1), preferred_element_type=F32)
            l = o[:, LANES:2 * LANES]
            o = o[:, 0:LANES] * (1.0 / l)
            lse = m + jnp.log(l) * LOG2E
            o_ref[0, rows, p * LANES:(p + 1) * LANES] = jnp.where(
                first, o[0:BAND_CHUNK], o[BAND_CHUNK:]).astype(BF16)
            lse_tile = jnp.where(lane == 2 * p, lse[0:BAND_CHUNK], lse_tile)
            lse_tile = jnp.where(lane == 2 * p + 1, lse[BAND_CHUNK:], lse_tile)
        lse_ref[0, rows, :] = lse_tile


def _band_attention(q, k, v):
    groups, length, _ = q.shape
    halo_per_blk = BAND_TQ // DIL_HALF
    n_halo = length // DIL_HALF
    cur = lambda g, i: (g, i, 0)
    prev = lambda g, i: (g, jnp.maximum(i * halo_per_blk - 1, 0), 0)
    nxt = lambda g, i: (g, jnp.minimum((i + 1) * halo_per_blk, n_halo - 1), 0)
    blk = pl.BlockSpec((1, BAND_TQ, DIL_W), cur)
    halo_p = pl.BlockSpec((1, DIL_HALF, DIL_W), prev)
    halo_n = pl.BlockSpec((1, DIL_HALF, DIL_W), nxt)
    return pl.pallas_call(
        functools.partial(_band_kernel, length=length),
        out_shape=[jax.ShapeDtypeStruct((groups, length, DIL_W), BF16),
                   jax.ShapeDtypeStruct((groups, length, LANES), F32)],
        grid=(groups, length // BAND_TQ),
        in_specs=[blk, halo_p, blk, halo_n, halo_p, blk, halo_n],
        out_specs=[pl.BlockSpec((1, BAND_TQ, DIL_W), cur), pl.BlockSpec((1, BAND_TQ, LANES), cur)],
        compiler_params=_params("parallel", "arbitrary"),
        name="band_attention",
    )(q, k, k, k, v, v, v)


CONV_T = 512
CONV_HALO = 16
CONV_PAD = CONV_WIDTH // 2
CONV_ROWS = 128


def _conv_kernel(up_ref, uc_ref, un_ref, w_ref, b_ref, lg_ref, lb_ref, pw_ref, pb_ref, o_ref, ubuf, ybuf):
    i = pl.program_id(1)
    n = pl.num_programs(1)
    prev_scale = jnp.where(i > 0, 1.0, 0.0).astype(F32)
    next_scale = jnp.where(i < n - 1, 1.0, 0.0).astype(F32)
    ubuf[0:CONV_HALO, :] = up_ref[0].astype(F32) * prev_scale
    ubuf[CONV_HALO:CONV_HALO + CONV_T, :] = uc_ref[0].astype(F32)
    ubuf[CONV_HALO + CONV_T:, :] = un_ref[0].astype(F32) * next_scale

    first_off = CONV_HALO - CONV_PAD

    def body(c, carry):
        r0 = pl.multiple_of(c * CONV_ROWS, CONV_ROWS)
        for sl in range(CONV_CH // LANES):
            lanes = slice(sl * LANES, (sl + 1) * LANES)
            y = None
            for s in range(SUBLANES):
                part = None
                for k in range(CONV_WIDTH):
                    off = first_off + k
                    if off % SUBLANES != s:
                        continue
                    win = ubuf[pl.ds(r0 + (off - s), CONV_ROWS + SUBLANES), lanes]
                    term = win * w_ref[k:k + 1, lanes]
                    part = term if part is None else part + term
                shifted = part[s:s + CONV_ROWS, :]
                y = shifted if y is None else y + shifted
            ybuf[pl.ds(r0, CONV_ROWS), lanes] = y
        return carry

    lax.fori_loop(0, CONV_T // CONV_ROWS, body, 0)

    y = ybuf[...] + b_ref[...]
    mu = jnp.mean(y, axis=-1, keepdims=True)
    yc = y - mu
    var = jnp.mean(yc * yc, axis=-1, keepdims=True)
    yn = yc * lax.rsqrt(var + NORM_EPS) * lg_ref[...] + lb_ref[...]
    act = _silu(yn).astype(BF16)
    o_ref[0] = (jnp.dot(act, pw_ref[...], preferred_element_type=F32) + pb_ref[...]).astype(BF16)


def _conv_module(u, w_dw, b_dw, ln_g, ln_b, pw_bf16, pw_b, batch, seq):
    u3 = u.reshape(batch, seq, CONV_CH)
    per = CONV_T // CONV_HALO
    n_halo = seq // CONV_HALO
    cur = lambda b, i: (b, i, 0)
    prev = lambda b, i: (b, jnp.maximum(i * per - 1, 0), 0)
    nxt = lambda b, i: (b, jnp.minimum((i + 1) * per, n_halo - 1), 0)
    const = lambda b, i: (0, 0)
    vec = pl.BlockSpec((1, CONV_CH), const)
    out = pl.pallas_call(
        _conv_kernel,
        out_shape=jax.ShapeDtypeStruct((batch, seq, CONV_CH), BF16),
        grid=(batch, seq // CONV_T),
        in_specs=[
            pl.BlockSpec((1, CONV_HALO, CONV_CH), prev),
            pl.BlockSpec((1, CONV_T, CONV_CH), cur),
            pl.BlockSpec((1, CONV_HALO, CONV_CH), nxt),
            pl.BlockSpec((CONV_WIDTH, CONV_CH), const),
            vec, vec, vec,
            pl.BlockSpec((CONV_CH, CONV_CH), const),
            vec,
        ],
        out_specs=pl.BlockSpec((1, CONV_T, CONV_CH), cur),
        scratch_shapes=[pltpu.VMEM((CONV_T + 2 * CONV_HALO, CONV_CH), F32),
                        pltpu.VMEM((CONV_T, CONV_CH), F32)],
        compiler_params=_params("parallel", "arbitrary"),
        name="conv_module",
    )(u3, u3, u3, w_dw, b_dw, ln_g, ln_b, pw_bf16, pw_b)
    return out.reshape(batch * seq, CONV_CH)


def _out_proj_kernel(ya_ref, o1_ref, o4_ref, o16_ref, l1_ref, l4_ref, l16_ref, yc_ref, gate_ref,
                     x_ref, w_ref, fg_ref, out_ref, onat, lnat, *, final_norm):
    tm = x_ref.shape[0]
    first = _lane_is_first_head((tm, LANES))

    def natural_stats(l_ref, d, slot):
        if d == 1:
            return l_ref[...]
        for rho in range(d):
            lnat[slot, pl.ds(rho, tm // d, stride=d), :] = l_ref[0, rho]
        return lnat[slot]

    l1 = natural_stats(l1_ref, 1, 0)
    l4 = natural_stats(l4_ref, 4, 0)
    l16 = natural_stats(l16_ref, 16, 1)
    m = jnp.maximum(jnp.maximum(l1, l4), l16)
    e1 = jnp.exp2(l1 - m)
    e4 = jnp.exp2(l4 - m)
    e16 = jnp.exp2(l16 - m)
    inv = 1.0 / (e1 + e4 + e16)
    weights = (e1 * inv, e4 * inv, e16 * inv)

    for slot, (d, o_ref) in enumerate(((4, o4_ref), (16, o16_ref))):
        for rho in range(d):
            for s in range(DIL_SLABS):
                onat[slot, s, pl.ds(rho, tm // d, stride=d), :] = (
                    o_ref[0, rho, :, s * LANES:(s + 1) * LANES].astype(F32))

    gate = gate_ref[...].astype(F32)
    pieces = [ya_ref[...].astype(F32) * gate[:, 0:NA_W]]
    for s in range(DIL_SLABS):
        outs = (o1_ref[:, s * LANES:(s + 1) * LANES].astype(F32), onat[0, s], onat[1, s])
        yb = None
        for w, o in zip(weights, outs):
            wb = jnp.where(first, w[:, 2 * s:2 * s + 1], w[:, 2 * s + 1:2 * s + 2])
            yb = wb * o if yb is None else yb + wb * o
        pieces.append(yb * gate[:, NA_W + s * LANES:NA_W + (s + 1) * LANES])
    pieces.append(yc_ref[...].astype(F32) * gate[:, NA_W + DIL_W:D_MIX])
    mix = jnp.concatenate(pieces, axis=1).astype(BF16)
    y = x_ref[...] + jnp.dot(mix, w_ref[...], preferred_element_type=F32)
    if final_norm:
        ms = jnp.mean(y * y, axis=-1, keepdims=True)
        y = y * lax.rsqrt(ms + NORM_EPS) * fg_ref[...]
    out_ref[...] = y


def _out_proj(ya, outs, lses, yc, gate, x2d, w_bf16, final_g, final_norm, batch, seq):
    n = x2d.shape[0]
    tm = TOKEN_TILE
    seq_blocks = seq // tm
    row = lambda i: (i, 0)
    const = lambda i: (0, 0)
    perm_idx = lambda i: (i // seq_blocks, 0, i % seq_blocks, 0)
    o_specs = [pl.BlockSpec((tm, DIL_W), row)] + [
        pl.BlockSpec((1, d, tm // d, DIL_W), perm_idx) for d in DIL_DILATIONS[1:]]
    l_specs = [pl.BlockSpec((tm, LANES), row)] + [
        pl.BlockSpec((1, d, tm // d, LANES), perm_idx) for d in DIL_DILATIONS[1:]]
    return pl.pallas_call(
        functools.partial(_out_proj_kernel, final_norm=final_norm),
        out_shape=jax.ShapeDtypeStruct((n, D_MODEL), F32),
        grid=(n // tm,),
        in_specs=[pl.BlockSpec((tm, NA_W), row)] + o_specs + l_specs + [
            pl.BlockSpec((tm, CONV_CH), row),
            pl.BlockSpec((tm, D_MIX), row),
            pl.BlockSpec((tm, D_MODEL), row),
            pl.BlockSpec((D_MIX, D_MODEL), const),
            pl.BlockSpec((1, D_MODEL), const),
        ],
        out_specs=pl.BlockSpec((tm, D_MODEL), row),
        scratch_shapes=[pltpu.VMEM((2, DIL_SLABS, tm, LANES), F32),
                        pltpu.VMEM((2, tm, LANES), F32)],
        compiler_params=_params("parallel"),
        name="out_proj",
    )(ya, *outs, *lses, yc, gate, x2d, w_bf16, final_g)


def _rotary_tables(seq):
    half = HEAD_DIM // 2
    inv_freq = jnp.power(ROPE_THETA, -jnp.arange(half, dtype=F32) * 2.0 / HEAD_DIM)
    ang = jnp.arange(seq).astype(F32)[:, None] * inv_freq[None, :]
    cos = jnp.cos(ang)
    sin = jnp.sin(ang)
    zero = jnp.zeros_like(sin)
    reps = LANES // HEAD_DIM
    cos_t = jnp.tile(jnp.concatenate([cos, cos], axis=1), (1, reps))
    sa_t = jnp.tile(jnp.concatenate([-sin, zero], axis=1), (1, reps))
    sb_t = jnp.tile(jnp.concatenate([zero, sin], axis=1), (1, reps))
    return cos_t, sa_t, sb_t


def kernel(x, norm_g, w_in, na_rpb, conv_w, conv_b, conv_ln_g, conv_ln_b, pw_w, pw_b, w_out, final_g):
    batch, seq, _ = x.shape
    depth = w_in.shape[0]
    n = batch * seq
    cos_t, sa_t, sb_t = _rotary_tables(seq)
    h = x.reshape(n, D_MODEL)
    fg = final_g.reshape(1, D_MODEL)
    for l in range(depth):
        (naq, nak, nav, q1, q4, q16, k1, k4, k16, v1, v4, v16, u, gate) = _in_proj(
            h, norm_g[l].reshape(1, D_MODEL), w_in[l].astype(BF16), cos_t, sa_t, sb_t, batch, seq)
        ya = _na_attention(naq, nak, nav, _na_bias_table(na_rpb[l]), batch, seq)
        outs, lses = [], []
        for d, (q, k, v) in zip(DIL_DILATIONS, ((q1, k1, v1), (q4, k4, v4), (q16, k16, v16))):
            groups = (batch * d, seq // d)
            o, lse = _band_attention(q.reshape(*groups, DIL_W), k.reshape(*groups, DIL_W),
                                     v.reshape(*groups, DIL_W))
            if d == 1:
                outs.append(o.reshape(n, DIL_W))
                lses.append(lse.reshape(n, LANES))
            else:
                outs.append(o.reshape(batch, d, seq // d, DIL_W))
                lses.append(lse.reshape(batch, d, seq // d, LANES))
        yc = _conv_module(u, conv_w[l], conv_b[l].reshape(1, CONV_CH), conv_ln_g[l].reshape(1, CONV_CH),
                          conv_ln_b[l].reshape(1, CONV_CH), pw_w[l].astype(BF16),
                          pw_b[l].reshape(1, CONV_CH), batch, seq)
        h = _out_proj(ya, outs, lses, yc, gate, h, w_out[l].astype(BF16), fg,
                      final_norm=(l == depth - 1), batch=batch, seq=seq)
    return h.reshape(batch, seq, D_MODEL)
```

```python
import functools

import numpy as np
import jax
import jax.numpy as jnp
from jax import lax
from jax.experimental import pallas as pl
from jax.experimental.pallas import tpu as pltpu

D_MODEL = 1024
HEAD_DIM = 64
GRID_W = 64
NA_HEADS = 4
NA_WIN_ROWS = 8
NA_WIN_COLS = 16
DIL_HEADS = 6
DIL_DILATIONS = (1, 4, 16)
DIL_HALF = 64
CONV_CH = 384
CONV_WIDTH = 31
ROPE_THETA = 10000.0
NORM_EPS = 1e-6
NEG_INF = -1e30
NA_W = NA_HEADS * HEAD_DIM
DIL_W = DIL_HEADS * HEAD_DIM
D_MIX = NA_W + DIL_W + CONV_CH
SCALE = HEAD_DIM ** -0.5
LOG2E = 1.4426950408889634
Q_SCALE = SCALE * LOG2E

LANES = 128
SUBLANES = 8
VMEM_LIMIT = 56 * 1024 * 1024
DIL_SLABS = DIL_W // LANES

_C = np.cumsum([0] + [NA_W] * 4 + [DIL_W] * 4 + [CONV_CH] * 3)
(C_AQ, C_AK, C_AV, C_AG, C_BQ, C_BK, C_BV, C_BG, C_CA, C_CB, C_CG, C_END) = [int(c) for c in _C]

BF16 = jnp.bfloat16
F32 = jnp.float32

TOKEN_TILE = 512
OUT_TILE = 1024
CONV_HALO = 16
CONV_PAD = CONV_WIDTH // 2
CONV_ROWS = 128


def _params(*sem):
    return pltpu.CompilerParams(dimension_semantics=sem, vmem_limit_bytes=VMEM_LIMIT)


def _silu(v):
    return v * (1.0 / (1.0 + jnp.exp(-v)))


def _lane_is_first_head(shape):
    return lax.broadcasted_iota(jnp.int32, shape, len(shape) - 1) < HEAD_DIM


def _in_proj_kernel(xp_ref, x_ref, xn_ref, g_ref, w_ref, cos_ref, sa_ref, sb_ref,
                    cw_ref, cb_ref, lg_ref, lb_ref, pw_ref, pb_ref,
                    naq_ref, nak_ref, nav_ref,
                    q1_ref, q4_ref, q16_ref, k1_ref, k4_ref, k16_ref, v1_ref, v4_ref, v16_ref,
                    ycg_ref, gate_ref, perm, ubuf, ybuf, *, seq_blocks):
    tm = x_ref.shape[0]

    def normed(x):
        ms = jnp.mean(x * x, axis=-1, keepdims=True)
        return (x * lax.rsqrt(ms + NORM_EPS) * g_ref[...]).astype(BF16)

    h = normed(x_ref[...])

    def mm(c0, c1):
        return jnp.dot(h, w_ref[:, c0:c1], preferred_element_type=F32)

    cos = cos_ref[...]
    sa = sa_ref[...]
    sb = sb_ref[...]

    def rope(zs):
        up = pltpu.roll(zs, LANES - HEAD_DIM // 2, 1)
        dn = pltpu.roll(zs, HEAD_DIM // 2, 1)
        return zs * cos + up * sa + dn * sb

    def emit_dilated(z, refs, rotary, scale):
        ref1, ref4, ref16 = refs
        for s in range(DIL_SLABS):
            zs = z[:, s * LANES:(s + 1) * LANES]
            if rotary:
                zs = rope(zs)
            if scale != 1.0:
                zs = zs * scale
            perm[s] = zs
            ref1[:, s * LANES:(s + 1) * LANES] = zs.astype(BF16)
        for d, ref in ((4, ref4), (16, ref16)):
            for rho in range(d):
                for s in range(DIL_SLABS):
                    ref[0, rho, :, s * LANES:(s + 1) * LANES] = (
                        perm[s, pl.ds(rho, tm // d, stride=d), :].astype(BF16))

    i = pl.program_id(0)
    pos = i % seq_blocks
    h_ext = jnp.concatenate([normed(xp_ref[...]), h, normed(xn_ref[...])], axis=0)
    zc = jnp.dot(h_ext, w_ref[:, C_CA:C_CG], preferred_element_type=F32)
    u = zc[:, 0:CONV_CH] * (1.0 / (1.0 + jnp.exp(-zc[:, CONV_CH:2 * CONV_CH])))
    ubuf[0:CONV_HALO, :] = u[0:CONV_HALO] * jnp.where(pos > 0, 1.0, 0.0).astype(F32)
    ubuf[CONV_HALO:CONV_HALO + tm, :] = u[CONV_HALO:CONV_HALO + tm]
    ubuf[CONV_HALO + tm:, :] = u[CONV_HALO + tm:] * jnp.where(pos < seq_blocks - 1, 1.0, 0.0).astype(F32)

    first_off = CONV_HALO - CONV_PAD

    for c in range(tm // CONV_ROWS):
        r0 = c * CONV_ROWS
        for sl in range(CONV_CH // LANES):
            lanes = slice(sl * LANES, (sl + 1) * LANES)
            y = None
            for s in range(SUBLANES):
                part = None
                for k in range(CONV_WIDTH):
                    off = first_off + k
                    if off % SUBLANES != s:
                        continue
                    win = ubuf[pl.ds(r0 + (off - s), CONV_ROWS + SUBLANES), lanes]
                    term = win * cw_ref[k:k + 1, lanes]
                    part = term if part is None else part + term
                shifted = part[s:s + CONV_ROWS, :]
                y = shifted if y is None else y + shifted
            ybuf[pl.ds(r0, CONV_ROWS), lanes] = y

    za = mm(C_AQ, C_BQ)
    naq_ref[...] = (za[:, 0:NA_W] * Q_SCALE).astype(BF16)
    nak_ref[...] = za[:, NA_W:2 * NA_W].astype(BF16)
    nav_ref[...] = za[:, 2 * NA_W:3 * NA_W].astype(BF16)
    gate_ref[:, 0:NA_W] = _silu(za[:, 3 * NA_W:4 * NA_W]).astype(BF16)

    zqk = mm(C_BQ, C_BV)
    emit_dilated(zqk[:, 0:DIL_W], (q1_ref, q4_ref, q16_ref), True, Q_SCALE)
    emit_dilated(zqk[:, DIL_W:2 * DIL_W], (k1_ref, k4_ref, k16_ref), True, 1.0)

    zvg = mm(C_BV, C_CA)
    emit_dilated(zvg[:, 0:DIL_W], (v1_ref, v4_ref, v16_ref), False, 1.0)
    gate_ref[:, NA_W:NA_W + DIL_W] = _silu(zvg[:, DIL_W:2 * DIL_W]).astype(BF16)

    y = ybuf[...] + cb_ref[...]
    mu = jnp.mean(y, axis=-1, keepdims=True)
    yc = y - mu
    var = jnp.mean(yc * yc, axis=-1, keepdims=True)
    yn = yc * lax.rsqrt(var + NORM_EPS) * lg_ref[...] + lb_ref[...]
    act = _silu(yn).astype(BF16)
    yc = jnp.dot(act, pw_ref[...], preferred_element_type=F32) + pb_ref[...]
    ycg_ref[...] = (yc * _silu(mm(C_CG, C_END))).astype(BF16)


def _in_proj(x2d, g, w_bf16, cos_t, sa_t, sb_t, conv_params, batch, seq):
    n = x2d.shape[0]
    tm = TOKEN_TILE
    seq_blocks = seq // tm
    halo_per_tile = tm // CONV_HALO
    row = lambda i: (i, 0)
    prev = lambda i: (jnp.maximum(i * halo_per_tile - 1, 0), 0)
    nxt = lambda i: (jnp.minimum((i + 1) * halo_per_tile, n // CONV_HALO - 1), 0)
    tab = lambda i: (i % seq_blocks, 0)
    const = lambda i: (0, 0)
    perm_idx = lambda i: (i // seq_blocks, 0, i % seq_blocks, 0)
    vec = pl.BlockSpec((1, CONV_CH), const)

    def dil_shapes():
        return [jax.ShapeDtypeStruct((n, DIL_W), BF16)] + [
            jax.ShapeDtypeStruct((batch, d, seq // d, DIL_W), BF16) for d in DIL_DILATIONS[1:]]

    def dil_specs():
        return [pl.BlockSpec((tm, DIL_W), row)] + [
            pl.BlockSpec((1, d, tm // d, DIL_W), perm_idx) for d in DIL_DILATIONS[1:]]

    out_shapes = ([jax.ShapeDtypeStruct((n, NA_W), BF16)] * 3 + dil_shapes() + dil_shapes() + dil_shapes()
                  + [jax.ShapeDtypeStruct((n, CONV_CH), BF16), jax.ShapeDtypeStruct((n, NA_W + DIL_W), BF16)])
    out_specs = ([pl.BlockSpec((tm, NA_W), row)] * 3 + dil_specs() + dil_specs() + dil_specs()
                 + [pl.BlockSpec((tm, CONV_CH), row), pl.BlockSpec((tm, NA_W + DIL_W), row)])
    return pl.pallas_call(
        functools.partial(_in_proj_kernel, seq_blocks=seq_blocks),
        out_shape=out_shapes,
        grid=(n // tm,),
        in_specs=[
            pl.BlockSpec((CONV_HALO, D_MODEL), prev),
            pl.BlockSpec((tm, D_MODEL), row),
            pl.BlockSpec((CONV_HALO, D_MODEL), nxt),
            pl.BlockSpec((1, D_MODEL), const),
            pl.BlockSpec((D_MODEL, C_END), const),
            pl.BlockSpec((tm, LANES), tab),
            pl.BlockSpec((tm, LANES), tab),
            pl.BlockSpec((tm, LANES), tab),
            pl.BlockSpec((CONV_WIDTH, CONV_CH), const),
            vec, vec, vec,
            pl.BlockSpec((CONV_CH, CONV_CH), const),
            vec,
        ],
        out_specs=out_specs,
        scratch_shapes=[pltpu.VMEM((DIL_SLABS, tm, LANES), F32),
                        pltpu.VMEM((tm + 2 * CONV_HALO, CONV_CH), F32),
                        pltpu.VMEM((tm, CONV_CH), F32)],
        compiler_params=_params("parallel"),
        name="in_proj",
    )(x2d, x2d, x2d, g, w_bf16, cos_t, sa_t, sb_t, *conv_params)


NA_ROWS_PER_STEP = 8
NA_KEYS = NA_WIN_ROWS * GRID_W


def _na_kernel(q_ref, k_ref, v_ref, bias_ref, o_ref, sbuf, ebuf, *, rows):
    blk = pl.program_id(1)
    first = _lane_is_first_head((GRID_W, LANES))
    ones = jnp.ones((NA_KEYS, LANES), BF16)
    n_pairs = NA_W // LANES
    starts, maxes = [], []
    for rr in range(NA_ROWS_PER_STEP):
        r = blk * NA_ROWS_PER_STEP + rr
        r0 = jnp.clip(r - NA_WIN_ROWS // 2, 0, rows - NA_WIN_ROWS)
        delta = r0 - r + (NA_WIN_ROWS - 1)
        start = pl.multiple_of(r0 * GRID_W, GRID_W)
        starts.append(start)
        q_row = q_ref[0, rr * GRID_W:(rr + 1) * GRID_W, :]
        k_win = k_ref[0, pl.ds(start, NA_KEYS), :]
        for p in range(n_pairs):
            qp = q_row[:, p * LANES:(p + 1) * LANES]
            zero = jnp.zeros_like(qp)
            lhs = jnp.concatenate([jnp.where(first, qp, zero), jnp.where(first, zero, qp)], axis=0)
            kp = k_win[:, p * LANES:(p + 1) * LANES]
            s = lax.dot_general(lhs, kp, (((1,), (1,)), ((), ())), preferred_element_type=F32)
            s = s + bias_ref[delta, p]
            sbuf[rr * n_pairs + p] = s
            maxes.append(jnp.max(s, axis=-1, keepdims=True))
    for c in range(NA_ROWS_PER_STEP * n_pairs):
        ebuf[c] = jnp.exp2(sbuf[c] - maxes[c]).astype(BF16)
    for rr in range(NA_ROWS_PER_STEP):
        v_win = v_ref[0, pl.ds(starts[rr], NA_KEYS), :]
        for p in range(n_pairs):
            v_aug = jnp.concatenate([v_win[:, p * LANES:(p + 1) * LANES], ones], axis=1)
            o = jnp.dot(ebuf[rr * n_pairs + p], v_aug, preferred_element_type=F32)
            o = o[:, 0:LANES] * (1.0 / o[:, LANES:2 * LANES])
            out = jnp.where(first, o[0:GRID_W], o[GRID_W:2 * GRID_W])
            o_ref[0, rr * GRID_W:(rr + 1) * GRID_W, p * LANES:(p + 1) * LANES] = out.astype(BF16)


def _na_bias_table(rpb):
    cols = np.arange(GRID_W)
    col_start = np.clip(cols - NA_WIN_COLS // 2, 0, GRID_W - NA_WIN_COLS)
    kc = np.arange(GRID_W)
    in_win = (kc[None, :] >= col_start[:, None]) & (kc[None, :] < col_start[:, None] + NA_WIN_COLS)
    col_off = kc[None, :] - cols[:, None] + (NA_WIN_COLS - 1)
    n_off = 2 * NA_WIN_COLS - 1
    onehot = (col_off[None] == np.arange(n_off)[:, None, None]) & in_win[None]
    t = jnp.einsum('hrj,jck->hrck', rpb.astype(F32), jnp.asarray(onehot, F32),
                   precision=lax.Precision.HIGHEST)
    t = jnp.where(jnp.asarray(in_win)[None, None], t * LOG2E, NEG_INF)
    t = jnp.stack([t[:, dlt:dlt + NA_WIN_ROWS] for dlt in range(NA_WIN_ROWS)], axis=0)
    t = t.transpose(0, 1, 3, 2, 4)
    return t.reshape(NA_WIN_ROWS, NA_HEADS // 2, 2 * GRID_W, NA_KEYS)


def _na_attention(q, k, v, bias, batch, seq):
    rows = seq // GRID_W
    tq = NA_ROWS_PER_STEP * GRID_W
    q3 = q.reshape(batch, seq, NA_W)
    k3 = k.reshape(batch, seq, NA_W)
    v3 = v.reshape(batch, seq, NA_W)
    out = pl.pallas_call(
        functools.partial(_na_kernel, rows=rows),
        out_shape=jax.ShapeDtypeStruct((batch, seq, NA_W), BF16),
        grid=(batch, seq // tq),
        in_specs=[
            pl.BlockSpec((1, tq, NA_W), lambda b, i: (b, i, 0)),
            pl.BlockSpec((1, seq, NA_W), lambda b, i: (b, 0, 0)),
            pl.BlockSpec((1, seq, NA_W), lambda b, i: (b, 0, 0)),
            pl.BlockSpec(bias.shape, lambda b, i: (0, 0, 0, 0)),
        ],
        out_specs=pl.BlockSpec((1, tq, NA_W), lambda b, i: (b, i, 0)),
        scratch_shapes=[pltpu.VMEM((NA_ROWS_PER_STEP * NA_W // LANES, 2 * GRID_W, NA_KEYS), F32),
                        pltpu.VMEM((NA_ROWS_PER_STEP * NA_W // LANES, 2 * GRID_W, NA_KEYS), BF16)],
        compiler_params=_params("parallel", "arbitrary"),
        name="na_attention",
    )(q3, k3, v3, bias)
    return out.reshape(batch * seq, NA_W)


BAND_CHUNK = 128
BAND_KEYS = BAND_CHUNK + 2 * DIL_HALF
BAND_TQ = 512


def _band_kernel(q_ref, kp_ref, kc_ref, kn_ref, vp_ref, vc_ref, vn_ref, o_ref, lse_ref, *, length):
    blk = pl.program_id(1)
    n_chunks = BAND_TQ // BAND_CHUNK
    first = _lane_is_first_head((BAND_CHUNK, LANES))
    lane = lax.broadcasted_iota(jnp.int32, (BAND_CHUNK, LANES), 1)
    qi = lax.broadcasted_iota(jnp.int32, (BAND_CHUNK, BAND_KEYS), 0)
    kj = lax.broadcasted_iota(jnp.int32, (BAND_CHUNK, BAND_KEYS), 1)
    rel = kj - DIL_HALF - qi
    band = (rel >= -DIL_HALF) & (rel <= DIL_HALF)
    ones = jnp.ones((BAND_KEYS, LANES), BF16)
    for c in range(n_chunks):
        a0 = blk * BAND_TQ + c * BAND_CHUNK
        kpos = a0 - DIL_HALF + kj
        ok = band & (kpos >= 0) & (kpos < length)
        mask_bias = jnp.where(ok, 0.0, NEG_INF).astype(F32)
        mask_bias2 = jnp.concatenate([mask_bias, mask_bias], axis=0)
        lo = c * BAND_CHUNK - DIL_HALF
        hi = lo + BAND_KEYS

        def window(prev_ref, cur_ref, next_ref):
            pieces = []
            if lo < 0:
                pieces.append(prev_ref[0, :, :])
            pieces.append(cur_ref[0, max(lo, 0):min(hi, BAND_TQ), :])
            if hi > BAND_TQ:
                pieces.append(next_ref[0, :, :])
            return pieces[0] if len(pieces) == 1 else jnp.concatenate(pieces, axis=0)

        k_win = window(kp_ref, kc_ref, kn_ref)
        v_win = window(vp_ref, vc_ref, vn_ref)
        q_chunk = q_ref[0, c * BAND_CHUNK:(c + 1) * BAND_CHUNK, :]
        rows = slice(c * BAND_CHUNK, (c + 1) * BAND_CHUNK)
        lse_tile = jnp.zeros((BAND_CHUNK, LANES), F32)
        for p in range(DIL_SLABS):
            qp = q_chunk[:, p * LANES:(p + 1) * LANES]
            zero = jnp.zeros_like(qp)
            lhs = jnp.concatenate([jnp.where(first, qp, zero), jnp.where(first, zero, qp)], axis=0)
            kp = k_win[:, p * LANES:(p + 1) * LANES]
            vp = v_win[:, p * LANES:(p + 1) * LANES]
            s = lax.dot_general(lhs, kp, (((1,), (1,)), ((), ())), preferred_element_type=F32)
            s = s + mask_bias2
            m = jnp.max(s, axis=-1, keepdims=True)
            e = jnp.exp2(s - m).astype(BF16)
            o = jnp.dot(e, jnp.concatenate([vp, ones], axis=1), preferred_element_type=F32)
            l = o[:, LANES:2 * LANES]
            o = o[:, 0:LANES] * (1.0 / l)
            lse = m + jnp.log(l) * LOG2E
            o_ref[0, rows, p * LANES:(p + 1) * LANES] = jnp.where(
                first, o[0:BAND_CHUNK], o[BAND_CHUNK:]).astype(BF16)
            lse_tile = jnp.where(lane == 2 * p, lse[0:BAND_CHUNK], lse_tile)
            lse_tile = jnp.where(lane == 2 * p + 1, lse[BAND_CHUNK:], lse_tile)
        lse_ref[0, rows, :] = lse_tile


def _band_attention(q, k, v):
    groups, length, _ = q.shape
    halo_per_blk = BAND_TQ // DIL_HALF
    n_halo = length // DIL_HALF
    cur = lambda g, i: (g, i, 0)
    prev = lambda g, i: (g, jnp.maximum(i * halo_per_blk - 1, 0), 0)
    nxt = lambda g, i: (g, jnp.minimum((i + 1) * halo_per_blk, n_halo - 1), 0)
    blk = pl.BlockSpec((1, BAND_TQ, DIL_W), cur)
    halo_p = pl.BlockSpec((1, DIL_HALF, DIL_W), prev)
    halo_n = pl.BlockSpec((1, DIL_HALF, DIL_W), nxt)
    return pl.pallas_call(
        functools.partial(_band_kernel, length=length),
        out_shape=[jax.ShapeDtypeStruct((groups, length, DIL_W), BF16),
                   jax.ShapeDtypeStruct((groups, length, LANES), F32)],
        grid=(groups, length // BAND_TQ),
        in_specs=[blk, halo_p, blk, halo_n, halo_p, blk, halo_n],
        out_specs=[pl.BlockSpec((1, BAND_TQ, DIL_W), cur), pl.BlockSpec((1, BAND_TQ, LANES), cur)],
        compiler_params=_params("parallel", "arbitrary"),
        name="band_attention",
    )(q, k, k, k, v, v, v)


def _out_proj_kernel(ya_ref, o1_ref, o4_ref, o16_ref, l1_ref, l4_ref, l16_ref, ycg_ref, gate_ref,
                     x_ref, w_ref, fg_ref, out_ref, onat, lnat, *, final_norm):
    tm = x_ref.shape[0]
    first = _lane_is_first_head((tm, LANES))

    def natural_stats(l_ref, d, slot):
        if d == 1:
            return l_ref[...]
        for rho in range(d):
            lnat[slot, pl.ds(rho, tm // d, stride=d), :] = l_ref[0, rho]
        return lnat[slot]

    l1 = natural_stats(l1_ref, 1, 0)
    l4 = natural_stats(l4_ref, 4, 0)
    l16 = natural_stats(l16_ref, 16, 1)
    m = jnp.maximum(jnp.maximum(l1, l4), l16)
    e1 = jnp.exp2(l1 - m)
    e4 = jnp.exp2(l4 - m)
    e16 = jnp.exp2(l16 - m)
    inv = 1.0 / (e1 + e4 + e16)
    weights = (e1 * inv, e4 * inv, e16 * inv)

    for slot, (d, o_ref) in enumerate(((4, o4_ref), (16, o16_ref))):
        for rho in range(d):
            for s in range(DIL_SLABS):
                onat[slot, s, pl.ds(rho, tm // d, stride=d), :] = (
                    o_ref[0, rho, :, s * LANES:(s + 1) * LANES].astype(F32))

    gate = gate_ref[...].astype(F32)
    pieces = [ya_ref[...].astype(F32) * gate[:, 0:NA_W]]
    for s in range(DIL_SLABS):
        outs = (o1_ref[:, s * LANES:(s + 1) * LANES].astype(F32), onat[0, s], onat[1, s])
        yb = None
        for w, o in zip(weights, outs):
            wb = jnp.where(first, w[:, 2 * s:2 * s + 1], w[:, 2 * s + 1:2 * s + 2])
            yb = wb * o if yb is None else yb + wb * o
        pieces.append(yb * gate[:, NA_W + s * LANES:NA_W + (s + 1) * LANES])
    mix = jnp.concatenate([piece.astype(BF16) for piece in pieces] + [ycg_ref[...]], axis=1)
    y = x_ref[...] + jnp.dot(mix, w_ref[...], preferred_element_type=F32)
    if final_norm:
        ms = jnp.mean(y * y, axis=-1, keepdims=True)
        y = y * lax.rsqrt(ms + NORM_EPS) * fg_ref[...]
    out_ref[...] = y


def _out_proj(ya, outs, lses, ycg, gate, x2d, w_bf16, final_g, final_norm, batch, seq):
    n = x2d.shape[0]
    tm = OUT_TILE
    seq_blocks = seq // tm
    row = lambda i: (i, 0)
    const = lambda i: (0, 0)
    perm_idx = lambda i: (i // seq_blocks, 0, i % seq_blocks, 0)
    o_specs = [pl.BlockSpec((tm, DIL_W), row)] + [
        pl.BlockSpec((1, d, tm // d, DIL_W), perm_idx) for d in DIL_DILATIONS[1:]]
    l_specs = [pl.BlockSpec((tm, LANES), row)] + [
        pl.BlockSpec((1, d, tm // d, LANES), perm_idx) for d in DIL_DILATIONS[1:]]
    return pl.pallas_call(
        functools.partial(_out_proj_kernel, final_norm=final_norm),
        out_shape=jax.ShapeDtypeStruct((n, D_MODEL), F32),
        grid=(n // tm,),
        in_specs=[pl.BlockSpec((tm, NA_W), row)] + o_specs + l_specs + [
            pl.BlockSpec((tm, CONV_CH), row),
            pl.BlockSpec((tm, NA_W + DIL_W), row),
            pl.BlockSpec((tm, D_MODEL), row),
            pl.BlockSpec((D_MIX, D_MODEL), const),
            pl.BlockSpec((1, D_MODEL), const),
        ],
        out_specs=pl.BlockSpec((tm, D_MODEL), row),
        scratch_shapes=[pltpu.VMEM((2, DIL_SLABS, tm, LANES), F32),
                        pltpu.VMEM((2, tm, LANES), F32)],
        compiler_params=_params("parallel"),
        name="out_proj",
    )(ya, *outs, *lses, ycg, gate, x2d, w_bf16, final_g)


def _rotary_tables(seq):
    half = HEAD_DIM // 2
    inv_freq = jnp.power(ROPE_THETA, -jnp.arange(half, dtype=F32) * 2.0 / HEAD_DIM)
    ang = jnp.arange(seq).astype(F32)[:, None] * inv_freq[None, :]
    cos = jnp.cos(ang)
    sin = jnp.sin(ang)
    zero = jnp.zeros_like(sin)
    reps = LANES // HEAD_DIM
    cos_t = jnp.tile(jnp.concatenate([cos, cos], axis=1), (1, reps))
    sa_t = jnp.tile(jnp.concatenate([-sin, zero], axis=1), (1, reps))
    sb_t = jnp.tile(jnp.concatenate([zero, sin], axis=1), (1, reps))
    return cos_t, sa_t, sb_t


def kernel(x, norm_g, w_in, na_rpb, conv_w, conv_b, conv_ln_g, conv_ln_b, pw_w, pw_b, w_out, final_g):
    batch, seq, _ = x.shape
    depth = w_in.shape[0]
    n = batch * seq
    cos_t, sa_t, sb_t = _rotary_tables(seq)
    h = x.reshape(n, D_MODEL)
    fg = final_g.reshape(1, D_MODEL)
    for l in range(depth):
        conv_params = (conv_w[l], conv_b[l].reshape(1, CONV_CH), conv_ln_g[l].reshape(1, CONV_CH),
                       conv_ln_b[l].reshape(1, CONV_CH), pw_w[l].astype(BF16), pw_b[l].reshape(1, CONV_CH))
        (naq, nak, nav, q1, q4, q16, k1, k4, k16, v1, v4, v16, ycg, gate) = _in_proj(
            h, norm_g[l].reshape(1, D_MODEL), w_in[l].astype(BF16), cos_t, sa_t, sb_t, conv_params, batch, seq)
        ya = _na_attention(naq, nak, nav, _na_bias_table(na_rpb[l]), batch, seq)
        outs, lses = [], []
        for d, (q, k, v) in zip(DIL_DILATIONS, ((q1, k1, v1), (q4, k4, v4), (q16, k16, v16))):
            groups = (batch * d, seq // d)
            o, lse = _band_attention(q.reshape(*groups, DIL_W), k.reshape(*groups, DIL_W),
                                     v.reshape(*groups, DIL_W))
            if d == 1:
                outs.append(o.reshape(n, DIL_W))
                lses.append(lse.reshape(n, LANES))
            else:
                outs.append(o.reshape(batch, d, seq // d, DIL_W))
                lses.append(lse.reshape(batch, d, seq // d, LANES))
        h = _out_proj(ya, outs, lses, ycg, gate, h, w_out[l].astype(BF16), fg,
                      final_norm=(l == depth - 1), batch=batch, seq=seq)
    return h.reshape(batch, seq, D_MODEL)
```

```python
import functools

import numpy as np
import jax
import jax.numpy as jnp
from jax import lax
from jax.experimental import pallas as pl
from jax.experimental.pallas import tpu as pltpu

D_MODEL = 1024
HEAD_DIM = 64
GRID_W = 64
NA_HEADS = 4
NA_WIN_ROWS = 8
NA_WIN_COLS = 16
DIL_HEADS = 6
DIL_DILATIONS = (1, 4, 16)
DIL_HALF = 64
CONV_CH = 384
CONV_WIDTH = 31
ROPE_THETA = 10000.0
NORM_EPS = 1e-6
NEG_INF = -1e30
NA_W = NA_HEADS * HEAD_DIM
DIL_W = DIL_HEADS * HEAD_DIM
D_MIX = NA_W + DIL_W + CONV_CH
SCALE = HEAD_DIM ** -0.5
LOG2E = 1.4426950408889634
Q_SCALE = SCALE * LOG2E

LANES = 128
SUBLANES = 8
VMEM_LIMIT = 56 * 1024 * 1024
DIL_SLABS = DIL_W // LANES

_C = np.cumsum([0] + [NA_W] * 4 + [DIL_W] * 4 + [CONV_CH] * 3)
(C_AQ, C_AK, C_AV, C_AG, C_BQ, C_BK, C_BV, C_BG, C_CA, C_CB, C_CG, C_END) = [int(c) for c in _C]

BF16 = jnp.bfloat16
F32 = jnp.float32

TOKEN_TILE = 512
OUT_TILE = 1024
CONV_HALO = 16
CONV_PAD = CONV_WIDTH // 2
CONV_ROWS = 128


def _params(*sem):
    return pltpu.CompilerParams(dimension_semantics=sem, vmem_limit_bytes=VMEM_LIMIT)


def _silu(v):
    return v * (1.0 / (1.0 + jnp.exp(-v)))


def _lane_is_first_head(shape):
    return lax.broadcasted_iota(jnp.int32, shape, len(shape) - 1) < HEAD_DIM


def _in_proj_kernel(xp_ref, x_ref, xn_ref, g_ref, w_ref, cos_ref, sa_ref, sb_ref,
                    cw_ref, cb_ref, lg_ref, lb_ref, pw_ref, pb_ref,
                    naq_ref, nak_ref, nav_ref,
                    q1_ref, q4_ref, q16_ref, k1_ref, k4_ref, k16_ref, v1_ref, v4_ref, v16_ref,
                    ycg_ref, gate_ref, perm, ubuf, ybuf, *, seq_blocks):
    tm = x_ref.shape[0]

    def normed(x):
        ms = jnp.mean(x * x, axis=-1, keepdims=True)
        return (x * lax.rsqrt(ms + NORM_EPS) * g_ref[...]).astype(BF16)

    h = normed(x_ref[...])

    def mm(c0, c1):
        return jnp.dot(h, w_ref[:, c0:c1], preferred_element_type=F32)

    cos = cos_ref[...]
    sa = sa_ref[...]
    sb = sb_ref[...]

    def rope(zs):
        up = pltpu.roll(zs, LANES - HEAD_DIM // 2, 1)
        dn = pltpu.roll(zs, HEAD_DIM // 2, 1)
        return zs * cos + up * sa + dn * sb

    def emit_dilated(z, refs, rotary, scale):
        ref1, ref4, ref16 = refs
        for s in range(DIL_SLABS):
            zs = z[:, s * LANES:(s + 1) * LANES]
            if rotary:
                zs = rope(zs)
            if scale != 1.0:
                zs = zs * scale
            perm[s] = zs
            ref1[:, s * LANES:(s + 1) * LANES] = zs.astype(BF16)
        for d, ref in ((4, ref4), (16, ref16)):
            for rho in range(d):
                for s in range(DIL_SLABS):
                    ref[0, rho, :, s * LANES:(s + 1) * LANES] = (
                        perm[s, pl.ds(rho, tm // d, stride=d), :].astype(BF16))

    i = pl.program_id(0)
    pos = i % seq_blocks
    h_ext = jnp.concatenate([normed(xp_ref[...]), h, normed(xn_ref[...])], axis=0)
    zc = jnp.dot(h_ext, w_ref[:, C_CA:C_CG], preferred_element_type=F32)
    u = zc[:, 0:CONV_CH] * (1.0 / (1.0 + jnp.exp(-zc[:, CONV_CH:2 * CONV_CH])))
    ubuf[0:CONV_HALO, :] = u[0:CONV_HALO] * jnp.where(pos > 0, 1.0, 0.0).astype(F32)
    ubuf[CONV_HALO:CONV_HALO + tm, :] = u[CONV_HALO:CONV_HALO + tm]
    ubuf[CONV_HALO + tm:, :] = u[CONV_HALO + tm:] * jnp.where(pos < seq_blocks - 1, 1.0, 0.0).astype(F32)

    first_off = CONV_HALO - CONV_PAD

    def conv_rows(c):
        r0 = c * CONV_ROWS
        for sl in range(CONV_CH // LANES):
            lanes = slice(sl * LANES, (sl + 1) * LANES)
            y = None
            for s in range(SUBLANES):
                part = None
                for k in range(CONV_WIDTH):
                    off = first_off + k
                    if off % SUBLANES != s:
                        continue
                    win = ubuf[pl.ds(r0 + (off - s), CONV_ROWS + SUBLANES), lanes]
                    term = win * cw_ref[k:k + 1, lanes]
                    part = term if part is None else part + term
                shifted = part[s:s + CONV_ROWS, :]
                y = shifted if y is None else y + shifted
            ybuf[pl.ds(r0, CONV_ROWS), lanes] = y

    conv_rows(0)
    za = mm(C_AQ, C_BQ)
    naq_ref[...] = (za[:, 0:NA_W] * Q_SCALE).astype(BF16)
    nak_ref[...] = za[:, NA_W:2 * NA_W].astype(BF16)
    nav_ref[...] = za[:, 2 * NA_W:3 * NA_W].astype(BF16)
    gate_ref[:, 0:NA_W] = _silu(za[:, 3 * NA_W:4 * NA_W]).astype(BF16)

    conv_rows(1)
    zqk = mm(C_BQ, C_BV)
    emit_dilated(zqk[:, 0:DIL_W], (q1_ref, q4_ref, q16_ref), True, Q_SCALE)
    emit_dilated(zqk[:, DIL_W:2 * DIL_W], (k1_ref, k4_ref, k16_ref), True, 1.0)

    conv_rows(2)
    zvg = mm(C_BV, C_CA)
    emit_dilated(zvg[:, 0:DIL_W], (v1_ref, v4_ref, v16_ref), False, 1.0)
    gate_ref[:, NA_W:NA_W + DIL_W] = _silu(zvg[:, DIL_W:2 * DIL_W]).astype(BF16)

    conv_rows(3)
    y = ybuf[...] + cb_ref[...]
    mu = jnp.mean(y, axis=-1, keepdims=True)
    yc = y - mu
    var = jnp.mean(yc * yc, axis=-1, keepdims=True)
    yn = yc * lax.rsqrt(var + NORM_EPS) * lg_ref[...] + lb_ref[...]
    act = _silu(yn).astype(BF16)
    yc = jnp.dot(act, pw_ref[...], preferred_element_type=F32) + pb_ref[...]
    ycg_ref[...] = (yc * _silu(mm(C_CG, C_END))).astype(BF16)


def _in_proj(x2d, g, w_bf16, cos_t, sa_t, sb_t, conv_params, batch, seq):
    n = x2d.shape[0]
    tm = TOKEN_TILE
    seq_blocks = seq // tm
    halo_per_tile = tm // CONV_HALO
    row = lambda i: (i, 0)
    prev = lambda i: (jnp.maximum(i * halo_per_tile - 1, 0), 0)
    nxt = lambda i: (jnp.minimum((i + 1) * halo_per_tile, n // CONV_HALO - 1), 0)
    tab = lambda i: (i % seq_blocks, 0)
    const = lambda i: (0, 0)
    perm_idx = lambda i: (i // seq_blocks, 0, i % seq_blocks, 0)
    vec = pl.BlockSpec((1, CONV_CH), const)

    def dil_shapes():
        return [jax.ShapeDtypeStruct((n, DIL_W), BF16)] + [
            jax.ShapeDtypeStruct((batch, d, seq // d, DIL_W), BF16) for d in DIL_DILATIONS[1:]]

    def dil_specs():
        return [pl.BlockSpec((tm, DIL_W), row)] + [
            pl.BlockSpec((1, d, tm // d, DIL_W), perm_idx) for d in DIL_DILATIONS[1:]]

    out_shapes = ([jax.ShapeDtypeStruct((n, NA_W), BF16)] * 3 + dil_shapes() + dil_shapes() + dil_shapes()
                  + [jax.ShapeDtypeStruct((n, CONV_CH), BF16), jax.ShapeDtypeStruct((n, NA_W + DIL_W), BF16)])
    out_specs = ([pl.BlockSpec((tm, NA_W), row)] * 3 + dil_specs() + dil_specs() + dil_specs()
                 + [pl.BlockSpec((tm, CONV_CH), row), pl.BlockSpec((tm, NA_W + DIL_W), row)])
    return pl.pallas_call(
        functools.partial(_in_proj_kernel, seq_blocks=seq_blocks),
        out_shape=out_shapes,
        grid=(n // tm,),
        in_specs=[
            pl.BlockSpec((CONV_HALO, D_MODEL), prev),
            pl.BlockSpec((tm, D_MODEL), row),
            pl.BlockSpec((CONV_HALO, D_MODEL), nxt),
            pl.BlockSpec((1, D_MODEL), const),
            pl.BlockSpec((D_MODEL, C_END), const),
            pl.BlockSpec((tm, LANES), tab),
            pl.BlockSpec((tm, LANES), tab),
            pl.BlockSpec((tm, LANES), tab),
            pl.BlockSpec((CONV_WIDTH, CONV_CH), const),
            vec, vec, vec,
            pl.BlockSpec((CONV_CH, CONV_CH), const),
            vec,
        ],
        out_specs=out_specs,
        scratch_shapes=[pltpu.VMEM((DIL_SLABS, tm, LANES), F32),
                        pltpu.VMEM((tm + 2 * CONV_HALO, CONV_CH), F32),
                        pltpu.VMEM((tm, CONV_CH), F32)],
        compiler_params=_params("parallel"),
        name="in_proj",
    )(x2d, x2d, x2d, g, w_bf16, cos_t, sa_t, sb_t, *conv_params)


NA_ROWS_PER_STEP = 8
NA_KEYS = NA_WIN_ROWS * GRID_W


def _na_kernel(q_ref, k_ref, v_ref, bias_ref, o_ref, sbuf, ebuf, *, rows):
    blk = pl.program_id(1)
    first = _lane_is_first_head((GRID_W, LANES))
    ones = jnp.ones((NA_KEYS, LANES), BF16)
    n_pairs = NA_W // LANES
    starts, maxes = [], []
    for rr in range(NA_ROWS_PER_STEP):
        r = blk * NA_ROWS_PER_STEP + rr
        r0 = jnp.clip(r - NA_WIN_ROWS // 2, 0, rows - NA_WIN_ROWS)
        delta = r0 - r + (NA_WIN_ROWS - 1)
        start = pl.multiple_of(r0 * GRID_W, GRID_W)
        starts.append(start)
        q_row = q_ref[0, rr * GRID_W:(rr + 1) * GRID_W, :]
        k_win = k_ref[0, pl.ds(start, NA_KEYS), :]
        for p in range(n_pairs):
            qp = q_row[:, p * LANES:(p + 1) * LANES]
            zero = jnp.zeros_like(qp)
            lhs = jnp.concatenate([jnp.where(first, qp, zero), jnp.where(first, zero, qp)], axis=0)
            kp = k_win[:, p * LANES:(p + 1) * LANES]
            s = lax.dot_general(lhs, kp, (((1,), (1,)), ((), ())), preferred_element_type=F32)
            s = s + bias_ref[delta, p]
            sbuf[rr * n_pairs + p] = s
            maxes.append(jnp.max(s, axis=-1, keepdims=True))
    for c in range(NA_ROWS_PER_STEP * n_pairs):
        ebuf[c] = jnp.exp2(sbuf[c] - maxes[c]).astype(BF16)
    for rr in range(NA_ROWS_PER_STEP):
        v_win = v_ref[0, pl.ds(starts[rr], NA_KEYS), :]
        for p in range(n_pairs):
            v_aug = jnp.concatenate([v_win[:, p * LANES:(p + 1) * LANES], ones], axis=1)
            o = jnp.dot(ebuf[rr * n_pairs + p], v_aug, preferred_element_type=F32)
            o = o[:, 0:LANES] * (1.0 / o[:, LANES:2 * LANES])
            out = jnp.where(first, o[0:GRID_W], o[GRID_W:2 * GRID_W])
            o_ref[0, rr * GRID_W:(rr + 1) * GRID_W, p * LANES:(p + 1) * LANES] = out.astype(BF16)


def _na_bias_table(rpb):
    cols = np.arange(GRID_W)
    col_start = np.clip(cols - NA_WIN_COLS // 2, 0, GRID_W - NA_WIN_COLS)
    kc = np.arange(GRID_W)
    in_win = (kc[None, :] >= col_start[:, None]) & (kc[None, :] < col_start[:, None] + NA_WIN_COLS)
    col_off = kc[None, :] - cols[:, None] + (NA_WIN_COLS - 1)
    n_off = 2 * NA_WIN_COLS - 1
    onehot = (col_off[None] == np.arange(n_off)[:, None, None]) & in_win[None]
    t = jnp.einsum('hrj,jck->hrck', rpb.astype(F32), jnp.asarray(onehot, F32),
                   precision=lax.Precision.HIGHEST)
    t = jnp.where(jnp.asarray(in_win)[None, None], t * LOG2E, NEG_INF)
    t = jnp.stack([t[:, dlt:dlt + NA_WIN_ROWS] for dlt in range(NA_WIN_ROWS)], axis=0)
    t = t.transpose(0, 1, 3, 2, 4)
    return t.reshape(NA_WIN_ROWS, NA_HEADS // 2, 2 * GRID_W, NA_KEYS)


def _na_attention(q, k, v, bias, batch, seq):
    rows = seq // GRID_W
    tq = NA_ROWS_PER_STEP * GRID_W
    q3 = q.reshape(batch, seq, NA_W)
    k3 = k.reshape(batch, seq, NA_W)
    v3 = v.reshape(batch, seq, NA_W)
    out = pl.pallas_call(
        functools.partial(_na_kernel, rows=rows),
        out_shape=jax.ShapeDtypeStruct((batch, seq, NA_W), BF16),
        grid=(batch, seq // tq),
        in_specs=[
            pl.BlockSpec((1, tq, NA_W), lambda b, i: (b, i, 0)),
            pl.BlockSpec((1, seq, NA_W), lambda b, i: (b, 0, 0)),
            pl.BlockSpec((1, seq, NA_W), lambda b, i: (b, 0, 0)),
            pl.BlockSpec(bias.shape, lambda b, i: (0, 0, 0, 0)),
        ],
        out_specs=pl.BlockSpec((1, tq, NA_W), lambda b, i: (b, i, 0)),
        scratch_shapes=[pltpu.VMEM((NA_ROWS_PER_STEP * NA_W // LANES, 2 * GRID_W, NA_KEYS), F32),
                        pltpu.VMEM((NA_ROWS_PER_STEP * NA_W // LANES, 2 * GRID_W, NA_KEYS), BF16)],
        compiler_params=_params("parallel", "arbitrary"),
        name="na_attention",
    )(q3, k3, v3, bias)
    return out.reshape(batch * seq, NA_W)


BAND_CHUNK = 128
BAND_KEYS = BAND_CHUNK + 2 * DIL_HALF
BAND_TQ_MAX = 1024
STAT_SUM_LANE = 8


def _band_kernel(q_ref, kp_ref, kc_ref, kn_ref, vp_ref, vc_ref, vn_ref, o_ref, stat_ref, *, length):
    blk = pl.program_id(1)
    BAND_TQ = q_ref.shape[1]
    n_chunks = BAND_TQ // BAND_CHUNK
    first = _lane_is_first_head((BAND_CHUNK, LANES))
    lane = lax.broadcasted_iota(jnp.int32, (BAND_CHUNK, LANES), 1)
    qi = lax.broadcasted_iota(jnp.int32, (BAND_CHUNK, BAND_KEYS), 0)
    kj = lax.broadcasted_iota(jnp.int32, (BAND_CHUNK, BAND_KEYS), 1)
    rel = kj - DIL_HALF - qi
    band = (rel >= -DIL_HALF) & (rel <= DIL_HALF)
    ones = jnp.ones((BAND_KEYS, LANES), BF16)
    for c in range(n_chunks):
        a0 = blk * BAND_TQ + c * BAND_CHUNK
        kpos = a0 - DIL_HALF + kj
        ok = band & (kpos >= 0) & (kpos < length)
        mask_bias = jnp.where(ok, 0.0, NEG_INF).astype(F32)
        mask_bias2 = jnp.concatenate([mask_bias, mask_bias], axis=0)
        lo = c * BAND_CHUNK - DIL_HALF
        hi = lo + BAND_KEYS

        def window(prev_ref, cur_ref, next_ref):
            pieces = []
            if lo < 0:
                pieces.append(prev_ref[0, :, :])
            pieces.append(cur_ref[0, max(lo, 0):min(hi, BAND_TQ), :])
            if hi > BAND_TQ:
                pieces.append(next_ref[0, :, :])
            return pieces[0] if len(pieces) == 1 else jnp.concatenate(pieces, axis=0)

        k_win = window(kp_ref, kc_ref, kn_ref)
        v_win = window(vp_ref, vc_ref, vn_ref)
        q_chunk = q_ref[0, c * BAND_CHUNK:(c + 1) * BAND_CHUNK, :]
        rows = slice(c * BAND_CHUNK, (c + 1) * BAND_CHUNK)
        stat_tile = jnp.zeros((BAND_CHUNK, LANES), F32)
        for p in range(DIL_SLABS):
            qp = q_chunk[:, p * LANES:(p + 1) * LANES]
            zero = jnp.zeros_like(qp)
            lhs = jnp.concatenate([jnp.where(first, qp, zero), jnp.where(first, zero, qp)], axis=0)
            kp = k_win[:, p * LANES:(p + 1) * LANES]
            vp = v_win[:, p * LANES:(p + 1) * LANES]
            s = lax.dot_general(lhs, kp, (((1,), (1,)), ((), ())), preferred_element_type=F32)
            s = s + mask_bias2
            m = jnp.max(s, axis=-1, keepdims=True)
            e = jnp.exp2(s - m).astype(BF16)
            o = jnp.dot(e, jnp.concatenate([vp, ones], axis=1), preferred_element_type=F32)
            l = o[:, LANES:2 * LANES]
            o_ref[0, rows, p * LANES:(p + 1) * LANES] = jnp.where(
                first, o[0:BAND_CHUNK, 0:LANES], o[BAND_CHUNK:, 0:LANES]).astype(BF16)
            for hh in range(2):
                half = slice(hh * BAND_CHUNK, (hh + 1) * BAND_CHUNK)
                stat_tile = jnp.where(lane == 2 * p + hh, m[half], stat_tile)
                stat_tile = jnp.where(lane == STAT_SUM_LANE + 2 * p + hh, l[half], stat_tile)
        stat_ref[0, rows, :] = stat_tile


def _band_attention(q, k, v):
    groups, length, _ = q.shape
    tq = min(BAND_TQ_MAX, length)
    halo_per_blk = tq // DIL_HALF
    n_halo = length // DIL_HALF
    cur = lambda g, i: (g, i, 0)
    prev = lambda g, i: (g, jnp.maximum(i * halo_per_blk - 1, 0), 0)
    nxt = lambda g, i: (g, jnp.minimum((i + 1) * halo_per_blk, n_halo - 1), 0)
    blk = pl.BlockSpec((1, tq, DIL_W), cur)
    halo_p = pl.BlockSpec((1, DIL_HALF, DIL_W), prev)
    halo_n = pl.BlockSpec((1, DIL_HALF, DIL_W), nxt)
    return pl.pallas_call(
        functools.partial(_band_kernel, length=length),
        out_shape=[jax.ShapeDtypeStruct((groups, length, DIL_W), BF16),
                   jax.ShapeDtypeStruct((groups, length, LANES), F32)],
        grid=(groups, length // tq),
        in_specs=[blk, halo_p, blk, halo_n, halo_p, blk, halo_n],
        out_specs=[pl.BlockSpec((1, tq, DIL_W), cur), pl.BlockSpec((1, tq, LANES), cur)],
        compiler_params=_params("parallel", "arbitrary"),
        name="band_attention",
    )(q, k, k, k, v, v, v)


def _out_proj_kernel(ya_ref, o1_ref, o4_ref, o16_ref, l1_ref, l4_ref, l16_ref, ycg_ref, gate_ref,
                     x_ref, w_ref, fg_ref, out_ref, onat, lnat, *, final_norm):
    tm = x_ref.shape[0]
    first = _lane_is_first_head((tm, LANES))

    def natural_stats(l_ref, d, slot):
        if d == 1:
            return l_ref[...]
        for rho in range(d):
            lnat[slot, pl.ds(rho, tm // d, stride=d), :] = l_ref[0, rho]
        return lnat[slot]

    stats = (natural_stats(l1_ref, 1, 0), natural_stats(l4_ref, 4, 0), natural_stats(l16_ref, 16, 1))
    m = jnp.maximum(jnp.maximum(stats[0], stats[1]), stats[2])
    scales = [jnp.exp2(st - m) for st in stats]
    denom = None
    for st, sc in zip(stats, scales):
        row_sum = pltpu.roll(st, LANES - STAT_SUM_LANE, 1)
        denom = sc * row_sum if denom is None else denom + sc * row_sum
    inv = 1.0 / denom
    weights = [sc * inv for sc in scales]

    for slot, (d, o_ref) in enumerate(((4, o4_ref), (16, o16_ref))):
        for rho in range(d):
            for s in range(DIL_SLABS):
                onat[slot, s, pl.ds(rho, tm // d, stride=d), :] = (
                    o_ref[0, rho, :, s * LANES:(s + 1) * LANES].astype(F32))

    gate = gate_ref[...].astype(F32)
    pieces = [ya_ref[...].astype(F32) * gate[:, 0:NA_W]]
    for s in range(DIL_SLABS):
        outs = (o1_ref[:, s * LANES:(s + 1) * LANES].astype(F32), onat[0, s], onat[1, s])
        yb = None
        for w, o in zip(weights, outs):
            wb = jnp.where(first, w[:, 2 * s:2 * s + 1], w[:, 2 * s + 1:2 * s + 2])
            yb = wb * o if yb is None else yb + wb * o
        pieces.append(yb * gate[:, NA_W + s * LANES:NA_W + (s + 1) * LANES])
    mix = jnp.concatenate([piece.astype(BF16) for piece in pieces] + [ycg_ref[...]], axis=1)
    y = x_ref[...] + jnp.dot(mix, w_ref[...], preferred_element_type=F32)
    if final_norm:
        ms = jnp.mean(y * y, axis=-1, keepdims=True)
        y = y * lax.rsqrt(ms + NORM_EPS) * fg_ref[...]
    out_ref[...] = y


def _out_proj(ya, outs, lses, ycg, gate, x2d, w_bf16, final_g, final_norm, batch, seq):
    n = x2d.shape[0]
    tm = OUT_TILE
    seq_blocks = seq // tm
    row = lambda i: (i, 0)
    const = lambda i: (0, 0)
    perm_idx = lambda i: (i // seq_blocks, 0, i % seq_blocks, 0)
    o_specs = [pl.BlockSpec((tm, DIL_W), row)] + [
        pl.BlockSpec((1, d, tm // d, DIL_W), perm_idx) for d in DIL_DILATIONS[1:]]
    l_specs = [pl.BlockSpec((tm, LANES), row)] + [
        pl.BlockSpec((1, d, tm // d, LANES), perm_idx) for d in DIL_DILATIONS[1:]]
    return pl.pallas_call(
        functools.partial(_out_proj_kernel, final_norm=final_norm),
        out_shape=jax.ShapeDtypeStruct((n, D_MODEL), F32),
        grid=(n // tm,),
        in_specs=[pl.BlockSpec((tm, NA_W), row)] + o_specs + l_specs + [
            pl.BlockSpec((tm, CONV_CH), row),
            pl.BlockSpec((tm, NA_W + DIL_W), row),
            pl.BlockSpec((tm, D_MODEL), row),
            pl.BlockSpec((D_MIX, D_MODEL), const),
            pl.BlockSpec((1, D_MODEL), const),
        ],
        out_specs=pl.BlockSpec((tm, D_MODEL), row),
        scratch_shapes=[pltpu.VMEM((2, DIL_SLABS, tm, LANES), F32),
                        pltpu.VMEM((2, tm, LANES), F32)],
        compiler_params=_params("parallel"),
        name="out_proj",
    )(ya, *outs, *lses, ycg, gate, x2d, w_bf16, final_g)


def _rotary_tables(seq):
    half = HEAD_DIM // 2
    inv_freq = jnp.power(ROPE_THETA, -jnp.arange(half, dtype=F32) * 2.0 / HEAD_DIM)
    ang = jnp.arange(seq).astype(F32)[:, None] * inv_freq[None, :]
    cos = jnp.cos(ang)
    sin = jnp.sin(ang)
    zero = jnp.zeros_like(sin)
    reps = LANES // HEAD_DIM
    cos_t = jnp.tile(jnp.concatenate([cos, cos], axis=1), (1, reps))
    sa_t = jnp.tile(jnp.concatenate([-sin, zero], axis=1), (1, reps))
    sb_t = jnp.tile(jnp.concatenate([zero, sin], axis=1), (1, reps))
    return cos_t, sa_t, sb_t


def kernel(x, norm_g, w_in, na_rpb, conv_w, conv_b, conv_ln_g, conv_ln_b, pw_w, pw_b, w_out, final_g):
    batch, seq, _ = x.shape
    depth = w_in.shape[0]
    n = batch * seq
    cos_t, sa_t, sb_t = _rotary_tables(seq)
    h = x.reshape(n, D_MODEL)
    fg = final_g.reshape(1, D_MODEL)
    for l in range(depth):
        conv_params = (conv_w[l], conv_b[l].reshape(1, CONV_CH), conv_ln_g[l].reshape(1, CONV_CH),
                       conv_ln_b[l].reshape(1, CONV_CH), pw_w[l].astype(BF16), pw_b[l].reshape(1, CONV_CH))
        (naq, nak, nav, q1, q4, q16, k1, k4, k16, v1, v4, v16, ycg, gate) = _in_proj(
            h, norm_g[l].reshape(1, D_MODEL), w_in[l].astype(BF16), cos_t, sa_t, sb_t, conv_params, batch, seq)
        ya = _na_attention(naq, nak, nav, _na_bias_table(na_rpb[l]), batch, seq)
        outs, lses = [], []
        for d, (q, k, v) in zip(DIL_DILATIONS, ((q1, k1, v1), (q4, k4, v4), (q16, k16, v16))):
            groups = (batch * d, seq // d)
            o, lse = _band_attention(q.reshape(*groups, DIL_W), k.reshape(*groups, DIL_W),
                                     v.reshape(*groups, DIL_W))
            if d == 1:
                outs.append(o.reshape(n, DIL_W))
                lses.append(lse.reshape(n, LANES))
            else:
                outs.append(o.reshape(batch, d, seq // d, DIL_W))
                lses.append(lse.reshape(batch, d, seq // d, LANES))
        h = _out_proj(ya, outs, lses, ycg, gate, h, w_out[l].astype(BF16), fg,
                      final_norm=(l == depth - 1), batch=batch, seq=seq)
    return h.reshape(batch, seq, D_MODEL)
```

```python
import functools

import numpy as np
import jax
import jax.numpy as jnp
from jax import lax
from jax.experimental import pallas as pl
from jax.experimental.pallas import tpu as pltpu

D_MODEL = 1024
HEAD_DIM = 64
GRID_W = 64
NA_HEADS = 4
NA_WIN_ROWS = 8
NA_WIN_COLS = 16
DIL_HEADS = 6
DIL_DILATIONS = (1, 4, 16)
DIL_HALF = 64
CONV_CH = 384
CONV_WIDTH = 31
ROPE_THETA = 10000.0
NORM_EPS = 1e-6
NEG_INF = -1e30
NA_W = NA_HEADS * HEAD_DIM
DIL_W = DIL_HEADS * HEAD_DIM
D_MIX = NA_W + DIL_W + CONV_CH
SCALE = HEAD_DIM ** -0.5
LOG2E = 1.4426950408889634
Q_SCALE = SCALE * LOG2E

LANES = 128
SUBLANES = 8
VMEM_LIMIT = 56 * 1024 * 1024
DIL_SLABS = DIL_W // LANES

_C = np.cumsum([0] + [NA_W] * 4 + [DIL_W] * 4 + [CONV_CH] * 3)
(C_AQ, C_AK, C_AV, C_AG, C_BQ, C_BK, C_BV, C_BG, C_CA, C_CB, C_CG, C_END) = [int(c) for c in _C]

BF16 = jnp.bfloat16
F32 = jnp.float32

TOKEN_TILE = 1024
OUT_TILE = 1024
CONV_HALO = 16
CONV_PAD = CONV_WIDTH // 2
CONV_ROWS = 128


def _params(*sem):
    return pltpu.CompilerParams(dimension_semantics=sem, vmem_limit_bytes=VMEM_LIMIT)


def _silu(v):
    return v * (1.0 / (1.0 + jnp.exp(-v)))


def _lane_is_first_head(shape):
    return lax.broadcasted_iota(jnp.int32, shape, len(shape) - 1) < HEAD_DIM


def _in_proj_kernel(xp_ref, x_ref, xn_ref, g_ref, w_ref, cos_ref, sa_ref, sb_ref,
                    cw_ref, cb_ref, lg_ref, lb_ref, pw_ref, pb_ref,
                    naq_ref, nak_ref, nav_ref,
                    q1_ref, q4_ref, q16_ref, k1_ref, k4_ref, k16_ref, v1_ref, v4_ref, v16_ref,
                    ycg_ref, gate_ref, perm, ubuf, ybuf, *, seq_blocks):
    tm = x_ref.shape[0]

    def normed(x):
        ms = jnp.mean(x * x, axis=-1, keepdims=True)
        return (x * lax.rsqrt(ms + NORM_EPS) * g_ref[...]).astype(BF16)

    h = normed(x_ref[...])

    def mm(c0, c1):
        return jnp.dot(h, w_ref[:, c0:c1], preferred_element_type=F32)

    cos = cos_ref[...]
    sa = sa_ref[...]
    sb = sb_ref[...]

    def rope(zs):
        up = pltpu.roll(zs, LANES - HEAD_DIM // 2, 1)
        dn = pltpu.roll(zs, HEAD_DIM // 2, 1)
        return zs * cos + up * sa + dn * sb

    def emit_dilated(z, refs, rotary, scale):
        ref1, ref4, ref16 = refs
        for s in range(DIL_SLABS):
            zs = z[:, s * LANES:(s + 1) * LANES]
            if rotary:
                zs = rope(zs)
            if scale != 1.0:
                zs = zs * scale
            perm[s] = zs
            ref1[:, s * LANES:(s + 1) * LANES] = zs.astype(BF16)
        for d, ref in ((4, ref4), (16, ref16)):
            for rho in range(d):
                for s in range(DIL_SLABS):
                    ref[0, rho, :, s * LANES:(s + 1) * LANES] = (
                        perm[s, pl.ds(rho, tm // d, stride=d), :].astype(BF16))

    i = pl.program_id(0)
    pos = i % seq_blocks
    h_ext = jnp.concatenate([normed(xp_ref[...]), h, normed(xn_ref[...])], axis=0)
    zc = jnp.dot(h_ext, w_ref[:, C_CA:C_CG], preferred_element_type=F32)
    u = zc[:, 0:CONV_CH] * (1.0 / (1.0 + jnp.exp(-zc[:, CONV_CH:2 * CONV_CH])))
    ubuf[0:CONV_HALO, :] = u[0:CONV_HALO] * jnp.where(pos > 0, 1.0, 0.0).astype(F32)
    ubuf[CONV_HALO:CONV_HALO + tm, :] = u[CONV_HALO:CONV_HALO + tm]
    ubuf[CONV_HALO + tm:, :] = u[CONV_HALO + tm:] * jnp.where(pos < seq_blocks - 1, 1.0, 0.0).astype(F32)

    first_off = CONV_HALO - CONV_PAD
    for c in range(tm // CONV_ROWS):
        r0 = c * CONV_ROWS
        for sl in range(CONV_CH // LANES):
            lanes = slice(sl * LANES, (sl + 1) * LANES)
            y = None
            for s in range(SUBLANES):
                part = None
                for k in range(CONV_WIDTH):
                    off = first_off + k
                    if off % SUBLANES != s:
                        continue
                    win = ubuf[pl.ds(r0 + (off - s), CONV_ROWS + SUBLANES), lanes]
                    term = win * cw_ref[k:k + 1, lanes]
                    part = term if part is None else part + term
                shifted = part[s:s + CONV_ROWS, :]
                y = shifted if y is None else y + shifted
            ybuf[pl.ds(r0, CONV_ROWS), lanes] = y

    za = mm(C_AQ, C_BQ)
    naq_ref[...] = (za[:, 0:NA_W] * Q_SCALE).astype(BF16)
    nak_ref[...] = za[:, NA_W:2 * NA_W].astype(BF16)
    nav_ref[...] = za[:, 2 * NA_W:3 * NA_W].astype(BF16)
    gate_ref[:, 0:NA_W] = _silu(za[:, 3 * NA_W:4 * NA_W]).astype(BF16)

    zqk = mm(C_BQ, C_BV)
    emit_dilated(zqk[:, 0:DIL_W], (q1_ref, q4_ref, q16_ref), True, Q_SCALE)
    emit_dilated(zqk[:, DIL_W:2 * DIL_W], (k1_ref, k4_ref, k16_ref), True, 1.0)

    zvg = mm(C_BV, C_CA)
    emit_dilated(zvg[:, 0:DIL_W], (v1_ref, v4_ref, v16_ref), False, 1.0)
    gate_ref[:, NA_W:NA_W + DIL_W] = _silu(zvg[:, DIL_W:2 * DIL_W]).astype(BF16)

    y = ybuf[...] + cb_ref[...]
    mu = jnp.mean(y, axis=-1, keepdims=True)
    yc = y - mu
    var = jnp.mean(yc * yc, axis=-1, keepdims=True)
    yn = yc * lax.rsqrt(var + NORM_EPS) * lg_ref[...] + lb_ref[...]
    act = _silu(yn).astype(BF16)
    yc = jnp.dot(act, pw_ref[...], preferred_element_type=F32) + pb_ref[...]
    ycg_ref[...] = (yc * _silu(mm(C_CG, C_END))).astype(BF16)


def _in_proj(x2d, g, w_bf16, cos_t, sa_t, sb_t, conv_params, batch, seq):
    n = x2d.shape[0]
    tm = TOKEN_TILE
    seq_blocks = seq // tm
    halo_per_tile = tm // CONV_HALO
    row = lambda i: (i, 0)
    prev = lambda i: (jnp.maximum(i * halo_per_tile - 1, 0), 0)
    nxt = lambda i: (jnp.minimum((i + 1) * halo_per_tile, n // CONV_HALO - 1), 0)
    tab = lambda i: (i % seq_blocks, 0)
    const = lambda i: (0, 0)
    perm_idx = lambda i: (i // seq_blocks, 0, i % seq_blocks, 0)
    vec = pl.BlockSpec((1, CONV_CH), const)

    def dil_shapes():
        return [jax.ShapeDtypeStruct((n, DIL_W), BF16)] + [
            jax.ShapeDtypeStruct((batch, d, seq // d, DIL_W), BF16) for d in DIL_DILATIONS[1:]]

    def dil_specs():
        return [pl.BlockSpec((tm, DIL_W), row)] + [
            pl.BlockSpec((1, d, tm // d, DIL_W), perm_idx) for d in DIL_DILATIONS[1:]]

    out_shapes = ([jax.ShapeDtypeStruct((n, NA_W), BF16)] * 3 + dil_shapes() + dil_shapes() + dil_shapes()
                  + [jax.ShapeDtypeStruct((n, CONV_CH), BF16), jax.ShapeDtypeStruct((n, NA_W + DIL_W), BF16)])
    out_specs = ([pl.BlockSpec((tm, NA_W), row)] * 3 + dil_specs() + dil_specs() + dil_specs()
                 + [pl.BlockSpec((tm, CONV_CH), row), pl.BlockSpec((tm, NA_W + DIL_W), row)])
    return pl.pallas_call(
        functools.partial(_in_proj_kernel, seq_blocks=seq_blocks),
        out_shape=out_shapes,
        grid=(n // tm,),
        in_specs=[
            pl.BlockSpec((CONV_HALO, D_MODEL), prev),
            pl.BlockSpec((tm, D_MODEL), row),
            pl.BlockSpec((CONV_HALO, D_MODEL), nxt),
            pl.BlockSpec((1, D_MODEL), const),
            pl.BlockSpec((D_MODEL, C_END), const),
            pl.BlockSpec((tm, LANES), tab),
            pl.BlockSpec((tm, LANES), tab),
            pl.BlockSpec((tm, LANES), tab),
            pl.BlockSpec((CONV_WIDTH, CONV_CH), const),
            vec, vec, vec,
            pl.BlockSpec((CONV_CH, CONV_CH), const),
            vec,
        ],
        out_specs=out_specs,
        scratch_shapes=[pltpu.VMEM((DIL_SLABS, tm, LANES), F32),
                        pltpu.VMEM((tm + 2 * CONV_HALO, CONV_CH), F32),
                        pltpu.VMEM((tm, CONV_CH), F32)],
        compiler_params=_params("parallel"),
        name="in_proj",
    )(x2d, x2d, x2d, g, w_bf16, cos_t, sa_t, sb_t, *conv_params)


NA_ROWS_PER_STEP = 16
NA_KEYS = NA_WIN_ROWS * GRID_W


def _na_kernel(q_ref, k_ref, v_ref, bias_ref, o_ref, sbuf, ebuf, *, rows):
    blk = pl.program_id(1)
    first = _lane_is_first_head((GRID_W, LANES))
    ones = jnp.ones((NA_KEYS, LANES), BF16)
    n_pairs = NA_W // LANES
    starts, maxes = [], []
    for rr in range(NA_ROWS_PER_STEP):
        r = blk * NA_ROWS_PER_STEP + rr
        r0 = jnp.clip(r - NA_WIN_ROWS // 2, 0, rows - NA_WIN_ROWS)
        delta = r0 - r + (NA_WIN_ROWS - 1)
        start = pl.multiple_of(r0 * GRID_W, GRID_W)
        starts.append(start)
        q_row = q_ref[0, rr * GRID_W:(rr + 1) * GRID_W, :]
        k_win = k_ref[0, pl.ds(start, NA_KEYS), :]
        for p in range(n_pairs):
            qp = q_row[:, p * LANES:(p + 1) * LANES]
            zero = jnp.zeros_like(qp)
            lhs = jnp.concatenate([jnp.where(first, qp, zero), jnp.where(first, zero, qp)], axis=0)
            kp = k_win[:, p * LANES:(p + 1) * LANES]
            s = lax.dot_general(lhs, kp, (((1,), (1,)), ((), ())), preferred_element_type=F32)
            s = s + bias_ref[delta, p]
            sbuf[rr * n_pairs + p] = s
            maxes.append(jnp.max(s, axis=-1, keepdims=True))
    for c in range(NA_ROWS_PER_STEP * n_pairs):
        ebuf[c] = jnp.exp2(sbuf[c] - maxes[c]).astype(BF16)
    for rr in range(NA_ROWS_PER_STEP):
        v_win = v_ref[0, pl.ds(starts[rr], NA_KEYS), :]
        for p in range(n_pairs):
            v_aug = jnp.concatenate([v_win[:, p * LANES:(p + 1) * LANES], ones], axis=1)
            o = jnp.dot(ebuf[rr * n_pairs + p], v_aug, preferred_element_type=F32)
            o = o[:, 0:LANES] * (1.0 / o[:, LANES:2 * LANES])
            out = jnp.where(first, o[0:GRID_W], o[GRID_W:2 * GRID_W])
            o_ref[0, rr * GRID_W:(rr + 1) * GRID_W, p * LANES:(p + 1) * LANES] = out.astype(BF16)


def _na_bias_table(rpb):
    cols = np.arange(GRID_W)
    col_start = np.clip(cols - NA_WIN_COLS // 2, 0, GRID_W - NA_WIN_COLS)
    kc = np.arange(GRID_W)
    in_win = (kc[None, :] >= col_start[:, None]) & (kc[None, :] < col_start[:, None] + NA_WIN_COLS)
    col_off = kc[None, :] - cols[:, None] + (NA_WIN_COLS - 1)
    n_off = 2 * NA_WIN_COLS - 1
    onehot = (col_off[None] == np.arange(n_off)[:, None, None]) & in_win[None]
    t = jnp.einsum('hrj,jck->hrck', rpb.astype(F32), jnp.asarray(onehot, F32),
                   precision=lax.Precision.HIGHEST)
    t = jnp.where(jnp.asarray(in_win)[None, None], t * LOG2E, NEG_INF)
    t = jnp.stack([t[:, dlt:dlt + NA_WIN_ROWS] for dlt in range(NA_WIN_ROWS)], axis=0)
    t = t.transpose(0, 1, 3, 2, 4)
    return t.reshape(NA_WIN_ROWS, NA_HEADS // 2, 2 * GRID_W, NA_KEYS)


def _na_attention(q, k, v, bias, batch, seq):
    rows = seq // GRID_W
    tq = NA_ROWS_PER_STEP * GRID_W
    q3 = q.reshape(batch, seq, NA_W)
    k3 = k.reshape(batch, seq, NA_W)
    v3 = v.reshape(batch, seq, NA_W)
    out = pl.pallas_call(
        functools.partial(_na_kernel, rows=rows),
        out_shape=jax.ShapeDtypeStruct((batch, seq, NA_W), BF16),
        grid=(batch, seq // tq),
        in_specs=[
            pl.BlockSpec((1, tq, NA_W), lambda b, i: (b, i, 0)),
            pl.BlockSpec((1, seq, NA_W), lambda b, i: (b, 0, 0)),
            pl.BlockSpec((1, seq, NA_W), lambda b, i: (b, 0, 0)),
            pl.BlockSpec(bias.shape, lambda b, i: (0, 0, 0, 0)),
        ],
        out_specs=pl.BlockSpec((1, tq, NA_W), lambda b, i: (b, i, 0)),
        scratch_shapes=[pltpu.VMEM((NA_ROWS_PER_STEP * NA_W // LANES, 2 * GRID_W, NA_KEYS), F32),
                        pltpu.VMEM((NA_ROWS_PER_STEP * NA_W // LANES, 2 * GRID_W, NA_KEYS), BF16)],
        compiler_params=_params("parallel", "arbitrary"),
        name="na_attention",
    )(q3, k3, v3, bias)
    return out.reshape(batch * seq, NA_W)


BAND_CHUNK = 128
BAND_KEYS = BAND_CHUNK + 2 * DIL_HALF
BAND_TQ_MAX = 1024
STAT_SUM_LANE = 8


def _band_kernel(q_ref, kp_ref, kc_ref, kn_ref, vp_ref, vc_ref, vn_ref, o_ref, stat_ref, *, length):
    blk = pl.program_id(1)
    BAND_TQ = q_ref.shape[1]
    n_chunks = BAND_TQ // BAND_CHUNK
    first = _lane_is_first_head((BAND_CHUNK, LANES))
    lane = lax.broadcasted_iota(jnp.int32, (BAND_CHUNK, LANES), 1)
    qi = lax.broadcasted_iota(jnp.int32, (BAND_CHUNK, BAND_KEYS), 0)
    kj = lax.broadcasted_iota(jnp.int32, (BAND_CHUNK, BAND_KEYS), 1)
    rel = kj - DIL_HALF - qi
    band = (rel >= -DIL_HALF) & (rel <= DIL_HALF)
    ones = jnp.ones((BAND_KEYS, LANES), BF16)
    for c in range(n_chunks):
        a0 = blk * BAND_TQ + c * BAND_CHUNK
        kpos = a0 - DIL_HALF + kj
        ok = band & (kpos >= 0) & (kpos < length)
        mask_bias = jnp.where(ok, 0.0, NEG_INF).astype(F32)
        mask_bias2 = jnp.concatenate([mask_bias, mask_bias], axis=0)
        lo = c * BAND_CHUNK - DIL_HALF
        hi = lo + BAND_KEYS

        def window(prev_ref, cur_ref, next_ref):
            pieces = []
            if lo < 0:
                pieces.append(prev_ref[0, :, :])
            pieces.append(cur_ref[0, max(lo, 0):min(hi, BAND_TQ), :])
            if hi > BAND_TQ:
                pieces.append(next_ref[0, :, :])
            return pieces[0] if len(pieces) == 1 else jnp.concatenate(pieces, axis=0)

        k_win = window(kp_ref, kc_ref, kn_ref)
        v_win = window(vp_ref, vc_ref, vn_ref)
        q_chunk = q_ref[0, c * BAND_CHUNK:(c + 1) * BAND_CHUNK, :]
        rows = slice(c * BAND_CHUNK, (c + 1) * BAND_CHUNK)
        stat_tile = jnp.zeros((BAND_CHUNK, LANES), F32)
        for p in range(DIL_SLABS):
            qp = q_chunk[:, p * LANES:(p + 1) * LANES]
            zero = jnp.zeros_like(qp)
            lhs = jnp.concatenate([jnp.where(first, qp, zero), jnp.where(first, zero, qp)], axis=0)
            kp = k_win[:, p * LANES:(p + 1) * LANES]
            vp = v_win[:, p * LANES:(p + 1) * LANES]
            s = lax.dot_general(lhs, kp, (((1,), (1,)), ((), ())), preferred_element_type=F32)
            s = s + mask_bias2
            m = jnp.max(s, axis=-1, keepdims=True)
            e = jnp.exp2(s - m).astype(BF16)
            o = jnp.dot(e, jnp.concatenate([vp, ones], axis=1), preferred_element_type=F32)
            l = o[:, LANES:2 * LANES]
            o_ref[0, rows, p * LANES:(p + 1) * LANES] = jnp.where(
                first, o[0:BAND_CHUNK, 0:LANES], o[BAND_CHUNK:, 0:LANES]).astype(BF16)
            for hh in range(2):
                half = slice(hh * BAND_CHUNK, (hh + 1) * BAND_CHUNK)
                stat_tile = jnp.where(lane == 2 * p + hh, m[half], stat_tile)
                stat_tile = jnp.where(lane == STAT_SUM_LANE + 2 * p + hh, l[half], stat_tile)
        stat_ref[0, rows, :] = stat_tile


def _band_attention(q, k, v):
    groups, length, _ = q.shape
    tq = min(BAND_TQ_MAX, length)
    halo_per_blk = tq // DIL_HALF
    n_halo = length // DIL_HALF
    cur = lambda g, i: (g, i, 0)
    prev = lambda g, i: (g, jnp.maximum(i * halo_per_blk - 1, 0), 0)
    nxt = lambda g, i: (g, jnp.minimum((i + 1) * halo_per_blk, n_halo - 1), 0)
    blk = pl.BlockSpec((1, tq, DIL_W), cur)
    halo_p = pl.BlockSpec((1, DIL_HALF, DIL_W), prev)
    halo_n = pl.BlockSpec((1, DIL_HALF, DIL_W), nxt)
    return pl.pallas_call(
        functools.partial(_band_kernel, length=length),
        out_shape=[jax.ShapeDtypeStruct((groups, length, DIL_W), BF16),
                   jax.ShapeDtypeStruct((groups, length, LANES), F32)],
        grid=(groups, length // tq),
        in_specs=[blk, halo_p, blk, halo_n, halo_p, blk, halo_n],
        out_specs=[pl.BlockSpec((1, tq, DIL_W), cur), pl.BlockSpec((1, tq, LANES), cur)],
        compiler_params=_params("parallel", "arbitrary"),
        name="band_attention",
    )(q, k, k, k, v, v, v)


def _out_proj_kernel(ya_ref, o1_ref, o4_ref, o16_ref, l1_ref, l4_ref, l16_ref, ycg_ref, gate_ref,
                     x_ref, w_ref, fg_ref, out_ref, onat, lnat, *, final_norm):
    tm = x_ref.shape[0]
    first = _lane_is_first_head((tm, LANES))

    def natural_stats(l_ref, d, slot):
        if d == 1:
            return l_ref[...]
        for rho in range(d):
            lnat[slot, pl.ds(rho, tm // d, stride=d), :] = l_ref[0, rho]
        return lnat[slot]

    stats = (natural_stats(l1_ref, 1, 0), natural_stats(l4_ref, 4, 0), natural_stats(l16_ref, 16, 1))
    m = jnp.maximum(jnp.maximum(stats[0], stats[1]), stats[2])
    scales = [jnp.exp2(st - m) for st in stats]
    denom = None
    for st, sc in zip(stats, scales):
        row_sum = pltpu.roll(st, LANES - STAT_SUM_LANE, 1)
        denom = sc * row_sum if denom is None else denom + sc * row_sum
    inv = 1.0 / denom
    weights = [sc * inv for sc in scales]

    for slot, (d, o_ref) in enumerate(((4, o4_ref), (16, o16_ref))):
        for rho in range(d):
            for s in range(DIL_SLABS):
                onat[slot, s, pl.ds(rho, tm // d, stride=d), :] = (
                    o_ref[0, rho, :, s * LANES:(s + 1) * LANES].astype(F32))

    gate = gate_ref[...].astype(F32)
    pieces = [ya_ref[...].astype(F32) * gate[:, 0:NA_W]]
    for s in range(DIL_SLABS):
        outs = (o1_ref[:, s * LANES:(s + 1) * LANES].astype(F32), onat[0, s], onat[1, s])
        yb = None
        for w, o in zip(weights, outs):
            wb = jnp.where(first, w[:, 2 * s:2 * s + 1], w[:, 2 * s + 1:2 * s + 2])
            yb = wb * o if yb is None else yb + wb * o
        pieces.append(yb * gate[:, NA_W + s * LANES:NA_W + (s + 1) * LANES])
    mix = jnp.concatenate([piece.astype(BF16) for piece in pieces] + [ycg_ref[...]], axis=1)
    y = x_ref[...] + jnp.dot(mix, w_ref[...], preferred_element_type=F32)
    if final_norm:
        ms = jnp.mean(y * y, axis=-1, keepdims=True)
        y = y * lax.rsqrt(ms + NORM_EPS) * fg_ref[...]
    out_ref[...] = y


def _out_proj(ya, outs, lses, ycg, gate, x2d, w_bf16, final_g, final_norm, batch, seq):
    n = x2d.shape[0]
    tm = OUT_TILE
    seq_blocks = seq // tm
    row = lambda i: (i, 0)
    const = lambda i: (0, 0)
    perm_idx = lambda i: (i // seq_blocks, 0, i % seq_blocks, 0)
    o_specs = [pl.BlockSpec((tm, DIL_W), row)] + [
        pl.BlockSpec((1, d, tm // d, DIL_W), perm_idx) for d in DIL_DILATIONS[1:]]
    l_specs = [pl.BlockSpec((tm, LANES), row)] + [
        pl.BlockSpec((1, d, tm // d, LANES), perm_idx) for d in DIL_DILATIONS[1:]]
    return pl.pallas_call(
        functools.partial(_out_proj_kernel, final_norm=final_norm),
        out_shape=jax.ShapeDtypeStruct((n, D_MODEL), F32),
        grid=(n // tm,),
        in_specs=[pl.BlockSpec((tm, NA_W), row)] + o_specs + l_specs + [
            pl.BlockSpec((tm, CONV_CH), row),
            pl.BlockSpec((tm, NA_W + DIL_W), row),
            pl.BlockSpec((tm, D_MODEL), row),
            pl.BlockSpec((D_MIX, D_MODEL), const),
            pl.BlockSpec((1, D_MODEL), const),
        ],
        out_specs=pl.BlockSpec((tm, D_MODEL), row),
        scratch_shapes=[pltpu.VMEM((2, DIL_SLABS, tm, LANES), F32),
                        pltpu.VMEM((2, tm, LANES), F32)],
        compiler_params=_params("parallel"),
        name="out_proj",
    )(ya, *outs, *lses, ycg, gate, x2d, w_bf16, final_g)


def _rotary_tables(seq):
    half = HEAD_DIM // 2
    inv_freq = jnp.power(ROPE_THETA, -jnp.arange(half, dtype=F32) * 2.0 / HEAD_DIM)
    ang = jnp.arange(seq).astype(F32)[:, None] * inv_freq[None, :]
    cos = jnp.cos(ang)
    sin = jnp.sin(ang)
    zero = jnp.zeros_like(sin)
    reps = LANES // HEAD_DIM
    cos_t = jnp.tile(jnp.concatenate([cos, cos], axis=1), (1, reps))
    sa_t = jnp.tile(jnp.concatenate([-sin, zero], axis=1), (1, reps))
    sb_t = jnp.tile(jnp.concatenate([zero, sin], axis=1), (1, reps))
    return cos_t, sa_t, sb_t


def kernel(x, norm_g, w_in, na_rpb, conv_w, conv_b, conv_ln_g, conv_ln_b, pw_w, pw_b, w_out, final_g):
    batch, seq, _ = x.shape
    depth = w_in.shape[0]
    n = batch * seq
    cos_t, sa_t, sb_t = _rotary_tables(seq)
    h = x.reshape(n, D_MODEL)
    fg = final_g.reshape(1, D_MODEL)
    for l in range(depth):
        conv_params = (conv_w[l], conv_b[l].reshape(1, CONV_CH), conv_ln_g[l].reshape(1, CONV_CH),
                       conv_ln_b[l].reshape(1, CONV_CH), pw_w[l].astype(BF16), pw_b[l].reshape(1, CONV_CH))
        (naq, nak, nav, q1, q4, q16, k1, k4, k16, v1, v4, v16, ycg, gate) = _in_proj(
            h, norm_g[l].reshape(1, D_MODEL), w_in[l].astype(BF16), cos_t, sa_t, sb_t, conv_params, batch, seq)
        ya = _na_attention(naq, nak, nav, _na_bias_table(na_rpb[l]), batch, seq)
        outs, lses = [], []
        for d, (q, k, v) in zip(DIL_DILATIONS, ((q1, k1, v1), (q4, k4, v4), (q16, k16, v16))):
            groups = (batch * d, seq // d)
            o, lse = _band_attention(q.reshape(*groups, DIL_W), k.reshape(*groups, DIL_W),
                                     v.reshape(*groups, DIL_W))
            if d == 1:
                outs.append(o.reshape(n, DIL_W))
                lses.append(lse.reshape(n, LANES))
            else:
                outs.append(o.reshape(batch, d, seq // d, DIL_W))
                lses.append(lse.reshape(batch, d, seq // d, LANES))
        h = _out_proj(ya, outs, lses, ycg, gate, h, w_out[l].astype(BF16), fg,
                      final_norm=(l == depth - 1), batch=batch, seq=seq)
    return h.reshape(batch, seq, D_MODEL)
```

```python
import functools

import numpy as np
import jax
import jax.numpy as jnp
from jax import lax
from jax.experimental import pallas as pl
from jax.experimental.pallas import tpu as pltpu

D_MODEL = 1024
HEAD_DIM = 64
GRID_W = 64
NA_HEADS = 4
NA_WIN_ROWS = 8
NA_WIN_COLS = 16
DIL_HEADS = 6
DIL_DILATIONS = (1, 4, 16)
DIL_HALF = 64
CONV_CH = 384
CONV_WIDTH = 31
ROPE_THETA = 10000.0
NORM_EPS = 1e-6
NEG_INF = -1e30
NA_W = NA_HEADS * HEAD_DIM
DIL_W = DIL_HEADS * HEAD_DIM
D_MIX = NA_W + DIL_W + CONV_CH
SCALE = HEAD_DIM ** -0.5
LOG2E = 1.4426950408889634
Q_SCALE = SCALE * LOG2E

LANES = 128
SUBLANES = 8
VMEM_LIMIT = 56 * 1024 * 1024
DIL_SLABS = DIL_W // LANES

_C = np.cumsum([0] + [NA_W] * 4 + [DIL_W] * 4 + [CONV_CH] * 3)
(C_AQ, C_AK, C_AV, C_AG, C_BQ, C_BK, C_BV, C_BG, C_CA, C_CB, C_CG, C_END) = [int(c) for c in _C]

BF16 = jnp.bfloat16
F32 = jnp.float32

TOKEN_TILE = 1024
OUT_TILE = 1024
CONV_HALO = 16
CONV_PAD = CONV_WIDTH // 2
CONV_ROWS = 128


def _params(*sem):
    return pltpu.CompilerParams(dimension_semantics=sem, vmem_limit_bytes=VMEM_LIMIT)


def _silu(v):
    return v * (1.0 / (1.0 + jnp.exp(-v)))


def _lane_is_first_head(shape):
    return lax.broadcasted_iota(jnp.int32, shape, len(shape) - 1) < HEAD_DIM


def _in_proj_kernel(xp_ref, x_ref, xn_ref, g_ref, w_ref, cos_ref, sa_ref, sb_ref,
                    cw_ref, cb_ref, lg_ref, lb_ref, pw_ref, pb_ref,
                    naq_ref, nak_ref, nav_ref,
                    q1_ref, q4_ref, q16_ref, k1_ref, k4_ref, k16_ref, v1_ref, v4_ref, v16_ref,
                    ycg_ref, gate_ref, perm, perm4, ubuf, ybuf, *, seq_blocks):
    tm = x_ref.shape[0]

    def normed(x):
        ms = jnp.mean(x * x, axis=-1, keepdims=True)
        return (x * lax.rsqrt(ms + NORM_EPS) * g_ref[...]).astype(BF16)

    h = normed(x_ref[...])

    def mm(c0, c1):
        return jnp.dot(h, w_ref[:, c0:c1], preferred_element_type=F32)

    cos = cos_ref[...]
    sa = sa_ref[...]
    sb = sb_ref[...]

    def rope(zs):
        up = pltpu.roll(zs, LANES - HEAD_DIM // 2, 1)
        dn = pltpu.roll(zs, HEAD_DIM // 2, 1)
        return zs * cos + up * sa + dn * sb

    def emit_dilated(z, refs, rotary, scale):
        ref1, ref4, ref16 = refs
        for s in range(DIL_SLABS):
            zs = z[:, s * LANES:(s + 1) * LANES]
            if rotary:
                zs = rope(zs)
            if scale != 1.0:
                zs = zs * scale
            perm[s] = zs
            ref1[:, s * LANES:(s + 1) * LANES] = zs.astype(BF16)
        for r1 in range(4):
            for s in range(DIL_SLABS):
                t = perm[s, pl.ds(r1, tm // 4, stride=4), :]
                perm4[s, r1] = t
                ref4[0, r1, :, s * LANES:(s + 1) * LANES] = t.astype(BF16)
        for rho in range(16):
            for s in range(DIL_SLABS):
                ref16[0, rho, :, s * LANES:(s + 1) * LANES] = (
                    perm4[s, rho % 4, pl.ds(rho // 4, tm // 16, stride=4), :].astype(BF16))

    i = pl.program_id(0)
    pos = i % seq_blocks
    h_ext = jnp.concatenate([normed(xp_ref[...]), h, normed(xn_ref[...])], axis=0)
    zc = jnp.dot(h_ext, w_ref[:, C_CA:C_CG], preferred_element_type=F32)
    u = zc[:, 0:CONV_CH] * (1.0 / (1.0 + jnp.exp(-zc[:, CONV_CH:2 * CONV_CH])))
    ubuf[0:CONV_HALO, :] = u[0:CONV_HALO] * jnp.where(pos > 0, 1.0, 0.0).astype(F32)
    ubuf[CONV_HALO:CONV_HALO + tm, :] = u[CONV_HALO:CONV_HALO + tm]
    ubuf[CONV_HALO + tm:, :] = u[CONV_HALO + tm:] * jnp.where(pos < seq_blocks - 1, 1.0, 0.0).astype(F32)

    first_off = CONV_HALO - CONV_PAD
    for c in range(tm // CONV_ROWS):
        r0 = c * CONV_ROWS
        for sl in range(CONV_CH // LANES):
            lanes = slice(sl * LANES, (sl + 1) * LANES)
            y = None
            for s in range(SUBLANES):
                part = None
                for k in range(CONV_WIDTH):
                    off = first_off + k
                    if off % SUBLANES != s:
                        continue
                    win = ubuf[pl.ds(r0 + (off - s), CONV_ROWS + SUBLANES), lanes]
                    term = win * cw_ref[k:k + 1, lanes]
                    part = term if part is None else part + term
                shifted = part[s:s + CONV_ROWS, :]
                y = shifted if y is None else y + shifted
            ybuf[pl.ds(r0, CONV_ROWS), lanes] = y

    za = mm(C_AQ, C_BQ)
    naq_ref[...] = (za[:, 0:NA_W] * Q_SCALE).astype(BF16)
    nak_ref[...] = za[:, NA_W:2 * NA_W].astype(BF16)
    nav_ref[...] = za[:, 2 * NA_W:3 * NA_W].astype(BF16)
    gate_ref[:, 0:NA_W] = _silu(za[:, 3 * NA_W:4 * NA_W]).astype(BF16)

    zqk = mm(C_BQ, C_BV)
    emit_dilated(zqk[:, 0:DIL_W], (q1_ref, q4_ref, q16_ref), True, Q_SCALE)
    emit_dilated(zqk[:, DIL_W:2 * DIL_W], (k1_ref, k4_ref, k16_ref), True, 1.0)

    zvg = mm(C_BV, C_CA)
    emit_dilated(zvg[:, 0:DIL_W], (v1_ref, v4_ref, v16_ref), False, 1.0)
    gate_ref[:, NA_W:NA_W + DIL_W] = _silu(zvg[:, DIL_W:2 * DIL_W]).astype(BF16)

    y = ybuf[...] + cb_ref[...]
    mu = jnp.mean(y, axis=-1, keepdims=True)
    yc = y - mu
    var = jnp.mean(yc * yc, axis=-1, keepdims=True)
    yn = yc * lax.rsqrt(var + NORM_EPS) * lg_ref[...] + lb_ref[...]
    act = _silu(yn).astype(BF16)
    yc = jnp.dot(act, pw_ref[...], preferred_element_type=F32) + pb_ref[...]
    ycg_ref[...] = (yc * _silu(mm(C_CG, C_END))).astype(BF16)


def _in_proj(x2d, g, w_bf16, cos_t, sa_t, sb_t, conv_params, batch, seq):
    n = x2d.shape[0]
    tm = TOKEN_TILE
    seq_blocks = seq // tm
    halo_per_tile = tm // CONV_HALO
    row = lambda i: (i, 0)
    prev = lambda i: (jnp.maximum(i * halo_per_tile - 1, 0), 0)
    nxt = lambda i: (jnp.minimum((i + 1) * halo_per_tile, n // CONV_HALO - 1), 0)
    tab = lambda i: (i % seq_blocks, 0)
    const = lambda i: (0, 0)
    perm_idx = lambda i: (i // seq_blocks, 0, i % seq_blocks, 0)
    vec = pl.BlockSpec((1, CONV_CH), const)

    def dil_shapes():
        return [jax.ShapeDtypeStruct((n, DIL_W), BF16)] + [
            jax.ShapeDtypeStruct((batch, d, seq // d, DIL_W), BF16) for d in DIL_DILATIONS[1:]]

    def dil_specs():
        return [pl.BlockSpec((tm, DIL_W), row)] + [
            pl.BlockSpec((1, d, tm // d, DIL_W), perm_idx) for d in DIL_DILATIONS[1:]]

    out_shapes = ([jax.ShapeDtypeStruct((n, NA_W), BF16)] * 3 + dil_shapes() + dil_shapes() + dil_shapes()
                  + [jax.ShapeDtypeStruct((n, CONV_CH), BF16), jax.ShapeDtypeStruct((n, NA_W + DIL_W), BF16)])
    out_specs = ([pl.BlockSpec((tm, NA_W), row)] * 3 + dil_specs() + dil_specs() + dil_specs()
                 + [pl.BlockSpec((tm, CONV_CH), row), pl.BlockSpec((tm, NA_W + DIL_W), row)])
    return pl.pallas_call(
        functools.partial(_in_proj_kernel, seq_blocks=seq_blocks),
        out_shape=out_shapes,
        grid=(n // tm,),
        in_specs=[
            pl.BlockSpec((CONV_HALO, D_MODEL), prev),
            pl.BlockSpec((tm, D_MODEL), row),
            pl.BlockSpec((CONV_HALO, D_MODEL), nxt),
            pl.BlockSpec((1, D_MODEL), const),
            pl.BlockSpec((D_MODEL, C_END), const),
            pl.BlockSpec((tm, LANES), tab),
            pl.BlockSpec((tm, LANES), tab),
            pl.BlockSpec((tm, LANES), tab),
            pl.BlockSpec((CONV_WIDTH, CONV_CH), const),
            vec, vec, vec,
            pl.BlockSpec((CONV_CH, CONV_CH), const),
            vec,
        ],
        out_specs=out_specs,
        scratch_shapes=[pltpu.VMEM((DIL_SLABS, tm, LANES), F32),
                        pltpu.VMEM((DIL_SLABS, 4, tm // 4, LANES), F32),
                        pltpu.VMEM((tm + 2 * CONV_HALO, CONV_CH), F32),
                        pltpu.VMEM((tm, CONV_CH), F32)],
        compiler_params=_params("parallel"),
        name="in_proj",
    )(x2d, x2d, x2d, g, w_bf16, cos_t, sa_t, sb_t, *conv_params)


NA_ROWS_PER_STEP = 16
NA_KEYS = NA_WIN_ROWS * GRID_W


def _na_kernel(q_ref, k_ref, v_ref, bias_ref, o_ref, sbuf, ebuf, *, rows):
    blk = pl.program_id(1)
    first = _lane_is_first_head((GRID_W, LANES))
    ones = jnp.ones((NA_KEYS, LANES), BF16)
    n_pairs = NA_W // LANES
    starts, maxes = [], []
    for rr in range(NA_ROWS_PER_STEP):
        r = blk * NA_ROWS_PER_STEP + rr
        r0 = jnp.clip(r - NA_WIN_ROWS // 2, 0, rows - NA_WIN_ROWS)
        delta = r0 - r + (NA_WIN_ROWS - 1)
        start = pl.multiple_of(r0 * GRID_W, GRID_W)
        starts.append(start)
        q_row = q_ref[0, rr * GRID_W:(rr + 1) * GRID_W, :]
        k_win = k_ref[0, pl.ds(start, NA_KEYS), :]
        for p in range(n_pairs):
            qp = q_row[:, p * LANES:(p + 1) * LANES]
            zero = jnp.zeros_like(qp)
            lhs = jnp.concatenate([jnp.where(first, qp, zero), jnp.where(first, zero, qp)], axis=0)
            kp = k_win[:, p * LANES:(p + 1) * LANES]
            s = lax.dot_general(lhs, kp, (((1,), (1,)), ((), ())), preferred_element_type=F32)
            s = s + bias_ref[delta, p]
            sbuf[rr * n_pairs + p] = s
            maxes.append(jnp.max(s, axis=-1, keepdims=True))
    for c in range(NA_ROWS_PER_STEP * n_pairs):
        ebuf[c] = jnp.exp2(sbuf[c] - maxes[c]).astype(BF16)
    for rr in range(NA_ROWS_PER_STEP):
        v_win = v_ref[0, pl.ds(starts[rr], NA_KEYS), :]
        for p in range(n_pairs):
            v_aug = jnp.concatenate([v_win[:, p * LANES:(p + 1) * LANES], ones], axis=1)
            o = jnp.dot(ebuf[rr * n_pairs + p], v_aug, preferred_element_type=F32)
            o = o[:, 0:LANES] * (1.0 / o[:, LANES:2 * LANES])
            out = jnp.where(first, o[0:GRID_W], o[GRID_W:2 * GRID_W])
            o_ref[0, rr * GRID_W:(rr + 1) * GRID_W, p * LANES:(p + 1) * LANES] = out.astype(BF16)


def _na_bias_table(rpb):
    cols = np.arange(GRID_W)
    col_start = np.clip(cols - NA_WIN_COLS // 2, 0, GRID_W - NA_WIN_COLS)
    kc = np.arange(GRID_W)
    in_win = (kc[None, :] >= col_start[:, None]) & (kc[None, :] < col_start[:, None] + NA_WIN_COLS)
    col_off = kc[None, :] - cols[:, None] + (NA_WIN_COLS - 1)
    n_off = 2 * NA_WIN_COLS - 1
    onehot = (col_off[None] == np.arange(n_off)[:, None, None]) & in_win[None]
    t = jnp.einsum('hrj,jck->hrck', rpb.astype(F32), jnp.asarray(onehot, F32),
                   precision=lax.Precision.HIGHEST)
    t = jnp.where(jnp.asarray(in_win)[None, None], t * LOG2E, NEG_INF)
    t = jnp.stack([t[:, dlt:dlt + NA_WIN_ROWS] for dlt in range(NA_WIN_ROWS)], axis=0)
    t = t.transpose(0, 1, 3, 2, 4)
    return t.reshape(NA_WIN_ROWS, NA_HEADS // 2, 2 * GRID_W, NA_KEYS)


def _na_attention(q, k, v, bias, batch, seq):
    rows = seq // GRID_W
    tq = NA_ROWS_PER_STEP * GRID_W
    q3 = q.reshape(batch, seq, NA_W)
    k3 = k.reshape(batch, seq, NA_W)
    v3 = v.reshape(batch, seq, NA_W)
    out = pl.pallas_call(
        functools.partial(_na_kernel, rows=rows),
        out_shape=jax.ShapeDtypeStruct((batch, seq, NA_W), BF16),
        grid=(batch, seq // tq),
        in_specs=[
            pl.BlockSpec((1, tq, NA_W), lambda b, i: (b, i, 0)),
            pl.BlockSpec((1, seq, NA_W), lambda b, i: (b, 0, 0)),
            pl.BlockSpec((1, seq, NA_W), lambda b, i: (b, 0, 0)),
            pl.BlockSpec(bias.shape, lambda b, i: (0, 0, 0, 0)),
        ],
        out_specs=pl.BlockSpec((1, tq, NA_W), lambda b, i: (b, i, 0)),
        scratch_shapes=[pltpu.VMEM((NA_ROWS_PER_STEP * NA_W // LANES, 2 * GRID_W, NA_KEYS), F32),
                        pltpu.VMEM((NA_ROWS_PER_STEP * NA_W // LANES, 2 * GRID_W, NA_KEYS), BF16)],
        compiler_params=_params("parallel", "arbitrary"),
        name="na_attention",
    )(q3, k3, v3, bias)
    return out.reshape(batch * seq, NA_W)


BAND_CHUNK = 128
BAND_KEYS = BAND_CHUNK + 2 * DIL_HALF
BAND_TQ_MAX = 1024
STAT_SUM_LANE = 8


def _band_kernel(q_ref, kp_ref, kc_ref, kn_ref, vp_ref, vc_ref, vn_ref, o_ref, stat_ref, *, length):
    blk = pl.program_id(1)
    BAND_TQ = q_ref.shape[1]
    n_chunks = BAND_TQ // BAND_CHUNK
    first = _lane_is_first_head((BAND_CHUNK, LANES))
    lane = lax.broadcasted_iota(jnp.int32, (BAND_CHUNK, LANES), 1)
    qi = lax.broadcasted_iota(jnp.int32, (BAND_CHUNK, BAND_KEYS), 0)
    kj = lax.broadcasted_iota(jnp.int32, (BAND_CHUNK, BAND_KEYS), 1)
    rel = kj - DIL_HALF - qi
    band = (rel >= -DIL_HALF) & (rel <= DIL_HALF)
    ones = jnp.ones((BAND_KEYS, LANES), BF16)
    for c in range(n_chunks):
        a0 = blk * BAND_TQ + c * BAND_CHUNK
        kpos = a0 - DIL_HALF + kj
        ok = band & (kpos >= 0) & (kpos < length)
        mask_bias = jnp.where(ok, 0.0, NEG_INF).astype(F32)
        mask_bias2 = jnp.concatenate([mask_bias, mask_bias], axis=0)
        lo = c * BAND_CHUNK - DIL_HALF
        hi = lo + BAND_KEYS

        def window(prev_ref, cur_ref, next_ref):
            pieces = []
            if lo < 0:
                pieces.append(prev_ref[0, :, :])
            pieces.append(cur_ref[0, max(lo, 0):min(hi, BAND_TQ), :])
            if hi > BAND_TQ:
                pieces.append(next_ref[0, :, :])
            return pieces[0] if len(pieces) == 1 else jnp.concatenate(pieces, axis=0)

        k_win = window(kp_ref, kc_ref, kn_ref)
        v_win = window(vp_ref, vc_ref, vn_ref)
        q_chunk = q_ref[0, c * BAND_CHUNK:(c + 1) * BAND_CHUNK, :]
        rows = slice(c * BAND_CHUNK, (c + 1) * BAND_CHUNK)
        stat_tile = jnp.zeros((BAND_CHUNK, LANES), F32)
        for p in range(DIL_SLABS):
            qp = q_chunk[:, p * LANES:(p + 1) * LANES]
            zero = jnp.zeros_like(qp)
            lhs = jnp.concatenate([jnp.where(first, qp, zero), jnp.where(first, zero, qp)], axis=0)
            kp = k_win[:, p * LANES:(p + 1) * LANES]
            vp = v_win[:, p * LANES:(p + 1) * LANES]
            s = lax.dot_general(lhs, kp, (((1,), (1,)), ((), ())), preferred_element_type=F32)
            s = s + mask_bias2
            m = jnp.max(s, axis=-1, keepdims=True)
            e = jnp.exp2(s - m).astype(BF16)
            o = jnp.dot(e, jnp.concatenate([vp, ones], axis=1), preferred_element_type=F32)
            l = o[:, LANES:2 * LANES]
            o_ref[0, rows, p * LANES:(p + 1) * LANES] = jnp.where(
                first, o[0:BAND_CHUNK, 0:LANES], o[BAND_CHUNK:, 0:LANES]).astype(BF16)
            for hh in range(2):
                half = slice(hh * BAND_CHUNK, (hh + 1) * BAND_CHUNK)
                stat_tile = jnp.where(lane == 2 * p + hh, m[half], stat_tile)
                stat_tile = jnp.where(lane == STAT_SUM_LANE + 2 * p + hh, l[half], stat_tile)
        stat_ref[0, rows, :] = stat_tile


def _band_attention(q, k, v):
    groups, length, _ = q.shape
    tq = min(BAND_TQ_MAX, length)
    halo_per_blk = tq // DIL_HALF
    n_halo = length // DIL_HALF
    cur = lambda g, i: (g, i, 0)
    prev = lambda g, i: (g, jnp.maximum(i * halo_per_blk - 1, 0), 0)
    nxt = lambda g, i: (g, jnp.minimum((i + 1) * halo_per_blk, n_halo - 1), 0)
    blk = pl.BlockSpec((1, tq, DIL_W), cur)
    halo_p = pl.BlockSpec((1, DIL_HALF, DIL_W), prev)
    halo_n = pl.BlockSpec((1, DIL_HALF, DIL_W), nxt)
    return pl.pallas_call(
        functools.partial(_band_kernel, length=length),
        out_shape=[jax.ShapeDtypeStruct((groups, length, DIL_W), BF16),
                   jax.ShapeDtypeStruct((groups, length, LANES), F32)],
        grid=(groups, length // tq),
        in_specs=[blk, halo_p, blk, halo_n, halo_p, blk, halo_n],
        out_specs=[pl.BlockSpec((1, tq, DIL_W), cur), pl.BlockSpec((1, tq, LANES), cur)],
        compiler_params=_params("parallel", "arbitrary"),
        name="band_attention",
    )(q, k, k, k, v, v, v)


def _out_proj_kernel(ya_ref, o1_ref, o4_ref, o16_ref, l1_ref, l4_ref, l16_ref, ycg_ref, gate_ref,
                     x_ref, w_ref, fg_ref, out_ref, onat, lnat, *, final_norm):
    tm = x_ref.shape[0]
    first = _lane_is_first_head((tm, LANES))

    def natural_stats(l_ref, d, slot):
        if d == 1:
            return l_ref[...]
        for rho in range(d):
            lnat[slot, pl.ds(rho, tm // d, stride=d), :] = l_ref[0, rho]
        return lnat[slot]

    stats = (natural_stats(l1_ref, 1, 0), natural_stats(l4_ref, 4, 0), natural_stats(l16_ref, 16, 1))
    m = jnp.maximum(jnp.maximum(stats[0], stats[1]), stats[2])
    scales = [jnp.exp2(st - m) for st in stats]
    denom = None
    for st, sc in zip(stats, scales):
        row_sum = pltpu.roll(st, LANES - STAT_SUM_LANE, 1)
        denom = sc * row_sum if denom is None else denom + sc * row_sum
    inv = 1.0 / denom
    weights = [sc * inv for sc in scales]

    for slot, (d, o_ref) in enumerate(((4, o4_ref), (16, o16_ref))):
        for rho in range(d):
            for s in range(DIL_SLABS):
                onat[slot, s, pl.ds(rho, tm // d, stride=d), :] = (
                    o_ref[0, rho, :, s * LANES:(s + 1) * LANES].astype(F32))

    gate = gate_ref[...].astype(F32)
    pieces = [ya_ref[...].astype(F32) * gate[:, 0:NA_W]]
    for s in range(DIL_SLABS):
        outs = (o1_ref[:, s * LANES:(s + 1) * LANES].astype(F32), onat[0, s], onat[1, s])
        yb = None
        for w, o in zip(weights, outs):
            wb = jnp.where(first, w[:, 2 * s:2 * s + 1], w[:, 2 * s + 1:2 * s + 2])
            yb = wb * o if yb is None else yb + wb * o
        pieces.append(yb * gate[:, NA_W + s * LANES:NA_W + (s + 1) * LANES])
    mix = jnp.concatenate([piece.astype(BF16) for piece in pieces] + [ycg_ref[...]], axis=1)
    y = x_ref[...] + jnp.dot(mix, w_ref[...], preferred_element_type=F32)
    if final_norm:
        ms = jnp.mean(y * y, axis=-1, keepdims=True)
        y = y * lax.rsqrt(ms + NORM_EPS) * fg_ref[...]
    out_ref[...] = y


def _out_proj(ya, outs, lses, ycg, gate, x2d, w_bf16, final_g, final_norm, batch, seq):
    n = x2d.shape[0]
    tm = OUT_TILE
    seq_blocks = seq // tm
    row = lambda i: (i, 0)
    const = lambda i: (0, 0)
    perm_idx = lambda i: (i // seq_blocks, 0, i % seq_blocks, 0)
    o_specs = [pl.BlockSpec((tm, DIL_W), row)] + [
        pl.BlockSpec((1, d, tm // d, DIL_W), perm_idx) for d in DIL_DILATIONS[1:]]
    l_specs = [pl.BlockSpec((tm, LANES), row)] + [
        pl.BlockSpec((1, d, tm // d, LANES), perm_idx) for d in DIL_DILATIONS[1:]]
    return pl.pallas_call(
        functools.partial(_out_proj_kernel, final_norm=final_norm),
        out_shape=jax.ShapeDtypeStruct((n, D_MODEL), F32),
        grid=(n // tm,),
        in_specs=[pl.BlockSpec((tm, NA_W), row)] + o_specs + l_specs + [
            pl.BlockSpec((tm, CONV_CH), row),
            pl.BlockSpec((tm, NA_W + DIL_W), row),
            pl.BlockSpec((tm, D_MODEL), row),
            pl.BlockSpec((D_MIX, D_MODEL), const),
            pl.BlockSpec((1, D_MODEL), const),
        ],
        out_specs=pl.BlockSpec((tm, D_MODEL), row),
        scratch_shapes=[pltpu.VMEM((2, DIL_SLABS, tm, LANES), F32),
                        pltpu.VMEM((2, tm, LANES), F32)],
        compiler_params=_params("parallel"),
        name="out_proj",
    )(ya, *outs, *lses, ycg, gate, x2d, w_bf16, final_g)


def _rotary_tables(seq):
    half = HEAD_DIM // 2
    inv_freq = jnp.power(ROPE_THETA, -jnp.arange(half, dtype=F32) * 2.0 / HEAD_DIM)
    ang = jnp.arange(seq).astype(F32)[:, None] * inv_freq[None, :]
    cos = jnp.cos(ang)
    sin = jnp.sin(ang)
    zero = jnp.zeros_like(sin)
    reps = LANES // HEAD_DIM
    cos_t = jnp.tile(jnp.concatenate([cos, cos], axis=1), (1, reps))
    sa_t = jnp.tile(jnp.concatenate([-sin, zero], axis=1), (1, reps))
    sb_t = jnp.tile(jnp.concatenate([zero, sin], axis=1), (1, reps))
    return cos_t, sa_t, sb_t


def kernel(x, norm_g, w_in, na_rpb, conv_w, conv_b, conv_ln_g, conv_ln_b, pw_w, pw_b, w_out, final_g):
    batch, seq, _ = x.shape
    depth = w_in.shape[0]
    n = batch * seq
    cos_t, sa_t, sb_t = _rotary_tables(seq)
    h = x.reshape(n, D_MODEL)
    fg = final_g.reshape(1, D_MODEL)
    for l in range(depth):
        conv_params = (conv_w[l], conv_b[l].reshape(1, CONV_CH), conv_ln_g[l].reshape(1, CONV_CH),
                       conv_ln_b[l].reshape(1, CONV_CH), pw_w[l].astype(BF16), pw_b[l].reshape(1, CONV_CH))
        (naq, nak, nav, q1, q4, q16, k1, k4, k16, v1, v4, v16, ycg, gate) = _in_proj(
            h, norm_g[l].reshape(1, D_MODEL), w_in[l].astype(BF16), cos_t, sa_t, sb_t, conv_params, batch, seq)
        ya = _na_attention(naq, nak, nav, _na_bias_table(na_rpb[l]), batch, seq)
        outs, lses = [], []
        for d, (q, k, v) in zip(DIL_DILATIONS, ((q1, k1, v1), (q4, k4, v4), (q16, k16, v16))):
            groups = (batch * d, seq // d)
            o, lse = _band_attention(q.reshape(*groups, DIL_W), k.reshape(*groups, DIL_W),
                                     v.reshape(*groups, DIL_W))
            if d == 1:
                outs.append(o.reshape(n, DIL_W))
                lses.append(lse.reshape(n, LANES))
            else:
                outs.append(o.reshape(batch, d, seq // d, DIL_W))
                lses.append(lse.reshape(batch, d, seq // d, LANES))
        h = _out_proj(ya, outs, lses, ycg, gate, h, w_out[l].astype(BF16), fg,
                      final_norm=(l == depth - 1), batch=batch, seq=seq)
    return h.reshape(batch, seq, D_MODEL)
```

```python
import functools

import numpy as np
import jax
import jax.numpy as jnp
from jax import lax
from jax.experimental import pallas as pl
from jax.experimental.pallas import tpu as pltpu

D_MODEL = 1024
HEAD_DIM = 64
GRID_W = 64
NA_HEADS = 4
NA_WIN_ROWS = 8
NA_WIN_COLS = 16
DIL_HEADS = 6
DIL_DILATIONS = (1, 4, 16)
DIL_HALF = 64
CONV_CH = 384
CONV_WIDTH = 31
ROPE_THETA = 10000.0
NORM_EPS = 1e-6
NEG_INF = -1e30
NA_W = NA_HEADS * HEAD_DIM
DIL_W = DIL_HEADS * HEAD_DIM
D_MIX = NA_W + DIL_W + CONV_CH
SCALE = HEAD_DIM ** -0.5
LOG2E = 1.4426950408889634
Q_SCALE = SCALE * LOG2E

LANES = 128
SUBLANES = 8
VMEM_LIMIT = 56 * 1024 * 1024
DIL_SLABS = DIL_W // LANES

_C = np.cumsum([0] + [NA_W] * 4 + [DIL_W] * 4 + [CONV_CH] * 3)
(C_AQ, C_AK, C_AV, C_AG, C_BQ, C_BK, C_BV, C_BG, C_CA, C_CB, C_CG, C_END) = [int(c) for c in _C]

BF16 = jnp.bfloat16
F32 = jnp.float32

TOKEN_TILE = 1024
OUT_TILE = 1024
CONV_HALO = 16
CONV_PAD = CONV_WIDTH // 2
CONV_ROWS = 128


def _params(*sem):
    return pltpu.CompilerParams(dimension_semantics=sem, vmem_limit_bytes=VMEM_LIMIT)


def _silu(v):
    return v * (1.0 / (1.0 + jnp.exp(-v)))


def _lane_is_first_head(shape):
    return lax.broadcasted_iota(jnp.int32, shape, len(shape) - 1) < HEAD_DIM


def _in_proj_kernel(x_ref, g_ref, w_ref, cos_ref, sa_ref, sb_ref,
                    naq_ref, nak_ref, nav_ref,
                    q1_ref, q4_ref, q16_ref, k1_ref, k4_ref, k16_ref, v1_ref, v4_ref, v16_ref,
                    u_ref, gate_ref, perm, perm4):
    tm = x_ref.shape[0]
    x = x_ref[...]
    ms = jnp.mean(x * x, axis=-1, keepdims=True)
    h = (x * lax.rsqrt(ms + NORM_EPS) * g_ref[...]).astype(BF16)

    def mm(c0, c1):
        return jnp.dot(h, w_ref[:, c0:c1], preferred_element_type=F32)

    cos = cos_ref[...]
    sa = sa_ref[...]
    sb = sb_ref[...]

    def rope(zs):
        up = pltpu.roll(zs, LANES - HEAD_DIM // 2, 1)
        dn = pltpu.roll(zs, HEAD_DIM // 2, 1)
        return zs * cos + up * sa + dn * sb

    def emit_dilated(z, refs, rotary, scale):
        ref1, ref4, ref16 = refs
        for s in range(DIL_SLABS):
            zs = z[:, s * LANES:(s + 1) * LANES]
            if rotary:
                zs = rope(zs)
            if scale != 1.0:
                zs = zs * scale
            perm[s] = zs
            ref1[:, s * LANES:(s + 1) * LANES] = zs.astype(BF16)
        for r1 in range(4):
            for s in range(DIL_SLABS):
                t = perm[s, pl.ds(r1, tm // 4, stride=4), :]
                perm4[s, r1] = t
                ref4[0, r1, :, s * LANES:(s + 1) * LANES] = t.astype(BF16)
        for rho in range(16):
            for s in range(DIL_SLABS):
                ref16[0, rho, :, s * LANES:(s + 1) * LANES] = (
                    perm4[s, rho % 4, pl.ds(rho // 4, tm // 16, stride=4), :].astype(BF16))

    za = mm(C_AQ, C_BQ)
    naq_ref[...] = (za[:, 0:NA_W] * Q_SCALE).astype(BF16)
    nak_ref[...] = za[:, NA_W:2 * NA_W].astype(BF16)
    nav_ref[...] = za[:, 2 * NA_W:3 * NA_W].astype(BF16)
    gate_ref[:, 0:NA_W] = _silu(za[:, 3 * NA_W:4 * NA_W]).astype(BF16)

    zqk = mm(C_BQ, C_BV)
    emit_dilated(zqk[:, 0:DIL_W], (q1_ref, q4_ref, q16_ref), True, Q_SCALE)
    emit_dilated(zqk[:, DIL_W:2 * DIL_W], (k1_ref, k4_ref, k16_ref), True, 1.0)

    zvg = mm(C_BV, C_CA)
    emit_dilated(zvg[:, 0:DIL_W], (v1_ref, v4_ref, v16_ref), False, 1.0)
    gate_ref[:, NA_W:NA_W + DIL_W] = _silu(zvg[:, DIL_W:2 * DIL_W]).astype(BF16)

    zc = mm(C_CA, C_CG)
    u_ref[...] = (zc[:, 0:CONV_CH] * (1.0 / (1.0 + jnp.exp(-zc[:, CONV_CH:2 * CONV_CH])))).astype(BF16)
    gate_ref[:, NA_W + DIL_W:D_MIX] = _silu(mm(C_CG, C_END)).astype(BF16)


def _in_proj(x2d, g, w_bf16, cos_t, sa_t, sb_t, batch, seq):
    n = x2d.shape[0]
    tm = TOKEN_TILE
    seq_blocks = seq // tm
    row = lambda i: (i, 0)
    tab = lambda i: (i % seq_blocks, 0)
    const = lambda i: (0, 0)
    perm_idx = lambda i: (i // seq_blocks, 0, i % seq_blocks, 0)

    def dil_shapes():
        return [jax.ShapeDtypeStruct((n, DIL_W), BF16)] + [
            jax.ShapeDtypeStruct((batch, d, seq // d, DIL_W), BF16) for d in DIL_DILATIONS[1:]]

    def dil_specs():
        return [pl.BlockSpec((tm, DIL_W), row)] + [
            pl.BlockSpec((1, d, tm // d, DIL_W), perm_idx) for d in DIL_DILATIONS[1:]]

    out_shapes = ([jax.ShapeDtypeStruct((n, NA_W), BF16)] * 3 + dil_shapes() + dil_shapes() + dil_shapes()
                  + [jax.ShapeDtypeStruct((n, CONV_CH), BF16), jax.ShapeDtypeStruct((n, D_MIX), BF16)])
    out_specs = ([pl.BlockSpec((tm, NA_W), row)] * 3 + dil_specs() + dil_specs() + dil_specs()
                 + [pl.BlockSpec((tm, CONV_CH), row), pl.BlockSpec((tm, D_MIX), row)])
    return pl.pallas_call(
        _in_proj_kernel,
        out_shape=out_shapes,
        grid=(n // tm,),
        in_specs=[
            pl.BlockSpec((tm, D_MODEL), row),
            pl.BlockSpec((1, D_MODEL), const),
            pl.BlockSpec((D_MODEL, C_END), const),
            pl.BlockSpec((tm, LANES), tab),
            pl.BlockSpec((tm, LANES), tab),
            pl.BlockSpec((tm, LANES), tab),
        ],
        out_specs=out_specs,
        scratch_shapes=[pltpu.VMEM((DIL_SLABS, tm, LANES), F32),
                        pltpu.VMEM((DIL_SLABS, 4, tm // 4, LANES), F32)],
        compiler_params=_params("parallel"),
        name="in_proj",
    )(x2d, g, w_bf16, cos_t, sa_t, sb_t)


NA_ROWS_PER_STEP = 16
NA_KEYS = NA_WIN_ROWS * GRID_W


def _na_kernel(q_ref, k_ref, v_ref, bias_ref, o_ref, sbuf, ebuf, *, rows):
    blk = pl.program_id(1)
    first = _lane_is_first_head((GRID_W, LANES))
    ones = jnp.ones((NA_KEYS, LANES), BF16)
    n_pairs = NA_W // LANES
    starts, maxes = [], []
    for rr in range(NA_ROWS_PER_STEP):
        r = blk * NA_ROWS_PER_STEP + rr
        r0 = jnp.clip(r - NA_WIN_ROWS // 2, 0, rows - NA_WIN_ROWS)
        delta = r0 - r + (NA_WIN_ROWS - 1)
        start = pl.multiple_of(r0 * GRID_W, GRID_W)
        starts.append(start)
        q_row = q_ref[0, rr * GRID_W:(rr + 1) * GRID_W, :]
        k_win = k_ref[0, pl.ds(start, NA_KEYS), :]
        for p in range(n_pairs):
            qp = q_row[:, p * LANES:(p + 1) * LANES]
            zero = jnp.zeros_like(qp)
            lhs = jnp.concatenate([jnp.where(first, qp, zero), jnp.where(first, zero, qp)], axis=0)
            kp = k_win[:, p * LANES:(p + 1) * LANES]
            s = lax.dot_general(lhs, kp, (((1,), (1,)), ((), ())), preferred_element_type=F32)
            s = s + bias_ref[delta, p]
            sbuf[rr * n_pairs + p] = s
            maxes.append(jnp.max(s, axis=-1, keepdims=True))
    for c in range(NA_ROWS_PER_STEP * n_pairs):
        ebuf[c] = jnp.exp2(sbuf[c] - maxes[c]).astype(BF16)
    for rr in range(NA_ROWS_PER_STEP):
        v_win = v_ref[0, pl.ds(starts[rr], NA_KEYS), :]
        for p in range(n_pairs):
            v_aug = jnp.concatenate([v_win[:, p * LANES:(p + 1) * LANES], ones], axis=1)
            o = jnp.dot(ebuf[rr * n_pairs + p], v_aug, preferred_element_type=F32)
            o = o[:, 0:LANES] * (1.0 / o[:, LANES:2 * LANES])
            out = jnp.where(first, o[0:GRID_W], o[GRID_W:2 * GRID_W])
            o_ref[0, rr * GRID_W:(rr + 1) * GRID_W, p * LANES:(p + 1) * LANES] = out.astype(BF16)


def _na_bias_table(rpb):
    cols = np.arange(GRID_W)
    col_start = np.clip(cols - NA_WIN_COLS // 2, 0, GRID_W - NA_WIN_COLS)
    kc = np.arange(GRID_W)
    in_win = (kc[None, :] >= col_start[:, None]) & (kc[None, :] < col_start[:, None] + NA_WIN_COLS)
    col_off = kc[None, :] - cols[:, None] + (NA_WIN_COLS - 1)
    n_off = 2 * NA_WIN_COLS - 1
    onehot = (col_off[None] == np.arange(n_off)[:, None, None]) & in_win[None]
    t = jnp.einsum('hrj,jck->hrck', rpb.astype(F32), jnp.asarray(onehot, F32),
                   precision=lax.Precision.HIGHEST)
    t = jnp.where(jnp.asarray(in_win)[None, None], t * LOG2E, NEG_INF)
    t = jnp.stack([t[:, dlt:dlt + NA_WIN_ROWS] for dlt in range(NA_WIN_ROWS)], axis=0)
    t = t.transpose(0, 1, 3, 2, 4)
    return t.reshape(NA_WIN_ROWS, NA_HEADS // 2, 2 * GRID_W, NA_KEYS)


def _na_attention(q, k, v, bias, batch, seq):
    rows = seq // GRID_W
    tq = NA_ROWS_PER_STEP * GRID_W
    q3 = q.reshape(batch, seq, NA_W)
    k3 = k.reshape(batch, seq, NA_W)
    v3 = v.reshape(batch, seq, NA_W)
    out = pl.pallas_call(
        functools.partial(_na_kernel, rows=rows),
        out_shape=jax.ShapeDtypeStruct((batch, seq, NA_W), BF16),
        grid=(batch, seq // tq),
        in_specs=[
            pl.BlockSpec((1, tq, NA_W), lambda b, i: (b, i, 0)),
            pl.BlockSpec((1, seq, NA_W), lambda b, i: (b, 0, 0)),
            pl.BlockSpec((1, seq, NA_W), lambda b, i: (b, 0, 0)),
            pl.BlockSpec(bias.shape, lambda b, i: (0, 0, 0, 0)),
        ],
        out_specs=pl.BlockSpec((1, tq, NA_W), lambda b, i: (b, i, 0)),
        scratch_shapes=[pltpu.VMEM((NA_ROWS_PER_STEP * NA_W // LANES, 2 * GRID_W, NA_KEYS), F32),
                        pltpu.VMEM((NA_ROWS_PER_STEP * NA_W // LANES, 2 * GRID_W, NA_KEYS), BF16)],
        compiler_params=_params("parallel", "arbitrary"),
        name="na_attention",
    )(q3, k3, v3, bias)
    return out.reshape(batch * seq, NA_W)


BAND_CHUNK = 128
BAND_KEYS = BAND_CHUNK + 2 * DIL_HALF
BAND_TQ_MAX = 1024
STAT_SUM_LANE = 8


def _band_kernel(q_ref, kp_ref, kc_ref, kn_ref, vp_ref, vc_ref, vn_ref, o_ref, stat_ref, *, length):
    blk = pl.program_id(1)
    BAND_TQ = q_ref.shape[1]
    n_chunks = BAND_TQ // BAND_CHUNK
    first = _lane_is_first_head((BAND_CHUNK, LANES))
    lane = lax.broadcasted_iota(jnp.int32, (BAND_CHUNK, LANES), 1)
    qi = lax.broadcasted_iota(jnp.int32, (BAND_CHUNK, BAND_KEYS), 0)
    kj = lax.broadcasted_iota(jnp.int32, (BAND_CHUNK, BAND_KEYS), 1)
    rel = kj - DIL_HALF - qi
    band = (rel >= -DIL_HALF) & (rel <= DIL_HALF)
    ones = jnp.ones((BAND_KEYS, LANES), BF16)
    for c in range(n_chunks):
        a0 = blk * BAND_TQ + c * BAND_CHUNK
        kpos = a0 - DIL_HALF + kj
        ok = band & (kpos >= 0) & (kpos < length)
        mask_bias = jnp.where(ok, 0.0, NEG_INF).astype(F32)
        mask_bias2 = jnp.concatenate([mask_bias, mask_bias], axis=0)
        lo = c * BAND_CHUNK - DIL_HALF
        hi = lo + BAND_KEYS

        def window(prev_ref, cur_ref, next_ref):
            pieces = []
            if lo < 0:
                pieces.append(prev_ref[0, :, :])
            pieces.append(cur_ref[0, max(lo, 0):min(hi, BAND_TQ), :])
            if hi > BAND_TQ:
                pieces.append(next_ref[0, :, :])
            return pieces[0] if len(pieces) == 1 else jnp.concatenate(pieces, axis=0)

        k_win = window(kp_ref, kc_ref, kn_ref)
        v_win = window(vp_ref, vc_ref, vn_ref)
        q_chunk = q_ref[0, c * BAND_CHUNK:(c + 1) * BAND_CHUNK, :]
        rows = slice(c * BAND_CHUNK, (c + 1) * BAND_CHUNK)
        stat_tile = jnp.zeros((BAND_CHUNK, LANES), F32)
        for p in range(DIL_SLABS):
            qp = q_chunk[:, p * LANES:(p + 1) * LANES]
            zero = jnp.zeros_like(qp)
            lhs = jnp.concatenate([jnp.where(first, qp, zero), jnp.where(first, zero, qp)], axis=0)
            kp = k_win[:, p * LANES:(p + 1) * LANES]
            vp = v_win[:, p * LANES:(p + 1) * LANES]
            s = lax.dot_general(lhs, kp, (((1,), (1,)), ((), ())), preferred_element_type=F32)
            s = s + mask_bias2
            m = jnp.max(s, axis=-1, keepdims=True)
            e = jnp.exp2(s - m).astype(BF16)
            o = jnp.dot(e, jnp.concatenate([vp, ones], axis=1), preferred_element_type=F32)
            l = o[:, LANES:2 * LANES]
            o_ref[0, rows, p * LANES:(p + 1) * LANES] = jnp.where(
                first, o[0:BAND_CHUNK, 0:LANES], o[BAND_CHUNK:, 0:LANES]).astype(BF16)
            for hh in range(2):
                half = slice(hh * BAND_CHUNK, (hh + 1) * BAND_CHUNK)
                stat_tile = jnp.where(lane == 2 * p + hh, m[half], stat_tile)
                stat_tile = jnp.where(lane == STAT_SUM_LANE + 2 * p + hh, l[half], stat_tile)
        stat_ref[0, rows, :] = stat_tile


def _band_attention(q, k, v):
    groups, length, _ = q.shape
    tq = min(BAND_TQ_MAX, length)
    halo_per_blk = tq // DIL_HALF
    n_halo = length // DIL_HALF
    cur = lambda g, i: (g, i, 0)
    prev = lambda g, i: (g, jnp.maximum(i * halo_per_blk - 1, 0), 0)
    nxt = lambda g, i: (g, jnp.minimum((i + 1) * halo_per_blk, n_halo - 1), 0)
    blk = pl.BlockSpec((1, tq, DIL_W), cur)
    halo_p = pl.BlockSpec((1, DIL_HALF, DIL_W), prev)
    halo_n = pl.BlockSpec((1, DIL_HALF, DIL_W), nxt)
    return pl.pallas_call(
        functools.partial(_band_kernel, length=length),
        out_shape=[jax.ShapeDtypeStruct((groups, length, DIL_W), BF16),
                   jax.ShapeDtypeStruct((groups, length, LANES), F32)],
        grid=(groups, length // tq),
        in_specs=[blk, halo_p, blk, halo_n, halo_p, blk, halo_n],
        out_specs=[pl.BlockSpec((1, tq, DIL_W), cur), pl.BlockSpec((1, tq, LANES), cur)],
        compiler_params=_params("parallel", "arbitrary"),
        name="band_attention",
    )(q, k, k, k, v, v, v)


def _conv_module(up_ref, uc_ref, un_ref, cw_ref, cb_ref, lg_ref, lb_ref, pw_ref, pb_ref, ubuf, ybuf, pos, n_pos):
    tm = uc_ref.shape[0]
    ubuf[0:CONV_HALO, :] = up_ref[...].astype(F32) * jnp.where(pos > 0, 1.0, 0.0).astype(F32)
    ubuf[CONV_HALO:CONV_HALO + tm, :] = uc_ref[...].astype(F32)
    ubuf[CONV_HALO + tm:, :] = un_ref[...].astype(F32) * jnp.where(pos < n_pos - 1, 1.0, 0.0).astype(F32)

    first_off = CONV_HALO - CONV_PAD
    for c in range(tm // CONV_ROWS):
        r0 = c * CONV_ROWS
        for sl in range(CONV_CH // LANES):
            lanes = slice(sl * LANES, (sl + 1) * LANES)
            y = None
            for s in range(SUBLANES):
                part = None
                for k in range(CONV_WIDTH):
                    off = first_off + k
                    if off % SUBLANES != s:
                        continue
                    win = ubuf[pl.ds(r0 + (off - s), CONV_ROWS + SUBLANES), lanes]
                    term = win * cw_ref[k:k + 1, lanes]
                    part = term if part is None else part + term
                shifted = part[s:s + CONV_ROWS, :]
                y = shifted if y is None else y + shifted
            ybuf[pl.ds(r0, CONV_ROWS), lanes] = y

    y = ybuf[...] + cb_ref[...]
    mu = jnp.mean(y, axis=-1, keepdims=True)
    yc = y - mu
    var = jnp.mean(yc * yc, axis=-1, keepdims=True)
    yn = yc * lax.rsqrt(var + NORM_EPS) * lg_ref[...] + lb_ref[...]
    act = _silu(yn).astype(BF16)
    return jnp.dot(act, pw_ref[...], preferred_element_type=F32) + pb_ref[...]


def _out_proj_kernel(ya_ref, o1_ref, o4_ref, o16_ref, l1_ref, l4_ref, l16_ref, up_ref, uc_ref, un_ref, gate_ref,
                     x_ref, w_ref, fg_ref, cw_ref, cb_ref, lg_ref, lb_ref, pw_ref, pb_ref,
                     out_ref, onat, lnat, ubuf, ybuf, *, final_norm, seq_blocks):
    tm = x_ref.shape[0]
    first = _lane_is_first_head((tm, LANES))
    yc = _conv_module(up_ref, uc_ref, un_ref, cw_ref, cb_ref, lg_ref, lb_ref, pw_ref, pb_ref, ubuf, ybuf,
                      pl.program_id(0) % seq_blocks, seq_blocks)

    def natural_stats(l_ref, d, slot):
        if d == 1:
            return l_ref[...]
        for rho in range(d):
            lnat[slot, pl.ds(rho, tm // d, stride=d), :] = l_ref[0, rho]
        return lnat[slot]

    stats = (natural_stats(l1_ref, 1, 0), natural_stats(l4_ref, 4, 0), natural_stats(l16_ref, 16, 1))
    m = jnp.maximum(jnp.maximum(stats[0], stats[1]), stats[2])
    scales = [jnp.exp2(st - m) for st in stats]
    denom = None
    for st, sc in zip(stats, scales):
        row_sum = pltpu.roll(st, LANES - STAT_SUM_LANE, 1)
        denom = sc * row_sum if denom is None else denom + sc * row_sum
    inv = 1.0 / denom
    weights = [sc * inv for sc in scales]

    for slot, (d, o_ref) in enumerate(((4, o4_ref), (16, o16_ref))):
        for rho in range(d):
            for s in range(DIL_SLABS):
                onat[slot, s, pl.ds(rho, tm // d, stride=d), :] = (
                    o_ref[0, rho, :, s * LANES:(s + 1) * LANES].astype(F32))

    gate = gate_ref[...].astype(F32)
    pieces = [ya_ref[...].astype(F32) * gate[:, 0:NA_W]]
    for s in range(DIL_SLABS):
        outs = (o1_ref[:, s * LANES:(s + 1) * LANES].astype(F32), onat[0, s], onat[1, s])
        yb = None
        for w, o in zip(weights, outs):
            wb = jnp.where(first, w[:, 2 * s:2 * s + 1], w[:, 2 * s + 1:2 * s + 2])
            yb = wb * o if yb is None else yb + wb * o
        pieces.append(yb * gate[:, NA_W + s * LANES:NA_W + (s + 1) * LANES])
    pieces.append(yc * gate[:, NA_W + DIL_W:D_MIX])
    mix = jnp.concatenate([piece.astype(BF16) for piece in pieces], axis=1)
    y = x_ref[...] + jnp.dot(mix, w_ref[...], preferred_element_type=F32)
    if final_norm:
        ms = jnp.mean(y * y, axis=-1, keepdims=True)
        y = y * lax.rsqrt(ms + NORM_EPS) * fg_ref[...]
    out_ref[...] = y


def _out_proj(ya, outs, lses, u, gate, x2d, w_bf16, final_g, conv_params, final_norm, batch, seq):
    n = x2d.shape[0]
    tm = OUT_TILE
    seq_blocks = seq // tm
    halo_per_tile = tm // CONV_HALO
    row = lambda i: (i, 0)
    prev = lambda i: (jnp.maximum(i * halo_per_tile - 1, 0), 0)
    nxt = lambda i: (jnp.minimum((i + 1) * halo_per_tile, n // CONV_HALO - 1), 0)
    const = lambda i: (0, 0)
    vec = pl.BlockSpec((1, CONV_CH), const)
    perm_idx = lambda i: (i // seq_blocks, 0, i % seq_blocks, 0)
    o_specs = [pl.BlockSpec((tm, DIL_W), row)] + [
        pl.BlockSpec((1, d, tm // d, DIL_W), perm_idx) for d in DIL_DILATIONS[1:]]
    l_specs = [pl.BlockSpec((tm, LANES), row)] + [
        pl.BlockSpec((1, d, tm // d, LANES), perm_idx) for d in DIL_DILATIONS[1:]]
    return pl.pallas_call(
        functools.partial(_out_proj_kernel, final_norm=final_norm, seq_blocks=seq_blocks),
        out_shape=jax.ShapeDtypeStruct((n, D_MODEL), F32),
        grid=(n // tm,),
        in_specs=[pl.BlockSpec((tm, NA_W), row)] + o_specs + l_specs + [
            pl.BlockSpec((CONV_HALO, CONV_CH), prev),
            pl.BlockSpec((tm, CONV_CH), row),
            pl.BlockSpec((CONV_HALO, CONV_CH), nxt),
            pl.BlockSpec((tm, D_MIX), row),
            pl.BlockSpec((tm, D_MODEL), row),
            pl.BlockSpec((D_MIX, D_MODEL), const),
            pl.BlockSpec((1, D_MODEL), const),
            pl.BlockSpec((CONV_WIDTH, CONV_CH), const),
            vec, vec, vec,
            pl.BlockSpec((CONV_CH, CONV_CH), const),
            vec,
        ],
        out_specs=pl.BlockSpec((tm, D_MODEL), row),
        scratch_shapes=[pltpu.VMEM((2, DIL_SLABS, tm, LANES), F32),
                        pltpu.VMEM((2, tm, LANES), F32),
                        pltpu.VMEM((tm + 2 * CONV_HALO, CONV_CH), F32),
                        pltpu.VMEM((tm, CONV_CH), F32)],
        compiler_params=_params("parallel"),
        name="out_proj",
    )(ya, *outs, *lses, u, u, u, gate, x2d, w_bf16, final_g, *conv_params)


def _rotary_tables(seq):
    half = HEAD_DIM // 2
    inv_freq = jnp.power(ROPE_THETA, -jnp.arange(half, dtype=F32) * 2.0 / HEAD_DIM)
    ang = jnp.arange(seq).astype(F32)[:, None] * inv_freq[None, :]
    cos = jnp.cos(ang)
    sin = jnp.sin(ang)
    zero = jnp.zeros_like(sin)
    reps = LANES // HEAD_DIM
    cos_t = jnp.tile(jnp.concatenate([cos, cos], axis=1), (1, reps))
    sa_t = jnp.tile(jnp.concatenate([-sin, zero], axis=1), (1, reps))
    sb_t = jnp.tile(jnp.concatenate([zero, sin], axis=1), (1, reps))
    return cos_t, sa_t, sb_t


def kernel(x, norm_g, w_in, na_rpb, conv_w, conv_b, conv_ln_g, conv_ln_b, pw_w, pw_b, w_out, final_g):
    batch, seq, _ = x.shape
    depth = w_in.shape[0]
    n = batch * seq
    cos_t, sa_t, sb_t = _rotary_tables(seq)
    h = x.reshape(n, D_MODEL)
    fg = final_g.reshape(1, D_MODEL)
    for l in range(depth):
        conv_params = (conv_w[l], conv_b[l].reshape(1, CONV_CH), conv_ln_g[l].reshape(1, CONV_CH),
                       conv_ln_b[l].reshape(1, CONV_CH), pw_w[l].astype(BF16), pw_b[l].reshape(1, CONV_CH))
        (naq, nak, nav, q1, q4, q16, k1, k4, k16, v1, v4, v16, u, gate) = _in_proj(
            h, norm_g[l].reshape(1, D_MODEL), w_in[l].astype(BF16), cos_t, sa_t, sb_t, batch, seq)
        ya = _na_attention(naq, nak, nav, _na_bias_table(na_rpb[l]), batch, seq)
        outs, lses = [], []
        for d, (q, k, v) in zip(DIL_DILATIONS, ((q1, k1, v1), (q4, k4, v4), (q16, k16, v16))):
            groups = (batch * d, seq // d)
            o, lse = _band_attention(q.reshape(*groups, DIL_W), k.reshape(*groups, DIL_W),
                                     v.reshape(*groups, DIL_W))
            if d == 1:
                outs.append(o.reshape(n, DIL_W))
                lses.append(lse.reshape(n, LANES))
            else:
                outs.append(o.reshape(batch, d, seq // d, DIL_W))
                lses.append(lse.reshape(batch, d, seq // d, LANES))
        h = _out_proj(ya, outs, lses, u, gate, h, w_out[l].astype(BF16), fg, conv_params,
                      final_norm=(l == depth - 1), batch=batch, seq=seq)
    return h.reshape(batch, seq, D_MODEL)
```

```python
import functools

import numpy as np
import jax
import jax.numpy as jnp
from jax import lax
from jax.experimental import pallas as pl
from jax.experimental.pallas import tpu as pltpu

D_MODEL = 1024
HEAD_DIM = 64
GRID_W = 64
NA_HEADS = 4
NA_WIN_ROWS = 8
NA_WIN_COLS = 16
DIL_HEADS = 6
DIL_DILATIONS = (1, 4, 16)
DIL_HALF = 64
CONV_CH = 384
CONV_WIDTH = 31
ROPE_THETA = 10000.0
NORM_EPS = 1e-6
NEG_INF = -1e30
NA_W = NA_HEADS * HEAD_DIM
DIL_W = DIL_HEADS * HEAD_DIM
D_MIX = NA_W + DIL_W + CONV_CH
SCALE = HEAD_DIM ** -0.5
LOG2E = 1.4426950408889634
Q_SCALE = SCALE * LOG2E

LANES = 128
SUBLANES = 8
VMEM_LIMIT = 56 * 1024 * 1024
DIL_SLABS = DIL_W // LANES

_C = np.cumsum([0] + [NA_W] * 4 + [DIL_W] * 4 + [CONV_CH] * 3)
(C_AQ, C_AK, C_AV, C_AG, C_BQ, C_BK, C_BV, C_BG, C_CA, C_CB, C_CG, C_END) = [int(c) for c in _C]

BF16 = jnp.bfloat16
F32 = jnp.float32

TOKEN_TILE = 1024
OUT_TILE = 1024
CONV_HALO = 16
CONV_PAD = CONV_WIDTH // 2
CONV_ROWS = 128


def _params(*sem):
    return pltpu.CompilerParams(dimension_semantics=sem, vmem_limit_bytes=VMEM_LIMIT)


def _silu(v):
    return v * (1.0 / (1.0 + jnp.exp(-v)))


def _lane_is_first_head(shape):
    return lax.broadcasted_iota(jnp.int32, shape, len(shape) - 1) < HEAD_DIM


def _in_proj_kernel(xp_ref, x_ref, xn_ref, g_ref, w_ref, cos_ref, sa_ref, sb_ref,
                    cw_ref, cb_ref, lg_ref, lb_ref, pw_ref, pb_ref,
                    naq_ref, nak_ref, nav_ref,
                    q1_ref, q4_ref, q16_ref, k1_ref, k4_ref, k16_ref, v1_ref, v4_ref, v16_ref,
                    ycg_ref, gate_ref, perm, perm4, ubuf, ybuf, *, seq_blocks):
    tm = x_ref.shape[0]

    def normed(x):
        ms = jnp.mean(x * x, axis=-1, keepdims=True)
        return (x * lax.rsqrt(ms + NORM_EPS) * g_ref[...]).astype(BF16)

    h = normed(x_ref[...])

    def mm(c0, c1):
        return jnp.dot(h, w_ref[:, c0:c1], preferred_element_type=F32)

    cos = cos_ref[...]
    sa = sa_ref[...]
    sb = sb_ref[...]

    def rope(zs):
        up = pltpu.roll(zs, LANES - HEAD_DIM // 2, 1)
        dn = pltpu.roll(zs, HEAD_DIM // 2, 1)
        return zs * cos + up * sa + dn * sb

    def emit_dilated(z, refs, rotary, scale):
        ref1, ref4, ref16 = refs
        for s in range(DIL_SLABS):
            zs = z[:, s * LANES:(s + 1) * LANES]
            if rotary:
                zs = rope(zs)
            if scale != 1.0:
                zs = zs * scale
            perm[s] = zs
            ref1[:, s * LANES:(s + 1) * LANES] = zs.astype(BF16)
        for r1 in range(4):
            for s in range(DIL_SLABS):
                t = perm[s, pl.ds(r1, tm // 4, stride=4), :]
                perm4[s, r1] = t
                ref4[0, r1, :, s * LANES:(s + 1) * LANES] = t.astype(BF16)
        for rho in range(16):
            for s in range(DIL_SLABS):
                ref16[0, rho, :, s * LANES:(s + 1) * LANES] = (
                    perm4[s, rho % 4, pl.ds(rho // 4, tm // 16, stride=4), :].astype(BF16))

    i = pl.program_id(0)
    pos = i % seq_blocks
    h_ext = jnp.concatenate([normed(xp_ref[...]), h, normed(xn_ref[...])], axis=0)
    zc = jnp.dot(h_ext, w_ref[:, C_CA:C_CG], preferred_element_type=F32)
    u = zc[:, 0:CONV_CH] * (1.0 / (1.0 + jnp.exp(-zc[:, CONV_CH:2 * CONV_CH])))
    ubuf[0:CONV_HALO, :] = u[0:CONV_HALO] * jnp.where(pos > 0, 1.0, 0.0).astype(F32)
    ubuf[CONV_HALO:CONV_HALO + tm, :] = u[CONV_HALO:CONV_HALO + tm]
    ubuf[CONV_HALO + tm:, :] = u[CONV_HALO + tm:] * jnp.where(pos < seq_blocks - 1, 1.0, 0.0).astype(F32)

    first_off = CONV_HALO - CONV_PAD
    for c in range(tm // CONV_ROWS):
        r0 = c * CONV_ROWS
        for sl in range(CONV_CH // LANES):
            lanes = slice(sl * LANES, (sl + 1) * LANES)
            y = None
            for s in range(SUBLANES):
                part = None
                for k in range(CONV_WIDTH):
                    off = first_off + k
                    if off % SUBLANES != s:
                        continue
                    win = ubuf[pl.ds(r0 + (off - s), CONV_ROWS + SUBLANES), lanes]
                    term = win * cw_ref[k:k + 1, lanes]
                    part = term if part is None else part + term
                shifted = part[s:s + CONV_ROWS, :]
                y = shifted if y is None else y + shifted
            ybuf[pl.ds(r0, CONV_ROWS), lanes] = y

    za = mm(C_AQ, C_BQ)
    naq_ref[...] = (za[:, 0:NA_W] * Q_SCALE).astype(BF16)
    nak_ref[...] = za[:, NA_W:2 * NA_W].astype(BF16)
    nav_ref[...] = za[:, 2 * NA_W:3 * NA_W].astype(BF16)
    gate_ref[:, 0:NA_W] = _silu(za[:, 3 * NA_W:4 * NA_W]).astype(BF16)

    zqk = mm(C_BQ, C_BV)
    emit_dilated(zqk[:, 0:DIL_W], (q1_ref, q4_ref, q16_ref), True, Q_SCALE)
    emit_dilated(zqk[:, DIL_W:2 * DIL_W], (k1_ref, k4_ref, k16_ref), True, 1.0)

    zvg = mm(C_BV, C_CA)
    emit_dilated(zvg[:, 0:DIL_W], (v1_ref, v4_ref, v16_ref), False, 1.0)
    gate_ref[:, NA_W:NA_W + DIL_W] = _silu(zvg[:, DIL_W:2 * DIL_W]).astype(BF16)

    y = ybuf[...] + cb_ref[...]
    mu = jnp.mean(y, axis=-1, keepdims=True)
    yc = y - mu
    var = jnp.mean(yc * yc, axis=-1, keepdims=True)
    yn = yc * lax.rsqrt(var + NORM_EPS) * lg_ref[...] + lb_ref[...]
    act = _silu(yn).astype(BF16)
    yc = jnp.dot(act, pw_ref[...], preferred_element_type=F32) + pb_ref[...]
    ycg_ref[...] = (yc * _silu(mm(C_CG, C_END))).astype(BF16)


def _in_proj(x2d, g, w_bf16, cos_t, sa_t, sb_t, conv_params, batch, seq):
    n = x2d.shape[0]
    tm = TOKEN_TILE
    seq_blocks = seq // tm
    halo_per_tile = tm // CONV_HALO
    row = lambda i: (i, 0)
    prev = lambda i: (jnp.maximum(i * halo_per_tile - 1, 0), 0)
    nxt = lambda i: (jnp.minimum((i + 1) * halo_per_tile, n // CONV_HALO - 1), 0)
    tab = lambda i: (i % seq_blocks, 0)
    const = lambda i: (0, 0)
    perm_idx = lambda i: (i // seq_blocks, 0, i % seq_blocks, 0)
    vec = pl.BlockSpec((1, CONV_CH), const)

    def dil_shapes():
        return [jax.ShapeDtypeStruct((n, DIL_W), BF16)] + [
            jax.ShapeDtypeStruct((batch, d, seq // d, DIL_W), BF16) for d in DIL_DILATIONS[1:]]

    def dil_specs():
        return [pl.BlockSpec((tm, DIL_W), row)] + [
            pl.BlockSpec((1, d, tm // d, DIL_W), perm_idx) for d in DIL_DILATIONS[1:]]

    out_shapes = ([jax.ShapeDtypeStruct((n, NA_W), BF16)] * 3 + dil_shapes() + dil_shapes() + dil_shapes()
                  + [jax.ShapeDtypeStruct((n, CONV_CH), BF16), jax.ShapeDtypeStruct((n, NA_W + DIL_W), BF16)])
    out_specs = ([pl.BlockSpec((tm, NA_W), row)] * 3 + dil_specs() + dil_specs() + dil_specs()
                 + [pl.BlockSpec((tm, CONV_CH), row), pl.BlockSpec((tm, NA_W + DIL_W), row)])
    return pl.pallas_call(
        functools.partial(_in_proj_kernel, seq_blocks=seq_blocks),
        out_shape=out_shapes,
        grid=(n // tm,),
        in_specs=[
            pl.BlockSpec((CONV_HALO, D_MODEL), prev),
            pl.BlockSpec((tm, D_MODEL), row),
            pl.BlockSpec((CONV_HALO, D_MODEL), nxt),
            pl.BlockSpec((1, D_MODEL), const),
            pl.BlockSpec((D_MODEL, C_END), const),
            pl.BlockSpec((tm, LANES), tab),
            pl.BlockSpec((tm, LANES), tab),
            pl.BlockSpec((tm, LANES), tab),
            pl.BlockSpec((CONV_WIDTH, CONV_CH), const),
            vec, vec, vec,
            pl.BlockSpec((CONV_CH, CONV_CH), const),
            vec,
        ],
        out_specs=out_specs,
        scratch_shapes=[pltpu.VMEM((DIL_SLABS, tm, LANES), F32),
                        pltpu.VMEM((DIL_SLABS, 4, tm // 4, LANES), F32),
                        pltpu.VMEM((tm + 2 * CONV_HALO, CONV_CH), F32),
                        pltpu.VMEM((tm, CONV_CH), F32)],
        compiler_params=_params("parallel"),
        name="in_proj",
    )(x2d, x2d, x2d, g, w_bf16, cos_t, sa_t, sb_t, *conv_params)


NA_ROWS_PER_STEP = 32
NA_KEYS = NA_WIN_ROWS * GRID_W


def _na_kernel(q_ref, k_ref, v_ref, bias_ref, o_ref, sbuf, ebuf, *, rows):
    blk = pl.program_id(1)
    first = _lane_is_first_head((GRID_W, LANES))
    ones = jnp.ones((NA_KEYS, LANES), BF16)
    n_pairs = NA_W // LANES
    starts, maxes = [], []
    for rr in range(NA_ROWS_PER_STEP):
        r = blk * NA_ROWS_PER_STEP + rr
        r0 = jnp.clip(r - NA_WIN_ROWS // 2, 0, rows - NA_WIN_ROWS)
        delta = r0 - r + (NA_WIN_ROWS - 1)
        start = pl.multiple_of(r0 * GRID_W, GRID_W)
        starts.append(start)
        q_row = q_ref[0, rr * GRID_W:(rr + 1) * GRID_W, :]
        k_win = k_ref[0, pl.ds(start, NA_KEYS), :]
        for p in range(n_pairs):
            qp = q_row[:, p * LANES:(p + 1) * LANES]
            zero = jnp.zeros_like(qp)
            lhs = jnp.concatenate([jnp.where(first, qp, zero), jnp.where(first, zero, qp)], axis=0)
            kp = k_win[:, p * LANES:(p + 1) * LANES]
            s = lax.dot_general(lhs, kp, (((1,), (1,)), ((), ())), preferred_element_type=F32)
            s = s + bias_ref[delta, p]
            sbuf[rr * n_pairs + p] = s
            maxes.append(jnp.max(s, axis=-1, keepdims=True))
    for c in range(NA_ROWS_PER_STEP * n_pairs):
        ebuf[c] = jnp.exp2(sbuf[c] - maxes[c]).astype(BF16)
    for rr in range(NA_ROWS_PER_STEP):
        v_win = v_ref[0, pl.ds(starts[rr], NA_KEYS), :]
        for p in range(n_pairs):
            v_aug = jnp.concatenate([v_win[:, p * LANES:(p + 1) * LANES], ones], axis=1)
            o = jnp.dot(ebuf[rr * n_pairs + p], v_aug, preferred_element_type=F32)
            o = o[:, 0:LANES] * (1.0 / o[:, LANES:2 * LANES])
            out = jnp.where(first, o[0:GRID_W], o[GRID_W:2 * GRID_W])
            o_ref[0, rr * GRID_W:(rr + 1) * GRID_W, p * LANES:(p + 1) * LANES] = out.astype(BF16)


def _na_bias_table(rpb):
    cols = np.arange(GRID_W)
    col_start = np.clip(cols - NA_WIN_COLS // 2, 0, GRID_W - NA_WIN_COLS)
    kc = np.arange(GRID_W)
    in_win = (kc[None, :] >= col_start[:, None]) & (kc[None, :] < col_start[:, None] + NA_WIN_COLS)
    col_off = kc[None, :] - cols[:, None] + (NA_WIN_COLS - 1)
    n_off = 2 * NA_WIN_COLS - 1
    onehot = (col_off[None] == np.arange(n_off)[:, None, None]) & in_win[None]
    t = jnp.einsum('hrj,jck->hrck', rpb.astype(F32), jnp.asarray(onehot, F32),
                   precision=lax.Precision.HIGHEST)
    t = jnp.where(jnp.asarray(in_win)[None, None], t * LOG2E, NEG_INF)
    t = jnp.stack([t[:, dlt:dlt + NA_WIN_ROWS] for dlt in range(NA_WIN_ROWS)], axis=0)
    t = t.transpose(0, 1, 3, 2, 4)
    return t.reshape(NA_WIN_ROWS, NA_HEADS // 2, 2 * GRID_W, NA_KEYS)


def _na_attention(q, k, v, bias, batch, seq):
    rows = seq // GRID_W
    tq = NA_ROWS_PER_STEP * GRID_W
    q3 = q.reshape(batch, seq, NA_W)
    k3 = k.reshape(batch, seq, NA_W)
    v3 = v.reshape(batch, seq, NA_W)
    out = pl.pallas_call(
        functools.partial(_na_kernel, rows=rows),
        out_shape=jax.ShapeDtypeStruct((batch, seq, NA_W), BF16),
        grid=(batch, seq // tq),
        in_specs=[
            pl.BlockSpec((1, tq, NA_W), lambda b, i: (b, i, 0)),
            pl.BlockSpec((1, seq, NA_W), lambda b, i: (b, 0, 0)),
            pl.BlockSpec((1, seq, NA_W), lambda b, i: (b, 0, 0)),
            pl.BlockSpec(bias.shape, lambda b, i: (0, 0, 0, 0)),
        ],
        out_specs=pl.BlockSpec((1, tq, NA_W), lambda b, i: (b, i, 0)),
        scratch_shapes=[pltpu.VMEM((NA_ROWS_PER_STEP * NA_W // LANES, 2 * GRID_W, NA_KEYS), F32),
                        pltpu.VMEM((NA_ROWS_PER_STEP * NA_W // LANES, 2 * GRID_W, NA_KEYS), BF16)],
        compiler_params=_params("parallel", "arbitrary"),
        name="na_attention",
    )(q3, k3, v3, bias)
    return out.reshape(batch * seq, NA_W)


BAND_CHUNK = 128
BAND_KEYS = BAND_CHUNK + 2 * DIL_HALF
BAND_STEP_TOKENS = 2048
STAT_SUM_LANE = 8


def _band_kernel(q_ref, kp_ref, kc_ref, kn_ref, vp_ref, vc_ref, vn_ref, o_ref, stat_ref, *, length):
    blk = pl.program_id(1)
    BAND_TQ = q_ref.shape[1]
    n_chunks = BAND_TQ // BAND_CHUNK
    first = _lane_is_first_head((BAND_CHUNK, LANES))
    lane = lax.broadcasted_iota(jnp.int32, (BAND_CHUNK, LANES), 1)
    qi = lax.broadcasted_iota(jnp.int32, (BAND_CHUNK, BAND_KEYS), 0)
    kj = lax.broadcasted_iota(jnp.int32, (BAND_CHUNK, BAND_KEYS), 1)
    rel = kj - DIL_HALF - qi
    band = (rel >= -DIL_HALF) & (rel <= DIL_HALF)
    ones = jnp.ones((BAND_KEYS, LANES), BF16)
    for gc in range(q_ref.shape[0] * n_chunks):
        gb, c = divmod(gc, n_chunks)
        a0 = blk * BAND_TQ + c * BAND_CHUNK
        kpos = a0 - DIL_HALF + kj
        ok = band & (kpos >= 0) & (kpos < length)
        mask_bias = jnp.where(ok, 0.0, NEG_INF).astype(F32)
        mask_bias2 = jnp.concatenate([mask_bias, mask_bias], axis=0)
        lo = c * BAND_CHUNK - DIL_HALF
        hi = lo + BAND_KEYS

        def window(prev_ref, cur_ref, next_ref):
            pieces = []
            if lo < 0:
                pieces.append(prev_ref[gb, :, :])
            pieces.append(cur_ref[gb, max(lo, 0):min(hi, BAND_TQ), :])
            if hi > BAND_TQ:
                pieces.append(next_ref[gb, :, :])
            return pieces[0] if len(pieces) == 1 else jnp.concatenate(pieces, axis=0)

        k_win = window(kp_ref, kc_ref, kn_ref)
        v_win = window(vp_ref, vc_ref, vn_ref)
        q_chunk = q_ref[gb, c * BAND_CHUNK:(c + 1) * BAND_CHUNK, :]
        rows = slice(c * BAND_CHUNK, (c + 1) * BAND_CHUNK)
        stat_tile = jnp.zeros((BAND_CHUNK, LANES), F32)
        for p in range(DIL_SLABS):
            qp = q_chunk[:, p * LANES:(p + 1) * LANES]
            zero = jnp.zeros_like(qp)
            lhs = jnp.concatenate([jnp.where(first, qp, zero), jnp.where(first, zero, qp)], axis=0)
            kp = k_win[:, p * LANES:(p + 1) * LANES]
            vp = v_win[:, p * LANES:(p + 1) * LANES]
            s = lax.dot_general(lhs, kp, (((1,), (1,)), ((), ())), preferred_element_type=F32)
            s = s + mask_bias2
            m = jnp.max(s, axis=-1, keepdims=True)
            e = jnp.exp2(s - m).astype(BF16)
            o = jnp.dot(e, jnp.concatenate([vp, ones], axis=1), preferred_element_type=F32)
            l = o[:, LANES:2 * LANES]
            o_ref[gb, rows, p * LANES:(p + 1) * LANES] = jnp.where(
                first, o[0:BAND_CHUNK, 0:LANES], o[BAND_CHUNK:, 0:LANES]).astype(BF16)
            for hh in range(2):
                half = slice(hh * BAND_CHUNK, (hh + 1) * BAND_CHUNK)
                stat_tile = jnp.where(lane == 2 * p + hh, m[half], stat_tile)
                stat_tile = jnp.where(lane == STAT_SUM_LANE + 2 * p + hh, l[half], stat_tile)
        stat_ref[gb, rows, :] = stat_tile


def _band_attention(q, k, v):
    groups, length, _ = q.shape
    tq = min(BAND_STEP_TOKENS, length)
    gb = BAND_STEP_TOKENS // tq
    halo_per_blk = tq // DIL_HALF
    n_halo = length // DIL_HALF
    cur = lambda g, i: (g, i, 0)
    prev = lambda g, i: (g, jnp.maximum(i * halo_per_blk - 1, 0), 0)
    nxt = lambda g, i: (g, jnp.minimum((i + 1) * halo_per_blk, n_halo - 1), 0)
    blk = pl.BlockSpec((gb, tq, DIL_W), cur)
    halo_p = pl.BlockSpec((gb, DIL_HALF, DIL_W), prev)
    halo_n = pl.BlockSpec((gb, DIL_HALF, DIL_W), nxt)
    return pl.pallas_call(
        functools.partial(_band_kernel, length=length),
        out_shape=[jax.ShapeDtypeStruct((groups, length, DIL_W), BF16),
                   jax.ShapeDtypeStruct((groups, length, LANES), F32)],
        grid=(groups // gb, length // tq),
        in_specs=[blk, halo_p, blk, halo_n, halo_p, blk, halo_n],
        out_specs=[pl.BlockSpec((gb, tq, DIL_W), cur), pl.BlockSpec((gb, tq, LANES), cur)],
        compiler_params=_params("parallel", "arbitrary"),
        name="band_attention",
    )(q, k, k, k, v, v, v)


def _out_proj_kernel(ya_ref, o1_ref, o4_ref, o16_ref, l1_ref, l4_ref, l16_ref, ycg_ref, gate_ref,
                     x_ref, w_ref, fg_ref, out_ref, onat, lnat, *, final_norm):
    tm = x_ref.shape[0]
    first = _lane_is_first_head((tm, LANES))

    def natural_stats(l_ref, d, slot):
        if d == 1:
            return l_ref[...]
        for rho in range(d):
            lnat[slot, pl.ds(rho, tm // d, stride=d), :] = l_ref[0, rho]
        return lnat[slot]

    stats = (natural_stats(l1_ref, 1, 0), natural_stats(l4_ref, 4, 0), natural_stats(l16_ref, 16, 1))
    m = jnp.maximum(jnp.maximum(stats[0], stats[1]), stats[2])
    scales = [jnp.exp2(st - m) for st in stats]
    denom = None
    for st, sc in zip(stats, scales):
        row_sum = pltpu.roll(st, LANES - STAT_SUM_LANE, 1)
        denom = sc * row_sum if denom is None else denom + sc * row_sum
    inv = 1.0 / denom
    weights = [sc * inv for sc in scales]

    for slot, (d, o_ref) in enumerate(((4, o4_ref), (16, o16_ref))):
        for rho in range(d):
            for s in range(DIL_SLABS):
                onat[slot, s, pl.ds(rho, tm // d, stride=d), :] = (
                    o_ref[0, rho, :, s * LANES:(s + 1) * LANES].astype(F32))

    gate = gate_ref[...].astype(F32)
    pieces = [ya_ref[...].astype(F32) * gate[:, 0:NA_W]]
    for s in range(DIL_SLABS):
        outs = (o1_ref[:, s * LANES:(s + 1) * LANES].astype(F32), onat[0, s], onat[1, s])
        yb = None
        for w, o in zip(weights, outs):
            wb = jnp.where(first, w[:, 2 * s:2 * s + 1], w[:, 2 * s + 1:2 * s + 2])
            yb = wb * o if yb is None else yb + wb * o
        pieces.append(yb * gate[:, NA_W + s * LANES:NA_W + (s + 1) * LANES])
    mix = jnp.concatenate([piece.astype(BF16) for piece in pieces] + [ycg_ref[...]], axis=1)
    y = x_ref[...] + jnp.dot(mix, w_ref[...], preferred_element_type=F32)
    if final_norm:
        ms = jnp.mean(y * y, axis=-1, keepdims=True)
        y = y * lax.rsqrt(ms + NORM_EPS) * fg_ref[...]
    out_ref[...] = y


def _out_proj(ya, outs, lses, ycg, gate, x2d, w_bf16, final_g, final_norm, batch, seq):
    n = x2d.shape[0]
    tm = OUT_TILE
    seq_blocks = seq // tm
    row = lambda i: (i, 0)
    const = lambda i: (0, 0)
    perm_idx = lambda i: (i // seq_blocks, 0, i % seq_blocks, 0)
    o_specs = [pl.BlockSpec((tm, DIL_W), row)] + [
        pl.BlockSpec((1, d, tm // d, DIL_W), perm_idx) for d in DIL_DILATIONS[1:]]
    l_specs = [pl.BlockSpec((tm, LANES), row)] + [
        pl.BlockSpec((1, d, tm // d, LANES), perm_idx) for d in DIL_DILATIONS[1:]]
    return pl.pallas_call(
        functools.partial(_out_proj_kernel, final_norm=final_norm),
        out_shape=jax.ShapeDtypeStruct((n, D_MODEL), F32),
        grid=(n // tm,),
        in_specs=[pl.BlockSpec((tm, NA_W), row)] + o_specs + l_specs + [
            pl.BlockSpec((tm, CONV_CH), row),
            pl.BlockSpec((tm, NA_W + DIL_W), row),
            pl.BlockSpec((tm, D_MODEL), row),
            pl.BlockSpec((D_MIX, D_MODEL), const),
            pl.BlockSpec((1, D_MODEL), const),
        ],
        out_specs=pl.BlockSpec((tm, D_MODEL), row),
        scratch_shapes=[pltpu.VMEM((2, DIL_SLABS, tm, LANES), F32),
                        pltpu.VMEM((2, tm, LANES), F32)],
        compiler_params=_params("parallel"),
        name="out_proj",
    )(ya, *outs, *lses, ycg, gate, x2d, w_bf16, final_g)


def _rotary_tables(seq):
    half = HEAD_DIM // 2
    inv_freq = jnp.power(ROPE_THETA, -jnp.arange(half, dtype=F32) * 2.0 / HEAD_DIM)
    ang = jnp.arange(seq).astype(F32)[:, None] * inv_freq[None, :]
    cos = jnp.cos(ang)
    sin = jnp.sin(ang)
    zero = jnp.zeros_like(sin)
    reps = LANES // HEAD_DIM
    cos_t = jnp.tile(jnp.concatenate([cos, cos], axis=1), (1, reps))
    sa_t = jnp.tile(jnp.concatenate([-sin, zero], axis=1), (1, reps))
    sb_t = jnp.tile(jnp.concatenate([zero, sin], axis=1), (1, reps))
    return cos_t, sa_t, sb_t


def kernel(x, norm_g, w_in, na_rpb, conv_w, conv_b, conv_ln_g, conv_ln_b, pw_w, pw_b, w_out, final_g):
    batch, seq, _ = x.shape
    depth = w_in.shape[0]
    n = batch * seq
    cos_t, sa_t, sb_t = _rotary_tables(seq)
    h = x.reshape(n, D_MODEL)
    fg = final_g.reshape(1, D_MODEL)
    for l in range(depth):
        conv_params = (conv_w[l], conv_b[l].reshape(1, CONV_CH), conv_ln_g[l].reshape(1, CONV_CH),
                       conv_ln_b[l].reshape(1, CONV_CH), pw_w[l].astype(BF16), pw_b[l].reshape(1, CONV_CH))
        (naq, nak, nav, q1, q4, q16, k1, k4, k16, v1, v4, v16, ycg, gate) = _in_proj(
            h, norm_g[l].reshape(1, D_MODEL), w_in[l].astype(BF16), cos_t, sa_t, sb_t, conv_params, batch, seq)
        ya = _na_attention(naq, nak, nav, _na_bias_table(na_rpb[l]), batch, seq)
        outs, lses = [], []
        for d, (q, k, v) in zip(DIL_DILATIONS, ((q1, k1, v1), (q4, k4, v4), (q16, k16, v16))):
            groups = (batch * d, seq // d)
            o, lse = _band_attention(q.reshape(*groups, DIL_W), k.reshape(*groups, DIL_W),
                                     v.reshape(*groups, DIL_W))
            if d == 1:
                outs.append(o.reshape(n, DIL_W))
                lses.append(lse.reshape(n, LANES))
            else:
                outs.append(o.reshape(batch, d, seq // d, DIL_W))
                lses.append(lse.reshape(batch, d, seq // d, LANES))
        h = _out_proj(ya, outs, lses, ycg, gate, h, w_out[l].astype(BF16), fg,
                      final_norm=(l == depth - 1), batch=batch, seq=seq)
    return h.reshape(batch, seq, D_MODEL)
```

```python
import functools

import numpy as np
import jax
import jax.numpy as jnp
from jax import lax
from jax.experimental import pallas as pl
from jax.experimental.pallas import tpu as pltpu

D_MODEL = 1024
HEAD_DIM = 64
GRID_W = 64
NA_HEADS = 4
NA_WIN_ROWS = 8
NA_WIN_COLS = 16
DIL_HEADS = 6
DIL_DILATIONS = (1, 4, 16)
DIL_HALF = 64
CONV_CH = 384
CONV_WIDTH = 31
ROPE_THETA = 10000.0
NORM_EPS = 1e-6
NEG_INF = -1e30
NA_W = NA_HEADS * HEAD_DIM
DIL_W = DIL_HEADS * HEAD_DIM
D_MIX = NA_W + DIL_W + CONV_CH
SCALE = HEAD_DIM ** -0.5
LOG2E = 1.4426950408889634
Q_SCALE = SCALE * LOG2E

LANES = 128
SUBLANES = 8
VMEM_LIMIT = 56 * 1024 * 1024
DIL_SLABS = DIL_W // LANES

_C = np.cumsum([0] + [NA_W] * 4 + [DIL_W] * 4 + [CONV_CH] * 3)
(C_AQ, C_AK, C_AV, C_AG, C_BQ, C_BK, C_BV, C_BG, C_CA, C_CB, C_CG, C_END) = [int(c) for c in _C]

BF16 = jnp.bfloat16
F32 = jnp.float32

TOKEN_TILE = 1024
OUT_TILE = 1024
CONV_HALO = 16
CONV_PAD = CONV_WIDTH // 2
CONV_ROWS = 128


def _params(*sem):
    return pltpu.CompilerParams(dimension_semantics=sem, vmem_limit_bytes=VMEM_LIMIT)


def _silu(v):
    return v * (1.0 / (1.0 + jnp.exp(-v)))


def _lane_is_first_head(shape):
    return lax.broadcasted_iota(jnp.int32, shape, len(shape) - 1) < HEAD_DIM


def _in_proj_kernel(xp_ref, x_ref, xn_ref, g_ref, w_ref, cos_ref, sa_ref, sb_ref,
                    cw_ref, cb_ref, lg_ref, lb_ref, pw_ref, pb_ref,
                    naq_ref, nak_ref, nav_ref,
                    q1_ref, q4_ref, q16_ref, k1_ref, k4_ref, k16_ref, v1_ref, v4_ref, v16_ref,
                    ycg_ref, gate_a_ref, gate_b_ref, perm, perm4, ubuf, ybuf, *, seq_blocks):
    tm = x_ref.shape[0]

    def normed(x):
        ms = jnp.mean(x * x, axis=-1, keepdims=True)
        return (x * lax.rsqrt(ms + NORM_EPS) * g_ref[...]).astype(BF16)

    h = normed(x_ref[...])

    def mm(c0, c1):
        return jnp.dot(h, w_ref[:, c0:c1], preferred_element_type=F32)

    cos = cos_ref[...]
    sa = sa_ref[...]
    sb = sb_ref[...]

    def rope(zs):
        up = pltpu.roll(zs, LANES - HEAD_DIM // 2, 1)
        dn = pltpu.roll(zs, HEAD_DIM // 2, 1)
        return zs * cos + up * sa + dn * sb

    def emit_dilated(z, refs, rotary, scale):
        ref1, ref4, ref16 = refs
        for s in range(DIL_SLABS):
            zs = z[:, s * LANES:(s + 1) * LANES]
            if rotary:
                zs = rope(zs)
            if scale != 1.0:
                zs = zs * scale
            perm[s] = zs
            ref1[:, s * LANES:(s + 1) * LANES] = zs.astype(BF16)
        for r1 in range(4):
            for s in range(DIL_SLABS):
                t = perm[s, pl.ds(r1, tm // 4, stride=4), :]
                perm4[s, r1] = t
                ref4[0, r1, :, s * LANES:(s + 1) * LANES] = t.astype(BF16)
        for rho in range(16):
            for s in range(DIL_SLABS):
                ref16[0, rho, :, s * LANES:(s + 1) * LANES] = (
                    perm4[s, rho % 4, pl.ds(rho // 4, tm // 16, stride=4), :].astype(BF16))

    i = pl.program_id(0)
    pos = i % seq_blocks
    h_ext = jnp.concatenate([normed(xp_ref[...]), h, normed(xn_ref[...])], axis=0)
    zc = jnp.dot(h_ext, w_ref[:, C_CA:C_CG], preferred_element_type=F32)
    u = zc[:, 0:CONV_CH] * (1.0 / (1.0 + jnp.exp(-zc[:, CONV_CH:2 * CONV_CH])))
    ubuf[0:CONV_HALO, :] = u[0:CONV_HALO] * jnp.where(pos > 0, 1.0, 0.0).astype(F32)
    ubuf[CONV_HALO:CONV_HALO + tm, :] = u[CONV_HALO:CONV_HALO + tm]
    ubuf[CONV_HALO + tm:, :] = u[CONV_HALO + tm:] * jnp.where(pos < seq_blocks - 1, 1.0, 0.0).astype(F32)

    first_off = CONV_HALO - CONV_PAD
    for c in range(tm // CONV_ROWS):
        r0 = c * CONV_ROWS
        for sl in range(CONV_CH // LANES):
            lanes = slice(sl * LANES, (sl + 1) * LANES)
            y = None
            for s in range(SUBLANES):
                part = None
                for k in range(CONV_WIDTH):
                    off = first_off + k
                    if off % SUBLANES != s:
                        continue
                    win = ubuf[pl.ds(r0 + (off - s), CONV_ROWS + SUBLANES), lanes]
                    term = win * cw_ref[k:k + 1, lanes]
                    part = term if part is None else part + term
                shifted = part[s:s + CONV_ROWS, :]
                y = shifted if y is None else y + shifted
            ybuf[pl.ds(r0, CONV_ROWS), lanes] = y

    za = mm(C_AQ, C_BQ)
    naq_ref[...] = (za[:, 0:NA_W] * Q_SCALE).astype(BF16)
    nak_ref[...] = za[:, NA_W:2 * NA_W].astype(BF16)
    nav_ref[...] = za[:, 2 * NA_W:3 * NA_W].astype(BF16)
    gate_a_ref[...] = _silu(za[:, 3 * NA_W:4 * NA_W]).astype(BF16)

    zqk = mm(C_BQ, C_BV)
    emit_dilated(zqk[:, 0:DIL_W], (q1_ref, q4_ref, q16_ref), True, Q_SCALE)
    emit_dilated(zqk[:, DIL_W:2 * DIL_W], (k1_ref, k4_ref, k16_ref), True, 1.0)

    zvg = mm(C_BV, C_CA)
    emit_dilated(zvg[:, 0:DIL_W], (v1_ref, v4_ref, v16_ref), False, 1.0)
    gate_b_ref[...] = _silu(zvg[:, DIL_W:2 * DIL_W]).astype(BF16)

    y = ybuf[...] + cb_ref[...]
    mu = jnp.mean(y, axis=-1, keepdims=True)
    yc = y - mu
    var = jnp.mean(yc * yc, axis=-1, keepdims=True)
    yn = yc * lax.rsqrt(var + NORM_EPS) * lg_ref[...] + lb_ref[...]
    act = _silu(yn).astype(BF16)
    yc = jnp.dot(act, pw_ref[...], preferred_element_type=F32) + pb_ref[...]
    ycg_ref[...] = (yc * _silu(mm(C_CG, C_END))).astype(BF16)


def _in_proj(x2d, g, w_bf16, cos_t, sa_t, sb_t, conv_params, batch, seq):
    n = x2d.shape[0]
    tm = TOKEN_TILE
    seq_blocks = seq // tm
    halo_per_tile = tm // CONV_HALO
    row = lambda i: (i, 0)
    prev = lambda i: (jnp.maximum(i * halo_per_tile - 1, 0), 0)
    nxt = lambda i: (jnp.minimum((i + 1) * halo_per_tile, n // CONV_HALO - 1), 0)
    tab = lambda i: (i % seq_blocks, 0)
    const = lambda i: (0, 0)
    perm_idx = lambda i: (i // seq_blocks, 0, i % seq_blocks, 0)
    vec = pl.BlockSpec((1, CONV_CH), const)

    def dil_shapes():
        return [jax.ShapeDtypeStruct((n, DIL_W), BF16)] + [
            jax.ShapeDtypeStruct((batch, d, seq // d, DIL_W), BF16) for d in DIL_DILATIONS[1:]]

    def dil_specs():
        return [pl.BlockSpec((tm, DIL_W), row)] + [
            pl.BlockSpec((1, d, tm // d, DIL_W), perm_idx) for d in DIL_DILATIONS[1:]]

    out_shapes = ([jax.ShapeDtypeStruct((n, NA_W), BF16)] * 3 + dil_shapes() + dil_shapes() + dil_shapes()
                  + [jax.ShapeDtypeStruct((n, CONV_CH), BF16), jax.ShapeDtypeStruct((n, NA_W), BF16),
                     jax.ShapeDtypeStruct((n, DIL_W), BF16)])
    out_specs = ([pl.BlockSpec((tm, NA_W), row)] * 3 + dil_specs() + dil_specs() + dil_specs()
                 + [pl.BlockSpec((tm, CONV_CH), row), pl.BlockSpec((tm, NA_W), row),
                    pl.BlockSpec((tm, DIL_W), row)])
    return pl.pallas_call(
        functools.partial(_in_proj_kernel, seq_blocks=seq_blocks),
        out_shape=out_shapes,
        grid=(n // tm,),
        in_specs=[
            pl.BlockSpec((CONV_HALO, D_MODEL), prev),
            pl.BlockSpec((tm, D_MODEL), row),
            pl.BlockSpec((CONV_HALO, D_MODEL), nxt),
            pl.BlockSpec((1, D_MODEL), const),
            pl.BlockSpec((D_MODEL, C_END), const),
            pl.BlockSpec((tm, LANES), tab),
            pl.BlockSpec((tm, LANES), tab),
            pl.BlockSpec((tm, LANES), tab),
            pl.BlockSpec((CONV_WIDTH, CONV_CH), const),
            vec, vec, vec,
            pl.BlockSpec((CONV_CH, CONV_CH), const),
            vec,
        ],
        out_specs=out_specs,
        scratch_shapes=[pltpu.VMEM((DIL_SLABS, tm, LANES), F32),
                        pltpu.VMEM((DIL_SLABS, 4, tm // 4, LANES), F32),
                        pltpu.VMEM((tm + 2 * CONV_HALO, CONV_CH), F32),
                        pltpu.VMEM((tm, CONV_CH), F32)],
        compiler_params=_params("parallel"),
        name="in_proj",
    )(x2d, x2d, x2d, g, w_bf16, cos_t, sa_t, sb_t, *conv_params)


NA_ROWS_PER_STEP = 32
NA_KEYS = NA_WIN_ROWS * GRID_W


def _na_kernel(q_ref, k_ref, v_ref, bias_ref, gate_ref, o_ref, sbuf, ebuf, *, rows):
    blk = pl.program_id(1)
    first = _lane_is_first_head((GRID_W, LANES))
    ones = jnp.ones((NA_KEYS, LANES), BF16)
    n_pairs = NA_W // LANES
    starts, maxes = [], []
    for rr in range(NA_ROWS_PER_STEP):
        r = blk * NA_ROWS_PER_STEP + rr
        r0 = jnp.clip(r - NA_WIN_ROWS // 2, 0, rows - NA_WIN_ROWS)
        delta = r0 - r + (NA_WIN_ROWS - 1)
        start = pl.multiple_of(r0 * GRID_W, GRID_W)
        starts.append(start)
        q_row = q_ref[0, rr * GRID_W:(rr + 1) * GRID_W, :]
        k_win = k_ref[0, pl.ds(start, NA_KEYS), :]
        for p in range(n_pairs):
            qp = q_row[:, p * LANES:(p + 1) * LANES]
            zero = jnp.zeros_like(qp)
            lhs = jnp.concatenate([jnp.where(first, qp, zero), jnp.where(first, zero, qp)], axis=0)
            kp = k_win[:, p * LANES:(p + 1) * LANES]
            s = lax.dot_general(lhs, kp, (((1,), (1,)), ((), ())), preferred_element_type=F32)
            s = s + bias_ref[delta, p]
            sbuf[rr * n_pairs + p] = s
            maxes.append(jnp.max(s, axis=-1, keepdims=True))
    for c in range(NA_ROWS_PER_STEP * n_pairs):
        ebuf[c] = jnp.exp2(sbuf[c] - maxes[c]).astype(BF16)
    for rr in range(NA_ROWS_PER_STEP):
        v_win = v_ref[0, pl.ds(starts[rr], NA_KEYS), :]
        for p in range(n_pairs):
            v_aug = jnp.concatenate([v_win[:, p * LANES:(p + 1) * LANES], ones], axis=1)
            o = jnp.dot(ebuf[rr * n_pairs + p], v_aug, preferred_element_type=F32)
            o = o[:, 0:LANES] * (1.0 / o[:, LANES:2 * LANES])
            out = jnp.where(first, o[0:GRID_W], o[GRID_W:2 * GRID_W])
            tile = (slice(rr * GRID_W, (rr + 1) * GRID_W), slice(p * LANES, (p + 1) * LANES))
            o_ref[0, tile[0], tile[1]] = (out * gate_ref[0, tile[0], tile[1]].astype(F32)).astype(BF16)


def _na_bias_table(rpb):
    cols = np.arange(GRID_W)
    col_start = np.clip(cols - NA_WIN_COLS // 2, 0, GRID_W - NA_WIN_COLS)
    kc = np.arange(GRID_W)
    in_win = (kc[None, :] >= col_start[:, None]) & (kc[None, :] < col_start[:, None] + NA_WIN_COLS)
    col_off = kc[None, :] - cols[:, None] + (NA_WIN_COLS - 1)
    n_off = 2 * NA_WIN_COLS - 1
    onehot = (col_off[None] == np.arange(n_off)[:, None, None]) & in_win[None]
    t = jnp.einsum('hrj,jck->hrck', rpb.astype(F32), jnp.asarray(onehot, F32),
                   precision=lax.Precision.HIGHEST)
    t = jnp.where(jnp.asarray(in_win)[None, None], t * LOG2E, NEG_INF)
    t = jnp.stack([t[:, dlt:dlt + NA_WIN_ROWS] for dlt in range(NA_WIN_ROWS)], axis=0)
    t = t.transpose(0, 1, 3, 2, 4)
    return t.reshape(NA_WIN_ROWS, NA_HEADS // 2, 2 * GRID_W, NA_KEYS)


def _na_attention(q, k, v, bias, gate, batch, seq):
    rows = seq // GRID_W
    tq = NA_ROWS_PER_STEP * GRID_W
    q3 = q.reshape(batch, seq, NA_W)
    k3 = k.reshape(batch, seq, NA_W)
    v3 = v.reshape(batch, seq, NA_W)
    g3 = gate.reshape(batch, seq, NA_W)
    out = pl.pallas_call(
        functools.partial(_na_kernel, rows=rows),
        out_shape=jax.ShapeDtypeStruct((batch, seq, NA_W), BF16),
        grid=(batch, seq // tq),
        in_specs=[
            pl.BlockSpec((1, tq, NA_W), lambda b, i: (b, i, 0)),
            pl.BlockSpec((1, seq, NA_W), lambda b, i: (b, 0, 0)),
            pl.BlockSpec((1, seq, NA_W), lambda b, i: (b, 0, 0)),
            pl.BlockSpec(bias.shape, lambda b, i: (0, 0, 0, 0)),
            pl.BlockSpec((1, tq, NA_W), lambda b, i: (b, i, 0)),
        ],
        out_specs=pl.BlockSpec((1, tq, NA_W), lambda b, i: (b, i, 0)),
        scratch_shapes=[pltpu.VMEM((NA_ROWS_PER_STEP * NA_W // LANES, 2 * GRID_W, NA_KEYS), F32),
                        pltpu.VMEM((NA_ROWS_PER_STEP * NA_W // LANES, 2 * GRID_W, NA_KEYS), BF16)],
        compiler_params=_params("parallel", "arbitrary"),
        name="na_attention",
    )(q3, k3, v3, bias, g3)
    return out.reshape(batch * seq, NA_W)


BAND_CHUNK = 128
BAND_KEYS = BAND_CHUNK + 2 * DIL_HALF
BAND_STEP_TOKENS = 4096
STAT_SUM_LANE = 8


def _band_kernel(q_ref, kp_ref, kc_ref, kn_ref, vp_ref, vc_ref, vn_ref, o_ref, stat_ref, *, length):
    blk = pl.program_id(1)
    BAND_TQ = q_ref.shape[1]
    n_chunks = BAND_TQ // BAND_CHUNK
    first = _lane_is_first_head((BAND_CHUNK, LANES))
    lane = lax.broadcasted_iota(jnp.int32, (BAND_CHUNK, LANES), 1)
    qi = lax.broadcasted_iota(jnp.int32, (BAND_CHUNK, BAND_KEYS), 0)
    kj = lax.broadcasted_iota(jnp.int32, (BAND_CHUNK, BAND_KEYS), 1)
    rel = kj - DIL_HALF - qi
    band = (rel >= -DIL_HALF) & (rel <= DIL_HALF)
    ones = jnp.ones((BAND_KEYS, LANES), BF16)
    for gc in range(q_ref.shape[0] * n_chunks):
        gb, c = divmod(gc, n_chunks)
        a0 = blk * BAND_TQ + c * BAND_CHUNK
        kpos = a0 - DIL_HALF + kj
        ok = band & (kpos >= 0) & (kpos < length)
        mask_bias = jnp.where(ok, 0.0, NEG_INF).astype(F32)
        mask_bias2 = jnp.concatenate([mask_bias, mask_bias], axis=0)
        lo = c * BAND_CHUNK - DIL_HALF
        hi = lo + BAND_KEYS

        def window(prev_ref, cur_ref, next_ref):
            pieces = []
            if lo < 0:
                pieces.append(prev_ref[gb, :, :])
            pieces.append(cur_ref[gb, max(lo, 0):min(hi, BAND_TQ), :])
            if hi > BAND_TQ:
                pieces.append(next_ref[gb, :, :])
            return pieces[0] if len(pieces) == 1 else jnp.concatenate(pieces, axis=0)

        k_win = window(kp_ref, kc_ref, kn_ref)
        v_win = window(vp_ref, vc_ref, vn_ref)
        q_chunk = q_ref[gb, c * BAND_CHUNK:(c + 1) * BAND_CHUNK, :]
        rows = slice(c * BAND_CHUNK, (c + 1) * BAND_CHUNK)
        stat_tile = jnp.zeros((BAND_CHUNK, LANES), F32)
        for p in range(DIL_SLABS):
            qp = q_chunk[:, p * LANES:(p + 1) * LANES]
            zero = jnp.zeros_like(qp)
            lhs = jnp.concatenate([jnp.where(first, qp, zero), jnp.where(first, zero, qp)], axis=0)
            kp = k_win[:, p * LANES:(p + 1) * LANES]
            vp = v_win[:, p * LANES:(p + 1) * LANES]
            s = lax.dot_general(lhs, kp, (((1,), (1,)), ((), ())), preferred_element_type=F32)
            s = s + mask_bias2
            m = jnp.max(s, axis=-1, keepdims=True)
            e = jnp.exp2(s - m).astype(BF16)
            o = jnp.dot(e, jnp.concatenate([vp, ones], axis=1), preferred_element_type=F32)
            l = o[:, LANES:2 * LANES]
            o_ref[gb, rows, p * LANES:(p + 1) * LANES] = jnp.where(
                first, o[0:BAND_CHUNK, 0:LANES], o[BAND_CHUNK:, 0:LANES]).astype(BF16)
            for hh in range(2):
                half = slice(hh * BAND_CHUNK, (hh + 1) * BAND_CHUNK)
                stat_tile = jnp.where(lane == 2 * p + hh, m[half], stat_tile)
                stat_tile = jnp.where(lane == STAT_SUM_LANE + 2 * p + hh, l[half], stat_tile)
        stat_ref[gb, rows, :] = stat_tile


def _band_attention(q, k, v):
    groups, length, _ = q.shape
    tq = min(BAND_STEP_TOKENS, length)
    gb = BAND_STEP_TOKENS // tq
    halo_per_blk = tq // DIL_HALF
    n_halo = length // DIL_HALF
    cur = lambda g, i: (g, i, 0)
    prev = lambda g, i: (g, jnp.maximum(i * halo_per_blk - 1, 0), 0)
    nxt = lambda g, i: (g, jnp.minimum((i + 1) * halo_per_blk, n_halo - 1), 0)
    blk = pl.BlockSpec((gb, tq, DIL_W), cur)
    halo_p = pl.BlockSpec((gb, DIL_HALF, DIL_W), prev)
    halo_n = pl.BlockSpec((gb, DIL_HALF, DIL_W), nxt)
    return pl.pallas_call(
        functools.partial(_band_kernel, length=length),
        out_shape=[jax.ShapeDtypeStruct((groups, length, DIL_W), BF16),
                   jax.ShapeDtypeStruct((groups, length, LANES), F32)],
        grid=(groups // gb, length // tq),
        in_specs=[blk, halo_p, blk, halo_n, halo_p, blk, halo_n],
        out_specs=[pl.BlockSpec((gb, tq, DIL_W), cur), pl.BlockSpec((gb, tq, LANES), cur)],
        compiler_params=_params("parallel", "arbitrary"),
        name="band_attention",
    )(q, k, k, k, v, v, v)


def _out_proj_kernel(ya_ref, o1_ref, o4_ref, o16_ref, l1_ref, l4_ref, l16_ref, ycg_ref, gate_ref,
                     x_ref, w_ref, fg_ref, out_ref, onat, lnat, *, final_norm):
    tm = x_ref.shape[0]
    first = _lane_is_first_head((tm, LANES))

    def natural_stats(l_ref, d, slot):
        if d == 1:
            return l_ref[...]
        for rho in range(d):
            lnat[slot, pl.ds(rho, tm // d, stride=d), :] = l_ref[0, rho]
        return lnat[slot]

    stats = (natural_stats(l1_ref, 1, 0), natural_stats(l4_ref, 4, 0), natural_stats(l16_ref, 16, 1))
    m = jnp.maximum(jnp.maximum(stats[0], stats[1]), stats[2])
    scales = [jnp.exp2(st - m) for st in stats]
    denom = None
    for st, sc in zip(stats, scales):
        row_sum = pltpu.roll(st, LANES - STAT_SUM_LANE, 1)
        denom = sc * row_sum if denom is None else denom + sc * row_sum
    inv = 1.0 / denom
    weights = [sc * inv for sc in scales]

    for slot, (d, o_ref) in enumerate(((4, o4_ref), (16, o16_ref))):
        for rho in range(d):
            for s in range(DIL_SLABS):
                onat[slot, s, pl.ds(rho, tm // d, stride=d), :] = (
                    o_ref[0, rho, :, s * LANES:(s + 1) * LANES].astype(F32))

    gate = gate_ref[...].astype(F32)
    pieces = [ya_ref[...]]
    for s in range(DIL_SLABS):
        outs = (o1_ref[:, s * LANES:(s + 1) * LANES].astype(F32), onat[0, s], onat[1, s])
        yb = None
        for w, o in zip(weights, outs):
            wb = jnp.where(first, w[:, 2 * s:2 * s + 1], w[:, 2 * s + 1:2 * s + 2])
            yb = wb * o if yb is None else yb + wb * o
        pieces.append(yb * gate[:, s * LANES:(s + 1) * LANES])
    mix = jnp.concatenate([piece.astype(BF16) for piece in pieces] + [ycg_ref[...]], axis=1)
    y = x_ref[...] + jnp.dot(mix, w_ref[...], preferred_element_type=F32)
    if final_norm:
        ms = jnp.mean(y * y, axis=-1, keepdims=True)
        y = y * lax.rsqrt(ms + NORM_EPS) * fg_ref[...]
    out_ref[...] = y


def _out_proj(ya, outs, lses, ycg, gate, x2d, w_bf16, final_g, final_norm, batch, seq):
    n = x2d.shape[0]
    tm = OUT_TILE
    seq_blocks = seq // tm
    row = lambda i: (i, 0)
    const = lambda i: (0, 0)
    perm_idx = lambda i: (i // seq_blocks, 0, i % seq_blocks, 0)
    o_specs = [pl.BlockSpec((tm, DIL_W), row)] + [
        pl.BlockSpec((1, d, tm // d, DIL_W), perm_idx) for d in DIL_DILATIONS[1:]]
    l_specs = [pl.BlockSpec((tm, LANES), row)] + [
        pl.BlockSpec((1, d, tm // d, LANES), perm_idx) for d in DIL_DILATIONS[1:]]
    return pl.pallas_call(
        functools.partial(_out_proj_kernel, final_norm=final_norm),
        out_shape=jax.ShapeDtypeStruct((n, D_MODEL), F32),
        grid=(n // tm,),
        in_specs=[pl.BlockSpec((tm, NA_W), row)] + o_specs + l_specs + [
            pl.BlockSpec((tm, CONV_CH), row),
            pl.BlockSpec((tm, DIL_W), row),
            pl.BlockSpec((tm, D_MODEL), row),
            pl.BlockSpec((D_MIX, D_MODEL), const),
            pl.BlockSpec((1, D_MODEL), const),
        ],
        out_specs=pl.BlockSpec((tm, D_MODEL), row),
        scratch_shapes=[pltpu.VMEM((2, DIL_SLABS, tm, LANES), F32),
                        pltpu.VMEM((2, tm, LANES), F32)],
        compiler_params=_params("parallel"),
        name="out_proj",
    )(ya, *outs, *lses, ycg, gate, x2d, w_bf16, final_g)


def _rotary_tables(seq):
    half = HEAD_DIM // 2
    inv_freq = jnp.power(ROPE_THETA, -jnp.arange(half, dtype=F32) * 2.0 / HEAD_DIM)
    ang = jnp.arange(seq).astype(F32)[:, None] * inv_freq[None, :]
    cos = jnp.cos(ang)
    sin = jnp.sin(ang)
    zero = jnp.zeros_like(sin)
    reps = LANES // HEAD_DIM
    cos_t = jnp.tile(jnp.concatenate([cos, cos], axis=1), (1, reps))
    sa_t = jnp.tile(jnp.concatenate([-sin, zero], axis=1), (1, reps))
    sb_t = jnp.tile(jnp.concatenate([zero, sin], axis=1), (1, reps))
    return cos_t, sa_t, sb_t


def kernel(x, norm_g, w_in, na_rpb, conv_w, conv_b, conv_ln_g, conv_ln_b, pw_w, pw_b, w_out, final_g):
    batch, seq, _ = x.shape
    depth = w_in.shape[0]
    n = batch * seq
    cos_t, sa_t, sb_t = _rotary_tables(seq)
    h = x.reshape(n, D_MODEL)
    fg = final_g.reshape(1, D_MODEL)
    for l in range(depth):
        conv_params = (conv_w[l], conv_b[l].reshape(1, CONV_CH), conv_ln_g[l].reshape(1, CONV_CH),
                       conv_ln_b[l].reshape(1, CONV_CH), pw_w[l].astype(BF16), pw_b[l].reshape(1, CONV_CH))
        (naq, nak, nav, q1, q4, q16, k1, k4, k16, v1, v4, v16, ycg, gate_a, gate) = _in_proj(
            h, norm_g[l].reshape(1, D_MODEL), w_in[l].astype(BF16), cos_t, sa_t, sb_t, conv_params, batch, seq)
        ya = _na_attention(naq, nak, nav, _na_bias_table(na_rpb[l]), gate_a, batch, seq)
        outs, lses = [], []
        for d, (q, k, v) in zip(DIL_DILATIONS, ((q1, k1, v1), (q4, k4, v4), (q16, k16, v16))):
            groups = (batch * d, seq // d)
            o, lse = _band_attention(q.reshape(*groups, DIL_W), k.reshape(*groups, DIL_W),
                                     v.reshape(*groups, DIL_W))
            if d == 1:
                outs.append(o.reshape(n, DIL_W))
                lses.append(lse.reshape(n, LANES))
            else:
                outs.append(o.reshape(batch, d, seq // d, DIL_W))
                lses.append(lse.reshape(batch, d, seq // d, LANES))
        h = _out_proj(ya, outs, lses, ycg, gate, h, w_out[l].astype(BF16), fg,
                      final_norm=(l == depth - 1), batch=batch, seq=seq)
    return h.reshape(batch, seq, D_MODEL)
```

```python
import functools

import numpy as np
import jax
import jax.numpy as jnp
from jax import lax
from jax.experimental import pallas as pl
from jax.experimental.pallas import tpu as pltpu

D_MODEL = 1024
HEAD_DIM = 64
GRID_W = 64
NA_HEADS = 4
NA_WIN_ROWS = 8
NA_WIN_COLS = 16
DIL_HEADS = 6
DIL_DILATIONS = (1, 4, 16)
DIL_HALF = 64
CONV_CH = 384
CONV_WIDTH = 31
ROPE_THETA = 10000.0
NORM_EPS = 1e-6
NEG_INF = -1e30
NA_W = NA_HEADS * HEAD_DIM
DIL_W = DIL_HEADS * HEAD_DIM
D_MIX = NA_W + DIL_W + CONV_CH
SCALE = HEAD_DIM ** -0.5
LOG2E = 1.4426950408889634
Q_SCALE = SCALE * LOG2E

LANES = 128
SUBLANES = 8
VMEM_LIMIT = 56 * 1024 * 1024
DIL_SLABS = DIL_W // LANES

_C = np.cumsum([0] + [NA_W] * 4 + [DIL_W] * 4 + [CONV_CH] * 3)
(C_AQ, C_AK, C_AV, C_AG, C_BQ, C_BK, C_BV, C_BG, C_CA, C_CB, C_CG, C_END) = [int(c) for c in _C]

BF16 = jnp.bfloat16
F32 = jnp.float32

TOKEN_TILE = 1024
OUT_TILE = 1024
CONV_HALO = 16
CONV_PAD = CONV_WIDTH // 2
CONV_ROWS = 128


def _params(*sem):
    return pltpu.CompilerParams(dimension_semantics=sem, vmem_limit_bytes=VMEM_LIMIT)


def _silu(v):
    return v * (1.0 / (1.0 + jnp.exp(-v)))


def _lane_is_first_head(shape):
    return lax.broadcasted_iota(jnp.int32, shape, len(shape) - 1) < HEAD_DIM


def _in_proj_kernel(xp_ref, x_ref, xn_ref, g_ref, w_ref, cos_ref, sa_ref, sb_ref,
                    cw_ref, cb_ref, lg_ref, lb_ref, pw_ref, pb_ref,
                    naq_ref, nak_ref, nav_ref,
                    q1_ref, q4_ref, q16_ref, k1_ref, k4_ref, k16_ref, v1_ref, v4_ref, v16_ref,
                    ycg_ref, gate_a_ref, gate_b_ref, perm, perm4, ubuf, ybuf, *, seq_blocks):
    tm = x_ref.shape[0]

    def normed(x):
        ms = jnp.mean(x * x, axis=-1, keepdims=True)
        return (x * lax.rsqrt(ms + NORM_EPS) * g_ref[...]).astype(BF16)

    h = normed(x_ref[...])

    def mm(c0, c1):
        return jnp.dot(h, w_ref[:, c0:c1], preferred_element_type=F32)

    cos = cos_ref[...]
    sa = sa_ref[...]
    sb = sb_ref[...]

    def rope(zs):
        up = pltpu.roll(zs, LANES - HEAD_DIM // 2, 1)
        dn = pltpu.roll(zs, HEAD_DIM // 2, 1)
        return zs * cos + up * sa + dn * sb

    def emit_dilated(z, refs, rotary, scale):
        ref1, ref4, ref16 = refs
        for s in range(DIL_SLABS):
            zs = z[:, s * LANES:(s + 1) * LANES]
            if rotary:
                zs = rope(zs)
            if scale != 1.0:
                zs = zs * scale
            perm[s] = zs
            ref1[:, s * LANES:(s + 1) * LANES] = zs.astype(BF16)
        for r1 in range(4):
            for s in range(DIL_SLABS):
                t = perm[s, pl.ds(r1, tm // 4, stride=4), :]
                perm4[s, r1] = t
                ref4[0, r1, :, s * LANES:(s + 1) * LANES] = t.astype(BF16)
        for rho in range(16):
            for s in range(DIL_SLABS):
                ref16[0, rho, :, s * LANES:(s + 1) * LANES] = (
                    perm4[s, rho % 4, pl.ds(rho // 4, tm // 16, stride=4), :].astype(BF16))

    i = pl.program_id(0)
    pos = i % seq_blocks
    h_ext = jnp.concatenate([normed(xp_ref[...]), h, normed(xn_ref[...])], axis=0)
    zc = jnp.dot(h_ext, w_ref[:, C_CA:C_CG], preferred_element_type=F32)
    u = zc[:, 0:CONV_CH] * (1.0 / (1.0 + jnp.exp(-zc[:, CONV_CH:2 * CONV_CH])))
    ubuf[0:CONV_HALO, :] = u[0:CONV_HALO] * jnp.where(pos > 0, 1.0, 0.0).astype(F32)
    ubuf[CONV_HALO:CONV_HALO + tm, :] = u[CONV_HALO:CONV_HALO + tm]
    ubuf[CONV_HALO + tm:, :] = u[CONV_HALO + tm:] * jnp.where(pos < seq_blocks - 1, 1.0, 0.0).astype(F32)

    first_off = CONV_HALO - CONV_PAD
    for c in range(tm // CONV_ROWS):
        r0 = c * CONV_ROWS
        for sl in range(CONV_CH // LANES):
            lanes = slice(sl * LANES, (sl + 1) * LANES)
            y = None
            for s in range(SUBLANES):
                part = None
                for k in range(CONV_WIDTH):
                    off = first_off + k
                    if off % SUBLANES != s:
                        continue
                    win = ubuf[pl.ds(r0 + (off - s), CONV_ROWS + SUBLANES), lanes]
                    term = win * cw_ref[k:k + 1, lanes]
                    part = term if part is None else part + term
                shifted = part[s:s + CONV_ROWS, :]
                y = shifted if y is None else y + shifted
            ybuf[pl.ds(r0, CONV_ROWS), lanes] = y

    za = mm(C_AQ, C_BQ)
    naq_ref[...] = (za[:, 0:NA_W] * Q_SCALE).astype(BF16)
    nak_ref[...] = za[:, NA_W:2 * NA_W].astype(BF16)
    nav_ref[...] = za[:, 2 * NA_W:3 * NA_W].astype(BF16)
    gate_a_ref[...] = _silu(za[:, 3 * NA_W:4 * NA_W]).astype(BF16)

    zqk = mm(C_BQ, C_BV)
    emit_dilated(zqk[:, 0:DIL_W], (q1_ref, q4_ref, q16_ref), True, Q_SCALE)
    emit_dilated(zqk[:, DIL_W:2 * DIL_W], (k1_ref, k4_ref, k16_ref), True, 1.0)

    zvg = mm(C_BV, C_CA)
    emit_dilated(zvg[:, 0:DIL_W], (v1_ref, v4_ref, v16_ref), False, 1.0)
    gate_b_ref[...] = _silu(zvg[:, DIL_W:2 * DIL_W]).astype(BF16)

    y = ybuf[...] + cb_ref[...]
    mu = jnp.mean(y, axis=-1, keepdims=True)
    yc = y - mu
    var = jnp.mean(yc * yc, axis=-1, keepdims=True)
    yn = yc * lax.rsqrt(var + NORM_EPS) * lg_ref[...] + lb_ref[...]
    act = _silu(yn).astype(BF16)
    yc = jnp.dot(act, pw_ref[...], preferred_element_type=F32) + pb_ref[...]
    ycg_ref[...] = (yc * _silu(mm(C_CG, C_END))).astype(BF16)


def _in_proj(x2d, g, w_bf16, cos_t, sa_t, sb_t, conv_params, batch, seq):
    n = x2d.shape[0]
    tm = TOKEN_TILE
    seq_blocks = seq // tm
    halo_per_tile = tm // CONV_HALO
    row = lambda i: (i, 0)
    prev = lambda i: (jnp.maximum(i * halo_per_tile - 1, 0), 0)
    nxt = lambda i: (jnp.minimum((i + 1) * halo_per_tile, n // CONV_HALO - 1), 0)
    tab = lambda i: (i % seq_blocks, 0)
    const = lambda i: (0, 0)
    perm_idx = lambda i: (i // seq_blocks, 0, i % seq_blocks, 0)
    vec = pl.BlockSpec((1, CONV_CH), const)

    def dil_shapes():
        return [jax.ShapeDtypeStruct((n, DIL_W), BF16)] + [
            jax.ShapeDtypeStruct((batch, d, seq // d, DIL_W), BF16) for d in DIL_DILATIONS[1:]]

    def dil_specs():
        return [pl.BlockSpec((tm, DIL_W), row)] + [
            pl.BlockSpec((1, d, tm // d, DIL_W), perm_idx) for d in DIL_DILATIONS[1:]]

    out_shapes = ([jax.ShapeDtypeStruct((n, NA_W), BF16)] * 3 + dil_shapes() + dil_shapes() + dil_shapes()
                  + [jax.ShapeDtypeStruct((n, CONV_CH), BF16), jax.ShapeDtypeStruct((n, NA_W), BF16),
                     jax.ShapeDtypeStruct((n, DIL_W), BF16)])
    out_specs = ([pl.BlockSpec((tm, NA_W), row)] * 3 + dil_specs() + dil_specs() + dil_specs()
                 + [pl.BlockSpec((tm, CONV_CH), row), pl.BlockSpec((tm, NA_W), row),
                    pl.BlockSpec((tm, DIL_W), row)])
    return pl.pallas_call(
        functools.partial(_in_proj_kernel, seq_blocks=seq_blocks),
        out_shape=out_shapes,
        grid=(n // tm,),
        in_specs=[
            pl.BlockSpec((CONV_HALO, D_MODEL), prev),
            pl.BlockSpec((tm, D_MODEL), row),
            pl.BlockSpec((CONV_HALO, D_MODEL), nxt),
            pl.BlockSpec((1, D_MODEL), const),
            pl.BlockSpec((D_MODEL, C_END), const),
            pl.BlockSpec((tm, LANES), tab),
            pl.BlockSpec((tm, LANES), tab),
            pl.BlockSpec((tm, LANES), tab),
            pl.BlockSpec((CONV_WIDTH, CONV_CH), const),
            vec, vec, vec,
            pl.BlockSpec((CONV_CH, CONV_CH), const),
            vec,
        ],
        out_specs=out_specs,
        scratch_shapes=[pltpu.VMEM((DIL_SLABS, tm, LANES), F32),
                        pltpu.VMEM((DIL_SLABS, 4, tm // 4, LANES), F32),
                        pltpu.VMEM((tm + 2 * CONV_HALO, CONV_CH), F32),
                        pltpu.VMEM((tm, CONV_CH), F32)],
        compiler_params=_params("parallel"),
        name="in_proj",
    )(x2d, x2d, x2d, g, w_bf16, cos_t, sa_t, sb_t, *conv_params)


NA_ROWS_PER_STEP = 32
NA_KEYS = NA_WIN_ROWS * GRID_W


def _na_kernel(q_ref, k_ref, v_ref, bias_ref, gate_ref, o_ref, sbuf, ebuf, *, rows):
    blk = pl.program_id(1)
    first = _lane_is_first_head((GRID_W, LANES))
    ones = jnp.ones((NA_KEYS, LANES), BF16)
    n_pairs = NA_W // LANES
    starts, maxes = [], []
    for rr in range(NA_ROWS_PER_STEP):
        r = blk * NA_ROWS_PER_STEP + rr
        r0 = jnp.clip(r - NA_WIN_ROWS // 2, 0, rows - NA_WIN_ROWS)
        delta = r0 - r + (NA_WIN_ROWS - 1)
        start = pl.multiple_of(r0 * GRID_W, GRID_W)
        starts.append(start)
        q_row = q_ref[0, rr * GRID_W:(rr + 1) * GRID_W, :]
        k_win = k_ref[0, pl.ds(start, NA_KEYS), :]
        for p in range(n_pairs):
            qp = q_row[:, p * LANES:(p + 1) * LANES]
            zero = jnp.zeros_like(qp)
            lhs = jnp.concatenate([jnp.where(first, qp, zero), jnp.where(first, zero, qp)], axis=0)
            kp = k_win[:, p * LANES:(p + 1) * LANES]
            s = lax.dot_general(lhs, kp, (((1,), (1,)), ((), ())), preferred_element_type=F32)
            s = s + bias_ref[delta, p]
            sbuf[rr * n_pairs + p] = s
            maxes.append(jnp.max(s, axis=-1, keepdims=True))
    for c in range(NA_ROWS_PER_STEP * n_pairs):
        ebuf[c] = jnp.exp2(sbuf[c] - maxes[c]).astype(BF16)
    for rr in range(NA_ROWS_PER_STEP):
        v_win = v_ref[0, pl.ds(starts[rr], NA_KEYS), :]
        for p in range(n_pairs):
            v_aug = jnp.concatenate([v_win[:, p * LANES:(p + 1) * LANES], ones], axis=1)
            o = jnp.dot(ebuf[rr * n_pairs + p], v_aug, preferred_element_type=F32)
            o = o[:, 0:LANES] * (1.0 / o[:, LANES:2 * LANES])
            out = jnp.where(first, o[0:GRID_W], o[GRID_W:2 * GRID_W])
            tile = (slice(rr * GRID_W, (rr + 1) * GRID_W), slice(p * LANES, (p + 1) * LANES))
            o_ref[0, tile[0], tile[1]] = (out * gate_ref[0, tile[0], tile[1]].astype(F32)).astype(BF16)


def _na_bias_table(rpb):
    cols = np.arange(GRID_W)
    col_start = np.clip(cols - NA_WIN_COLS // 2, 0, GRID_W - NA_WIN_COLS)
    kc = np.arange(GRID_W)
    in_win = (kc[None, :] >= col_start[:, None]) & (kc[None, :] < col_start[:, None] + NA_WIN_COLS)
    col_off = kc[None, :] - cols[:, None] + (NA_WIN_COLS - 1)
    n_off = 2 * NA_WIN_COLS - 1
    onehot = (col_off[None] == np.arange(n_off)[:, None, None]) & in_win[None]
    t = jnp.einsum('hrj,jck->hrck', rpb.astype(F32), jnp.asarray(onehot, F32),
                   precision=lax.Precision.HIGHEST)
    t = jnp.where(jnp.asarray(in_win)[None, None], t * LOG2E, NEG_INF)
    t = jnp.stack([t[:, dlt:dlt + NA_WIN_ROWS] for dlt in range(NA_WIN_ROWS)], axis=0)
    t = t.transpose(0, 1, 3, 2, 4)
    return t.reshape(NA_WIN_ROWS, NA_HEADS // 2, 2 * GRID_W, NA_KEYS)


def _na_attention(q, k, v, bias, gate, batch, seq):
    rows = seq // GRID_W
    tq = NA_ROWS_PER_STEP * GRID_W
    q3 = q.reshape(batch, seq, NA_W)
    k3 = k.reshape(batch, seq, NA_W)
    v3 = v.reshape(batch, seq, NA_W)
    g3 = gate.reshape(batch, seq, NA_W)
    out = pl.pallas_call(
        functools.partial(_na_kernel, rows=rows),
        out_shape=jax.ShapeDtypeStruct((batch, seq, NA_W), BF16),
        grid=(batch, seq // tq),
        in_specs=[
            pl.BlockSpec((1, tq, NA_W), lambda b, i: (b, i, 0)),
            pl.BlockSpec((1, seq, NA_W), lambda b, i: (b, 0, 0)),
            pl.BlockSpec((1, seq, NA_W), lambda b, i: (b, 0, 0)),
            pl.BlockSpec(bias.shape, lambda b, i: (0, 0, 0, 0)),
            pl.BlockSpec((1, tq, NA_W), lambda b, i: (b, i, 0)),
        ],
        out_specs=pl.BlockSpec((1, tq, NA_W), lambda b, i: (b, i, 0)),
        scratch_shapes=[pltpu.VMEM((NA_ROWS_PER_STEP * NA_W // LANES, 2 * GRID_W, NA_KEYS), F32),
                        pltpu.VMEM((NA_ROWS_PER_STEP * NA_W // LANES, 2 * GRID_W, NA_KEYS), BF16)],
        compiler_params=_params("parallel", "arbitrary"),
        name="na_attention",
    )(q3, k3, v3, bias, g3)
    return out.reshape(batch * seq, NA_W)


BAND_CHUNK = 128
BAND_KEYS = BAND_CHUNK + 2 * DIL_HALF
BAND_STEP_TOKENS = 4096
STAT_SUM_LANE = 8


def _band_kernel(q_ref, kp_ref, kc_ref, kn_ref, vp_ref, vc_ref, vn_ref, o_ref, stat_ref, *, length):
    blk = pl.program_id(1)
    BAND_TQ = q_ref.shape[1]
    n_chunks = BAND_TQ // BAND_CHUNK
    first = _lane_is_first_head((BAND_CHUNK, LANES))
    lane = lax.broadcasted_iota(jnp.int32, (BAND_CHUNK, LANES), 1)
    qi = lax.broadcasted_iota(jnp.int32, (BAND_CHUNK, BAND_KEYS), 0)
    kj = lax.broadcasted_iota(jnp.int32, (BAND_CHUNK, BAND_KEYS), 1)
    rel = kj - DIL_HALF - qi
    band = (rel >= -DIL_HALF) & (rel <= DIL_HALF)
    ones = jnp.ones((BAND_KEYS, LANES), BF16)
    for gc in range(q_ref.shape[0] * n_chunks):
        gb, c = divmod(gc, n_chunks)
        a0 = blk * BAND_TQ + c * BAND_CHUNK
        kpos = a0 - DIL_HALF + kj
        ok = band & (kpos >= 0) & (kpos < length)
        mask_bias = jnp.where(ok, 0.0, NEG_INF).astype(F32)
        mask_bias2 = jnp.concatenate([mask_bias, mask_bias], axis=0)
        lo = c * BAND_CHUNK - DIL_HALF
        hi = lo + BAND_KEYS

        def window(prev_ref, cur_ref, next_ref):
            pieces = []
            if lo < 0:
                pieces.append(prev_ref[gb, :, :])
            pieces.append(cur_ref[gb, max(lo, 0):min(hi, BAND_TQ), :])
            if hi > BAND_TQ:
                pieces.append(next_ref[gb, :, :])
            return pieces[0] if len(pieces) == 1 else jnp.concatenate(pieces, axis=0)

        k_win = window(kp_ref, kc_ref, kn_ref)
        v_win = window(vp_ref, vc_ref, vn_ref)
        q_chunk = q_ref[gb, c * BAND_CHUNK:(c + 1) * BAND_CHUNK, :]
        rows = slice(c * BAND_CHUNK, (c + 1) * BAND_CHUNK)
        stat_tile = jnp.zeros((BAND_CHUNK, LANES), F32)
        for p in range(DIL_SLABS):
            qp = q_chunk[:, p * LANES:(p + 1) * LANES]
            zero = jnp.zeros_like(qp)
            lhs = jnp.concatenate([jnp.where(first, qp, zero), jnp.where(first, zero, qp)], axis=0)
            kp = k_win[:, p * LANES:(p + 1) * LANES]
            vp = v_win[:, p * LANES:(p + 1) * LANES]
            s = lax.dot_general(lhs, kp, (((1,), (1,)), ((), ())), preferred_element_type=F32)
            s = s + mask_bias2
            m = jnp.max(s, axis=-1, keepdims=True)
            e = jnp.exp2(s - m).astype(BF16)
            o = jnp.dot(e, jnp.concatenate([vp, ones], axis=1), preferred_element_type=F32)
            l = o[:, LANES:2 * LANES]
            o_ref[gb, rows, p * LANES:(p + 1) * LANES] = jnp.where(
                first, o[0:BAND_CHUNK, 0:LANES], o[BAND_CHUNK:, 0:LANES]).astype(BF16)
            for hh in range(2):
                half = slice(hh * BAND_CHUNK, (hh + 1) * BAND_CHUNK)
                stat_tile = jnp.where(lane == 2 * p + hh, m[half], stat_tile)
                stat_tile = jnp.where(lane == STAT_SUM_LANE + 2 * p + hh, l[half], stat_tile)
        stat_ref[gb, rows, :] = stat_tile


def _band_attention(q, k, v):
    groups, length, _ = q.shape
    tq = min(BAND_STEP_TOKENS, length)
    gb = BAND_STEP_TOKENS // tq
    halo_per_blk = tq // DIL_HALF
    n_halo = length // DIL_HALF
    cur = lambda g, i: (g, i, 0)
    prev = lambda g, i: (g, jnp.maximum(i * halo_per_blk - 1, 0), 0)
    nxt = lambda g, i: (g, jnp.minimum((i + 1) * halo_per_blk, n_halo - 1), 0)
    blk = pl.BlockSpec((gb, tq, DIL_W), cur)
    halo_p = pl.BlockSpec((gb, DIL_HALF, DIL_W), prev)
    halo_n = pl.BlockSpec((gb, DIL_HALF, DIL_W), nxt)
    return pl.pallas_call(
        functools.partial(_band_kernel, length=length),
        out_shape=[jax.ShapeDtypeStruct((groups, length, DIL_W), BF16),
                   jax.ShapeDtypeStruct((groups, length, LANES), F32)],
        grid=(groups // gb, length // tq),
        in_specs=[blk, halo_p, blk, halo_n, halo_p, blk, halo_n],
        out_specs=[pl.BlockSpec((gb, tq, DIL_W), cur), pl.BlockSpec((gb, tq, LANES), cur)],
        compiler_params=_params("parallel", "arbitrary"),
        name="band_attention",
    )(q, k, k, k, v, v, v)


def _out_proj_kernel(ya_ref, o1_ref, o4_ref, o16_ref, l1_ref, l4_ref, l16_ref, ycg_ref, gate_ref,
                     x_ref, w_ref, fg_ref, expand_ref, out_ref, onat, lnat, *, final_norm):
    tm = x_ref.shape[0]
    lane = lax.broadcasted_iota(jnp.int32, (tm, LANES), 1)

    def natural_stats(l_ref, d, slot):
        if d == 1:
            return l_ref[...]
        for rho in range(d):
            lnat[slot, pl.ds(rho, tm // d, stride=d), :] = l_ref[0, rho]
        return lnat[slot]

    stats = (natural_stats(l1_ref, 1, 0), natural_stats(l4_ref, 4, 0), natural_stats(l16_ref, 16, 1))
    m = jnp.maximum(jnp.maximum(stats[0], stats[1]), stats[2])
    scales = [jnp.exp2(st - m) for st in stats]
    denom = None
    for st, sc in zip(stats, scales):
        row_sum = pltpu.roll(st, LANES - STAT_SUM_LANE, 1)
        denom = sc * row_sum if denom is None else denom + sc * row_sum
    inv = 1.0 / denom

    def per_channel(sc):
        w = jnp.where(lane < DIL_HEADS, sc * inv, 0.0)
        hi = w.astype(BF16)
        lo = (w - hi.astype(F32)).astype(BF16)
        return jnp.dot(jnp.concatenate([hi, lo], axis=1), expand_ref[...], preferred_element_type=F32)

    weights = [per_channel(sc) for sc in scales]

    for slot, (d, o_ref) in enumerate(((4, o4_ref), (16, o16_ref))):
        for rho in range(d):
            for s in range(DIL_SLABS):
                onat[slot, s, pl.ds(rho, tm // d, stride=d), :] = (
                    o_ref[0, rho, :, s * LANES:(s + 1) * LANES].astype(F32))

    gate = gate_ref[...].astype(F32)
    pieces = [ya_ref[...]]
    for s in range(DIL_SLABS):
        outs = (o1_ref[:, s * LANES:(s + 1) * LANES].astype(F32), onat[0, s], onat[1, s])
        yb = None
        for w, o in zip(weights, outs):
            wb = w[:, s * LANES:(s + 1) * LANES]
            yb = wb * o if yb is None else yb + wb * o
        pieces.append(yb * gate[:, s * LANES:(s + 1) * LANES])
    mix = jnp.concatenate([piece.astype(BF16) for piece in pieces] + [ycg_ref[...]], axis=1)
    y = x_ref[...] + jnp.dot(mix, w_ref[...], preferred_element_type=F32)
    if final_norm:
        ms = jnp.mean(y * y, axis=-1, keepdims=True)
        y = y * lax.rsqrt(ms + NORM_EPS) * fg_ref[...]
    out_ref[...] = y


def _head_expand_matrix():
    e = np.zeros((2 * LANES, DIL_W), np.float32)
    for h in range(DIL_HEADS):
        e[h, h * HEAD_DIM:(h + 1) * HEAD_DIM] = 1.0
        e[LANES + h, h * HEAD_DIM:(h + 1) * HEAD_DIM] = 1.0
    return jnp.asarray(e, BF16)


def _out_proj(ya, outs, lses, ycg, gate, x2d, w_bf16, final_g, final_norm, batch, seq):
    n = x2d.shape[0]
    tm = OUT_TILE
    seq_blocks = seq // tm
    row = lambda i: (i, 0)
    const = lambda i: (0, 0)
    perm_idx = lambda i: (i // seq_blocks, 0, i % seq_blocks, 0)
    o_specs = [pl.BlockSpec((tm, DIL_W), row)] + [
        pl.BlockSpec((1, d, tm // d, DIL_W), perm_idx) for d in DIL_DILATIONS[1:]]
    l_specs = [pl.BlockSpec((tm, LANES), row)] + [
        pl.BlockSpec((1, d, tm // d, LANES), perm_idx) for d in DIL_DILATIONS[1:]]
    return pl.pallas_call(
        functools.partial(_out_proj_kernel, final_norm=final_norm),
        out_shape=jax.ShapeDtypeStruct((n, D_MODEL), F32),
        grid=(n // tm,),
        in_specs=[pl.BlockSpec((tm, NA_W), row)] + o_specs + l_specs + [
            pl.BlockSpec((tm, CONV_CH), row),
            pl.BlockSpec((tm, DIL_W), row),
            pl.BlockSpec((tm, D_MODEL), row),
            pl.BlockSpec((D_MIX, D_MODEL), const),
            pl.BlockSpec((1, D_MODEL), const),
            pl.BlockSpec((2 * LANES, DIL_W), const),
        ],
        out_specs=pl.BlockSpec((tm, D_MODEL), row),
        scratch_shapes=[pltpu.VMEM((2, DIL_SLABS, tm, LANES), F32),
                        pltpu.VMEM((2, tm, LANES), F32)],
        compiler_params=_params("parallel"),
        name="out_proj",
    )(ya, *outs, *lses, ycg, gate, x2d, w_bf16, final_g, _head_expand_matrix())


def _rotary_tables(seq):
    half = HEAD_DIM // 2
    inv_freq = jnp.power(ROPE_THETA, -jnp.arange(half, dtype=F32) * 2.0 / HEAD_DIM)
    ang = jnp.arange(seq).astype(F32)[:, None] * inv_freq[None, :]
    cos = jnp.cos(ang)
    sin = jnp.sin(ang)
    zero = jnp.zeros_like(sin)
    reps = LANES // HEAD_DIM
    cos_t = jnp.tile(jnp.concatenate([cos, cos], axis=1), (1, reps))
    sa_t = jnp.tile(jnp.concatenate([-sin, zero], axis=1), (1, reps))
    sb_t = jnp.tile(jnp.concatenate([zero, sin], axis=1), (1, reps))
    return cos_t, sa_t, sb_t


def kernel(x, norm_g, w_in, na_rpb, conv_w, conv_b, conv_ln_g, conv_ln_b, pw_w, pw_b, w_out, final_g):
    batch, seq, _ = x.shape
    depth = w_in.shape[0]
    n = batch * seq
    cos_t, sa_t, sb_t = _rotary_tables(seq)
    h = x.reshape(n, D_MODEL)
    fg = final_g.reshape(1, D_MODEL)
    for l in range(depth):
        conv_params = (conv_w[l], conv_b[l].reshape(1, CONV_CH), conv_ln_g[l].reshape(1, CONV_CH),
                       conv_ln_b[l].reshape(1, CONV_CH), pw_w[l].astype(BF16), pw_b[l].reshape(1, CONV_CH))
        (naq, nak, nav, q1, q4, q16, k1, k4, k16, v1, v4, v16, ycg, gate_a, gate) = _in_proj(
            h, norm_g[l].reshape(1, D_MODEL), w_in[l].astype(BF16), cos_t, sa_t, sb_t, conv_params, batch, seq)
        ya = _na_attention(naq, nak, nav, _na_bias_table(na_rpb[l]), gate_a, batch, seq)
        outs, lses = [], []
        for d, (q, k, v) in zip(DIL_DILATIONS, ((q1, k1, v1), (q4, k4, v4), (q16, k16, v16))):
            groups = (batch * d, seq // d)
            o, lse = _band_attention(q.reshape(*groups, DIL_W), k.reshape(*groups, DIL_W),
                                     v.reshape(*groups, DIL_W))
            if d == 1:
                outs.append(o.reshape(n, DIL_W))
                lses.append(lse.reshape(n, LANES))
            else:
                outs.append(o.reshape(batch, d, seq // d, DIL_W))
                lses.append(lse.reshape(batch, d, seq // d, LANES))
        h = _out_proj(ya, outs, lses, ycg, gate, h, w_out[l].astype(BF16), fg,
                      final_norm=(l == depth - 1), batch=batch, seq=seq)
    return h.reshape(batch, seq, D_MODEL)
```

```python
import functools

import numpy as np
import jax
import jax.numpy as jnp
from jax import lax
from jax.experimental import pallas as pl
from jax.experimental.pallas import tpu as pltpu

D_MODEL = 1024
HEAD_DIM = 64
GRID_W = 64
NA_HEADS = 4
NA_WIN_ROWS = 8
NA_WIN_COLS = 16
DIL_HEADS = 6
DIL_DILATIONS = (1, 4, 16)
DIL_HALF = 64
CONV_CH = 384
CONV_WIDTH = 31
ROPE_THETA = 10000.0
NORM_EPS = 1e-6
NEG_INF = -1e30
NA_W = NA_HEADS * HEAD_DIM
DIL_W = DIL_HEADS * HEAD_DIM
D_MIX = NA_W + DIL_W + CONV_CH
SCALE = HEAD_DIM ** -0.5
LOG2E = 1.4426950408889634
Q_SCALE = SCALE * LOG2E

LANES = 128
SUBLANES = 8
VMEM_LIMIT = 56 * 1024 * 1024
DIL_SLABS = DIL_W // LANES

_C = np.cumsum([0] + [NA_W] * 4 + [DIL_W] * 4 + [CONV_CH] * 3)
(C_AQ, C_AK, C_AV, C_AG, C_BQ, C_BK, C_BV, C_BG, C_CA, C_CB, C_CG, C_END) = [int(c) for c in _C]

BF16 = jnp.bfloat16
F32 = jnp.float32

TOKEN_TILE = 1024
OUT_TILE = 1024
CONV_HALO = 16
CONV_PAD = CONV_WIDTH // 2
CONV_ROWS = 256


def _params(*sem):
    return pltpu.CompilerParams(dimension_semantics=sem, vmem_limit_bytes=VMEM_LIMIT)


def _silu(v):
    return v * (1.0 / (1.0 + jnp.exp(-v)))


def _lane_is_first_head(shape):
    return lax.broadcasted_iota(jnp.int32, shape, len(shape) - 1) < HEAD_DIM


def _in_proj_kernel(xp_ref, x_ref, xn_ref, g_ref, w_ref, cos_ref, sa_ref, sb_ref,
                    cw_ref, cb_ref, lg_ref, lb_ref, pw_ref, pb_ref,
                    naq_ref, nak_ref, nav_ref,
                    q1_ref, q4_ref, q16_ref, k1_ref, k4_ref, k16_ref, v1_ref, v4_ref, v16_ref,
                    ycg_ref, gate_a_ref, gate_b_ref, perm, perm4, ubuf, ybuf, *, seq_blocks):
    tm = x_ref.shape[0]

    def normed(x):
        ms = jnp.mean(x * x, axis=-1, keepdims=True)
        return (x * lax.rsqrt(ms + NORM_EPS) * g_ref[...]).astype(BF16)

    h = normed(x_ref[...])

    def mm(c0, c1):
        return jnp.dot(h, w_ref[:, c0:c1], preferred_element_type=F32)

    cos = cos_ref[...]
    sa = sa_ref[...]
    sb = sb_ref[...]

    def rope(zs):
        up = pltpu.roll(zs, LANES - HEAD_DIM // 2, 1)
        dn = pltpu.roll(zs, HEAD_DIM // 2, 1)
        return zs * cos + up * sa + dn * sb

    def emit_dilated(z, refs, rotary, scale):
        ref1, ref4, ref16 = refs
        for s in range(DIL_SLABS):
            zs = z[:, s * LANES:(s + 1) * LANES]
            if rotary:
                zs = rope(zs)
            if scale != 1.0:
                zs = zs * scale
            perm[s] = zs
            ref1[:, s * LANES:(s + 1) * LANES] = zs.astype(BF16)
        for r1 in range(4):
            for s in range(DIL_SLABS):
                t = perm[s, pl.ds(r1, tm // 4, stride=4), :]
                perm4[s, r1] = t
                ref4[0, r1, :, s * LANES:(s + 1) * LANES] = t.astype(BF16)
        for rho in range(16):
            for s in range(DIL_SLABS):
                ref16[0, rho, :, s * LANES:(s + 1) * LANES] = (
                    perm4[s, rho % 4, pl.ds(rho // 4, tm // 16, stride=4), :].astype(BF16))

    i = pl.program_id(0)
    pos = i % seq_blocks
    h_ext = jnp.concatenate([normed(xp_ref[...]), h, normed(xn_ref[...])], axis=0)
    zc = jnp.dot(h_ext, w_ref[:, C_CA:C_CG], preferred_element_type=F32)
    u = zc[:, 0:CONV_CH] * (1.0 / (1.0 + jnp.exp(-zc[:, CONV_CH:2 * CONV_CH])))
    ubuf[0:CONV_HALO, :] = u[0:CONV_HALO] * jnp.where(pos > 0, 1.0, 0.0).astype(F32)
    ubuf[CONV_HALO:CONV_HALO + tm, :] = u[CONV_HALO:CONV_HALO + tm]
    ubuf[CONV_HALO + tm:, :] = u[CONV_HALO + tm:] * jnp.where(pos < seq_blocks - 1, 1.0, 0.0).astype(F32)

    first_off = CONV_HALO - CONV_PAD
    for c in range(tm // CONV_ROWS):
        r0 = c * CONV_ROWS
        for sl in range(CONV_CH // LANES):
            lanes = slice(sl * LANES, (sl + 1) * LANES)
            y = None
            for s in range(SUBLANES):
                part = None
                for k in range(CONV_WIDTH):
                    off = first_off + k
                    if off % SUBLANES != s:
                        continue
                    win = ubuf[pl.ds(r0 + (off - s), CONV_ROWS + SUBLANES), lanes]
                    term = win * cw_ref[k:k + 1, lanes]
                    part = term if part is None else part + term
                shifted = part[s:s + CONV_ROWS, :]
                y = shifted if y is None else y + shifted
            ybuf[pl.ds(r0, CONV_ROWS), lanes] = y

    za = mm(C_AQ, C_BQ)
    naq_ref[...] = (za[:, 0:NA_W] * Q_SCALE).astype(BF16)
    nak_ref[...] = za[:, NA_W:2 * NA_W].astype(BF16)
    nav_ref[...] = za[:, 2 * NA_W:3 * NA_W].astype(BF16)
    gate_a_ref[...] = _silu(za[:, 3 * NA_W:4 * NA_W]).astype(BF16)

    zqk = mm(C_BQ, C_BV)
    emit_dilated(zqk[:, 0:DIL_W], (q1_ref, q4_ref, q16_ref), True, Q_SCALE)
    emit_dilated(zqk[:, DIL_W:2 * DIL_W], (k1_ref, k4_ref, k16_ref), True, 1.0)

    zvg = mm(C_BV, C_CA)
    emit_dilated(zvg[:, 0:DIL_W], (v1_ref, v4_ref, v16_ref), False, 1.0)
    gate_b_ref[...] = _silu(zvg[:, DIL_W:2 * DIL_W]).astype(BF16)

    y = ybuf[...] + cb_ref[...]
    mu = jnp.mean(y, axis=-1, keepdims=True)
    yc = y - mu
    var = jnp.mean(yc * yc, axis=-1, keepdims=True)
    yn = yc * lax.rsqrt(var + NORM_EPS) * lg_ref[...] + lb_ref[...]
    act = _silu(yn).astype(BF16)
    yc = jnp.dot(act, pw_ref[...], preferred_element_type=F32) + pb_ref[...]
    ycg_ref[...] = (yc * _silu(mm(C_CG, C_END))).astype(BF16)


def _in_proj(x2d, g, w_bf16, cos_t, sa_t, sb_t, conv_params, batch, seq):
    n = x2d.shape[0]
    tm = TOKEN_TILE
    seq_blocks = seq // tm
    halo_per_tile = tm // CONV_HALO
    row = lambda i: (i, 0)
    prev = lambda i: (jnp.maximum(i * halo_per_tile - 1, 0), 0)
    nxt = lambda i: (jnp.minimum((i + 1) * halo_per_tile, n // CONV_HALO - 1), 0)
    tab = lambda i: (i % seq_blocks, 0)
    const = lambda i: (0, 0)
    perm_idx = lambda i: (i // seq_blocks, 0, i % seq_blocks, 0)
    vec = pl.BlockSpec((1, CONV_CH), const)

    def dil_shapes():
        return [jax.ShapeDtypeStruct((n, DIL_W), BF16)] + [
            jax.ShapeDtypeStruct((batch, d, seq // d, DIL_W), BF16) for d in DIL_DILATIONS[1:]]

    def dil_specs():
        return [pl.BlockSpec((tm, DIL_W), row)] + [
            pl.BlockSpec((1, d, tm // d, DIL_W), perm_idx) for d in DIL_DILATIONS[1:]]

    out_shapes = ([jax.ShapeDtypeStruct((n, NA_W), BF16)] * 3 + dil_shapes() + dil_shapes() + dil_shapes()
                  + [jax.ShapeDtypeStruct((n, CONV_CH), BF16), jax.ShapeDtypeStruct((n, NA_W), BF16),
                     jax.ShapeDtypeStruct((n, DIL_W), BF16)])
    out_specs = ([pl.BlockSpec((tm, NA_W), row)] * 3 + dil_specs() + dil_specs() + dil_specs()
                 + [pl.BlockSpec((tm, CONV_CH), row), pl.BlockSpec((tm, NA_W), row),
                    pl.BlockSpec((tm, DIL_W), row)])
    return pl.pallas_call(
        functools.partial(_in_proj_kernel, seq_blocks=seq_blocks),
        out_shape=out_shapes,
        grid=(n // tm,),
        in_specs=[
            pl.BlockSpec((CONV_HALO, D_MODEL), prev),
            pl.BlockSpec((tm, D_MODEL), row),
            pl.BlockSpec((CONV_HALO, D_MODEL), nxt),
            pl.BlockSpec((1, D_MODEL), const),
            pl.BlockSpec((D_MODEL, C_END), const),
            pl.BlockSpec((tm, LANES), tab),
            pl.BlockSpec((tm, LANES), tab),
            pl.BlockSpec((tm, LANES), tab),
            pl.BlockSpec((CONV_WIDTH, CONV_CH), const),
            vec, vec, vec,
            pl.BlockSpec((CONV_CH, CONV_CH), const),
            vec,
        ],
        out_specs=out_specs,
        scratch_shapes=[pltpu.VMEM((DIL_SLABS, tm, LANES), F32),
                        pltpu.VMEM((DIL_SLABS, 4, tm // 4, LANES), F32),
                        pltpu.VMEM((tm + 2 * CONV_HALO, CONV_CH), F32),
                        pltpu.VMEM((tm, CONV_CH), F32)],
        compiler_params=_params("parallel"),
        name="in_proj",
    )(x2d, x2d, x2d, g, w_bf16, cos_t, sa_t, sb_t, *conv_params)


NA_ROWS_PER_STEP = 32
NA_KEYS = NA_WIN_ROWS * GRID_W


def _na_kernel(q_ref, k_ref, v_ref, bias_ref, gate_ref, o_ref, sbuf, ebuf, *, rows):
    blk = pl.program_id(1)
    first = _lane_is_first_head((GRID_W, LANES))
    ones = jnp.ones((NA_KEYS, LANES), BF16)
    n_pairs = NA_W // LANES
    starts, maxes = [], []
    for rr in range(NA_ROWS_PER_STEP):
        r = blk * NA_ROWS_PER_STEP + rr
        r0 = jnp.clip(r - NA_WIN_ROWS // 2, 0, rows - NA_WIN_ROWS)
        delta = r0 - r + (NA_WIN_ROWS - 1)
        start = pl.multiple_of(r0 * GRID_W, GRID_W)
        starts.append(start)
        q_row = q_ref[0, rr * GRID_W:(rr + 1) * GRID_W, :]
        k_win = k_ref[0, pl.ds(start, NA_KEYS), :]
        for p in range(n_pairs):
            qp = q_row[:, p * LANES:(p + 1) * LANES]
            zero = jnp.zeros_like(qp)
            lhs = jnp.concatenate([jnp.where(first, qp, zero), jnp.where(first, zero, qp)], axis=0)
            kp = k_win[:, p * LANES:(p + 1) * LANES]
            s = lax.dot_general(lhs, kp, (((1,), (1,)), ((), ())), preferred_element_type=F32)
            s = s + bias_ref[delta, p]
            sbuf[rr * n_pairs + p] = s
            maxes.append(jnp.max(s, axis=-1, keepdims=True))
    for c in range(NA_ROWS_PER_STEP * n_pairs):
        ebuf[c] = jnp.exp2(sbuf[c] - maxes[c]).astype(BF16)
    for rr in range(NA_ROWS_PER_STEP):
        v_win = v_ref[0, pl.ds(starts[rr], NA_KEYS), :]
        for p in range(n_pairs):
            v_aug = jnp.concatenate([v_win[:, p * LANES:(p + 1) * LANES], ones], axis=1)
            o = jnp.dot(ebuf[rr * n_pairs + p], v_aug, preferred_element_type=F32)
            o = o[:, 0:LANES] * (1.0 / o[:, LANES:2 * LANES])
            out = jnp.where(first, o[0:GRID_W], o[GRID_W:2 * GRID_W])
            tile = (slice(rr * GRID_W, (rr + 1) * GRID_W), slice(p * LANES, (p + 1) * LANES))
            o_ref[0, tile[0], tile[1]] = (out * gate_ref[0, tile[0], tile[1]].astype(F32)).astype(BF16)


def _na_bias_table(rpb):
    cols = np.arange(GRID_W)
    col_start = np.clip(cols - NA_WIN_COLS // 2, 0, GRID_W - NA_WIN_COLS)
    kc = np.arange(GRID_W)
    in_win = (kc[None, :] >= col_start[:, None]) & (kc[None, :] < col_start[:, None] + NA_WIN_COLS)
    col_off = kc[None, :] - cols[:, None] + (NA_WIN_COLS - 1)
    n_off = 2 * NA_WIN_COLS - 1
    onehot = (col_off[None] == np.arange(n_off)[:, None, None]) & in_win[None]
    t = jnp.einsum('hrj,jck->hcrk', rpb.astype(F32), jnp.asarray(onehot, F32),
                   precision=lax.Precision.HIGHEST)
    t = jnp.where(jnp.asarray(in_win)[None, :, None, :], t * LOG2E, NEG_INF)
    t = jnp.stack([t[:, :, dlt:dlt + NA_WIN_ROWS] for dlt in range(NA_WIN_ROWS)], axis=0)
    return t.reshape(NA_WIN_ROWS, NA_HEADS // 2, 2 * GRID_W, NA_KEYS)


def _na_attention(q, k, v, bias, gate, batch, seq):
    rows = seq // GRID_W
    tq = NA_ROWS_PER_STEP * GRID_W
    q3 = q.reshape(batch, seq, NA_W)
    k3 = k.reshape(batch, seq, NA_W)
    v3 = v.reshape(batch, seq, NA_W)
    g3 = gate.reshape(batch, seq, NA_W)
    out = pl.pallas_call(
        functools.partial(_na_kernel, rows=rows),
        out_shape=jax.ShapeDtypeStruct((batch, seq, NA_W), BF16),
        grid=(batch, seq // tq),
        in_specs=[
            pl.BlockSpec((1, tq, NA_W), lambda b, i: (b, i, 0)),
            pl.BlockSpec((1, seq, NA_W), lambda b, i: (b, 0, 0)),
            pl.BlockSpec((1, seq, NA_W), lambda b, i: (b, 0, 0)),
            pl.BlockSpec(bias.shape, lambda b, i: (0, 0, 0, 0)),
            pl.BlockSpec((1, tq, NA_W), lambda b, i: (b, i, 0)),
        ],
        out_specs=pl.BlockSpec((1, tq, NA_W), lambda b, i: (b, i, 0)),
        scratch_shapes=[pltpu.VMEM((NA_ROWS_PER_STEP * NA_W // LANES, 2 * GRID_W, NA_KEYS), F32),
                        pltpu.VMEM((NA_ROWS_PER_STEP * NA_W // LANES, 2 * GRID_W, NA_KEYS), BF16)],
        compiler_params=_params("parallel", "arbitrary"),
        name="na_attention",
    )(q3, k3, v3, bias, g3)
    return out.reshape(batch * seq, NA_W)


BAND_CHUNK = 128
BAND_KEYS = BAND_CHUNK + 2 * DIL_HALF
BAND_STEP_TOKENS = 4096
STAT_SUM_LANE = 8


def _band_kernel(q_ref, kp_ref, kc_ref, kn_ref, vp_ref, vc_ref, vn_ref, o_ref, stat_ref, *, length):
    blk = pl.program_id(1)
    BAND_TQ = q_ref.shape[1]
    n_chunks = BAND_TQ // BAND_CHUNK
    first = _lane_is_first_head((BAND_CHUNK, LANES))
    lane = lax.broadcasted_iota(jnp.int32, (BAND_CHUNK, LANES), 1)
    qi = lax.broadcasted_iota(jnp.int32, (BAND_CHUNK, BAND_KEYS), 0)
    kj = lax.broadcasted_iota(jnp.int32, (BAND_CHUNK, BAND_KEYS), 1)
    rel = kj - DIL_HALF - qi
    band = (rel >= -DIL_HALF) & (rel <= DIL_HALF)
    ones = jnp.ones((BAND_KEYS, LANES), BF16)
    for gc in range(q_ref.shape[0] * n_chunks):
        gb, c = divmod(gc, n_chunks)
        a0 = blk * BAND_TQ + c * BAND_CHUNK
        kpos = a0 - DIL_HALF + kj
        ok = band & (kpos >= 0) & (kpos < length)
        mask_bias = jnp.where(ok, 0.0, NEG_INF).astype(F32)
        mask_bias2 = jnp.concatenate([mask_bias, mask_bias], axis=0)
        lo = c * BAND_CHUNK - DIL_HALF
        hi = lo + BAND_KEYS

        def window(prev_ref, cur_ref, next_ref):
            pieces = []
            if lo < 0:
                pieces.append(prev_ref[gb, :, :])
            pieces.append(cur_ref[gb, max(lo, 0):min(hi, BAND_TQ), :])
            if hi > BAND_TQ:
                pieces.append(next_ref[gb, :, :])
            return pieces[0] if len(pieces) == 1 else jnp.concatenate(pieces, axis=0)

        k_win = window(kp_ref, kc_ref, kn_ref)
        v_win = window(vp_ref, vc_ref, vn_ref)
        q_chunk = q_ref[gb, c * BAND_CHUNK:(c + 1) * BAND_CHUNK, :]
        rows = slice(c * BAND_CHUNK, (c + 1) * BAND_CHUNK)
        stat_tile = jnp.zeros((BAND_CHUNK, LANES), F32)
        for p in range(DIL_SLABS):
            qp = q_chunk[:, p * LANES:(p + 1) * LANES]
            zero = jnp.zeros_like(qp)
            lhs = jnp.concatenate([jnp.where(first, qp, zero), jnp.where(first, zero, qp)], axis=0)
            kp = k_win[:, p * LANES:(p + 1) * LANES]
            vp = v_win[:, p * LANES:(p + 1) * LANES]
            s = lax.dot_general(lhs, kp, (((1,), (1,)), ((), ())), preferred_element_type=F32)
            s = s + mask_bias2
            m = jnp.max(s, axis=-1, keepdims=True)
            e = jnp.exp2(s - m).astype(BF16)
            o = jnp.dot(e, jnp.concatenate([vp, ones], axis=1), preferred_element_type=F32)
            l = o[:, LANES:2 * LANES]
            o_ref[gb, rows, p * LANES:(p + 1) * LANES] = jnp.where(
                first, o[0:BAND_CHUNK, 0:LANES], o[BAND_CHUNK:, 0:LANES]).astype(BF16)
            for hh in range(2):
                half = slice(hh * BAND_CHUNK, (hh + 1) * BAND_CHUNK)
                stat_tile = jnp.where(lane == 2 * p + hh, m[half], stat_tile)
                stat_tile = jnp.where(lane == STAT_SUM_LANE + 2 * p + hh, l[half], stat_tile)
        stat_ref[gb, rows, :] = stat_tile


def _band_attention(q, k, v):
    groups, length, _ = q.shape
    tq = min(BAND_STEP_TOKENS, length)
    gb = BAND_STEP_TOKENS // tq
    halo_per_blk = tq // DIL_HALF
    n_halo = length // DIL_HALF
    cur = lambda g, i: (g, i, 0)
    prev = lambda g, i: (g, jnp.maximum(i * halo_per_blk - 1, 0), 0)
    nxt = lambda g, i: (g, jnp.minimum((i + 1) * halo_per_blk, n_halo - 1), 0)
    blk = pl.BlockSpec((gb, tq, DIL_W), cur)
    halo_p = pl.BlockSpec((gb, DIL_HALF, DIL_W), prev)
    halo_n = pl.BlockSpec((gb, DIL_HALF, DIL_W), nxt)
    return pl.pallas_call(
        functools.partial(_band_kernel, length=length),
        out_shape=[jax.ShapeDtypeStruct((groups, length, DIL_W), BF16),
                   jax.ShapeDtypeStruct((groups, length, LANES), F32)],
        grid=(groups // gb, length // tq),
        in_specs=[blk, halo_p, blk, halo_n, halo_p, blk, halo_n],
        out_specs=[pl.BlockSpec((gb, tq, DIL_W), cur), pl.BlockSpec((gb, tq, LANES), cur)],
        compiler_params=_params("parallel", "arbitrary"),
        name="band_attention",
    )(q, k, k, k, v, v, v)


def _out_proj_kernel(ya_ref, o1_ref, o4_ref, o16_ref, l1_ref, l4_ref, l16_ref, ycg_ref, gate_ref,
                     x_ref, w_ref, fg_ref, expand_ref, out_ref, onat, lnat, *, final_norm):
    tm = x_ref.shape[0]
    lane = lax.broadcasted_iota(jnp.int32, (tm, LANES), 1)

    def natural_stats(l_ref, d, slot):
        if d == 1:
            return l_ref[...]
        for rho in range(d):
            lnat[slot, pl.ds(rho, tm // d, stride=d), :] = l_ref[0, rho]
        return lnat[slot]

    stats = (natural_stats(l1_ref, 1, 0), natural_stats(l4_ref, 4, 0), natural_stats(l16_ref, 16, 1))
    m = jnp.maximum(jnp.maximum(stats[0], stats[1]), stats[2])
    scales = [jnp.exp2(st - m) for st in stats]
    denom = None
    for st, sc in zip(stats, scales):
        row_sum = pltpu.roll(st, LANES - STAT_SUM_LANE, 1)
        denom = sc * row_sum if denom is None else denom + sc * row_sum
    inv = 1.0 / denom

    def per_channel(sc):
        w = jnp.where(lane < DIL_HEADS, sc * inv, 0.0)
        hi = w.astype(BF16)
        lo = (w - hi.astype(F32)).astype(BF16)
        return jnp.dot(jnp.concatenate([hi, lo], axis=1), expand_ref[...], preferred_element_type=F32)

    weights = [per_channel(sc) for sc in scales]

    for slot, (d, o_ref) in enumerate(((4, o4_ref), (16, o16_ref))):
        for rho in range(d):
            for s in range(DIL_SLABS):
                onat[slot, s, pl.ds(rho, tm // d, stride=d), :] = (
                    o_ref[0, rho, :, s * LANES:(s + 1) * LANES].astype(F32))

    gate = gate_ref[...].astype(F32)
    pieces = [ya_ref[...]]
    for s in range(DIL_SLABS):
        outs = (o1_ref[:, s * LANES:(s + 1) * LANES].astype(F32), onat[0, s], onat[1, s])
        yb = None
        for w, o in zip(weights, outs):
            wb = w[:, s * LANES:(s + 1) * LANES]
            yb = wb * o if yb is None else yb + wb * o
        pieces.append(yb * gate[:, s * LANES:(s + 1) * LANES])
    mix = jnp.concatenate([piece.astype(BF16) for piece in pieces] + [ycg_ref[...]], axis=1)
    y = x_ref[...] + jnp.dot(mix, w_ref[...], preferred_element_type=F32)
    if final_norm:
        ms = jnp.mean(y * y, axis=-1, keepdims=True)
        y = y * lax.rsqrt(ms + NORM_EPS) * fg_ref[...]
    out_ref[...] = y


def _head_expand_matrix():
    e = np.zeros((2 * LANES, DIL_W), np.float32)
    for h in range(DIL_HEADS):
        e[h, h * HEAD_DIM:(h + 1) * HEAD_DIM] = 1.0
        e[LANES + h, h * HEAD_DIM:(h + 1) * HEAD_DIM] = 1.0
    return jnp.asarray(e, BF16)


def _out_proj(ya, outs, lses, ycg, gate, x2d, w_bf16, final_g, final_norm, batch, seq):
    n = x2d.shape[0]
    tm = OUT_TILE
    seq_blocks = seq // tm
    row = lambda i: (i, 0)
    const = lambda i: (0, 0)
    perm_idx = lambda i: (i // seq_blocks, 0, i % seq_blocks, 0)
    o_specs = [pl.BlockSpec((tm, DIL_W), row)] + [
        pl.BlockSpec((1, d, tm // d, DIL_W), perm_idx) for d in DIL_DILATIONS[1:]]
    l_specs = [pl.BlockSpec((tm, LANES), row)] + [
        pl.BlockSpec((1, d, tm // d, LANES), perm_idx) for d in DIL_DILATIONS[1:]]
    return pl.pallas_call(
        functools.partial(_out_proj_kernel, final_norm=final_norm),
        out_shape=jax.ShapeDtypeStruct((n, D_MODEL), F32),
        grid=(n // tm,),
        in_specs=[pl.BlockSpec((tm, NA_W), row)] + o_specs + l_specs + [
            pl.BlockSpec((tm, CONV_CH), row),
            pl.BlockSpec((tm, DIL_W), row),
            pl.BlockSpec((tm, D_MODEL), row),
            pl.BlockSpec((D_MIX, D_MODEL), const),
            pl.BlockSpec((1, D_MODEL), const),
            pl.BlockSpec((2 * LANES, DIL_W), const),
        ],
        out_specs=pl.BlockSpec((tm, D_MODEL), row),
        scratch_shapes=[pltpu.VMEM((2, DIL_SLABS, tm, LANES), F32),
                        pltpu.VMEM((2, tm, LANES), F32)],
        compiler_params=_params("parallel"),
        name="out_proj",
    )(ya, *outs, *lses, ycg, gate, x2d, w_bf16, final_g, _head_expand_matrix())


def _rotary_tables(seq):
    half = HEAD_DIM // 2
    inv_freq = jnp.power(ROPE_THETA, -jnp.arange(half, dtype=F32) * 2.0 / HEAD_DIM)
    ang = jnp.arange(seq).astype(F32)[:, None] * inv_freq[None, :]
    cos = jnp.cos(ang)
    sin = jnp.sin(ang)
    zero = jnp.zeros_like(sin)
    reps = LANES // HEAD_DIM
    cos_t = jnp.tile(jnp.concatenate([cos, cos], axis=1), (1, reps))
    sa_t = jnp.tile(jnp.concatenate([-sin, zero], axis=1), (1, reps))
    sb_t = jnp.tile(jnp.concatenate([zero, sin], axis=1), (1, reps))
    return cos_t, sa_t, sb_t


def kernel(x, norm_g, w_in, na_rpb, conv_w, conv_b, conv_ln_g, conv_ln_b, pw_w, pw_b, w_out, final_g):
    batch, seq, _ = x.shape
    depth = w_in.shape[0]
    n = batch * seq
    cos_t, sa_t, sb_t = _rotary_tables(seq)
    h = x.reshape(n, D_MODEL)
    fg = final_g.reshape(1, D_MODEL)
    for l in range(depth):
        conv_params = (conv_w[l], conv_b[l].reshape(1, CONV_CH), conv_ln_g[l].reshape(1, CONV_CH),
                       conv_ln_b[l].reshape(1, CONV_CH), pw_w[l].astype(BF16), pw_b[l].reshape(1, CONV_CH))
        (naq, nak, nav, q1, q4, q16, k1, k4, k16, v1, v4, v16, ycg, gate_a, gate) = _in_proj(
            h, norm_g[l].reshape(1, D_MODEL), w_in[l].astype(BF16), cos_t, sa_t, sb_t, conv_params, batch, seq)
        ya = _na_attention(naq, nak, nav, _na_bias_table(na_rpb[l]), gate_a, batch, seq)
        outs, lses = [], []
        for d, (q, k, v) in zip(DIL_DILATIONS, ((q1, k1, v1), (q4, k4, v4), (q16, k16, v16))):
            groups = (batch * d, seq // d)
            o, lse = _band_attention(q.reshape(*groups, DIL_W), k.reshape(*groups, DIL_W),
                                     v.reshape(*groups, DIL_W))
            if d == 1:
                outs.append(o.reshape(n, DIL_W))
                lses.append(lse.reshape(n, LANES))
            else:
                outs.append(o.reshape(batch, d, seq // d, DIL_W))
                lses.append(lse.reshape(batch, d, seq // d, LANES))
        h = _out_proj(ya, outs, lses, ycg, gate, h, w_out[l].astype(BF16), fg,
                      final_norm=(l == depth - 1), batch=batch, seq=seq)
    return h.reshape(batch, seq, D_MODEL)
```

```python
import functools

import numpy as np
import jax
import jax.numpy as jnp
from jax import lax
from jax.experimental import pallas as pl
from jax.experimental.pallas import tpu as pltpu

D_MODEL = 1024
HEAD_DIM = 64
GRID_W = 64
NA_HEADS = 4
NA_WIN_ROWS = 8
NA_WIN_COLS = 16
DIL_HEADS = 6
DIL_DILATIONS = (1, 4, 16)
DIL_HALF = 64
CONV_CH = 384
CONV_WIDTH = 31
ROPE_THETA = 10000.0
NORM_EPS = 1e-6
NEG_INF = -1e30
NA_W = NA_HEADS * HEAD_DIM
DIL_W = DIL_HEADS * HEAD_DIM
D_MIX = NA_W + DIL_W + CONV_CH
SCALE = HEAD_DIM ** -0.5
LOG2E = 1.4426950408889634
Q_SCALE = SCALE * LOG2E

LANES = 128
SUBLANES = 8
VMEM_LIMIT = 56 * 1024 * 1024
DIL_SLABS = DIL_W // LANES

_C = np.cumsum([0] + [NA_W] * 4 + [DIL_W] * 4 + [CONV_CH] * 3)
(C_AQ, C_AK, C_AV, C_AG, C_BQ, C_BK, C_BV, C_BG, C_CA, C_CB, C_CG, C_END) = [int(c) for c in _C]

BF16 = jnp.bfloat16
F32 = jnp.float32

TOKEN_TILE = 1024
OUT_TILE = 1024
CONV_HALO = 16
CONV_PAD = CONV_WIDTH // 2
CONV_ROWS = 128


def _params(*sem):
    return pltpu.CompilerParams(dimension_semantics=sem, vmem_limit_bytes=VMEM_LIMIT)


def _silu(v):
    return v * (1.0 / (1.0 + jnp.exp(-v)))


def _lane_is_first_head(shape):
    return lax.broadcasted_iota(jnp.int32, shape, len(shape) - 1) < HEAD_DIM


def _in_proj_kernel(xp_ref, x_ref, xn_ref, g_ref, w_ref, cos_ref, sa_ref, sb_ref,
                    cw_ref, cb_ref, lg_ref, lb_ref, pw_ref, pb_ref,
                    naq_ref, nak_ref, nav_ref,
                    q1_ref, q4_ref, q16_ref, k1_ref, k4_ref, k16_ref, v1_ref, v4_ref, v16_ref,
                    ycg_ref, gate_a_ref, gate_b_ref, perm, perm4, ubuf, ybuf, *, seq_blocks):
    tm = x_ref.shape[0]

    def normed(x):
        ms = jnp.mean(x * x, axis=-1, keepdims=True)
        return (x * lax.rsqrt(ms + NORM_EPS) * g_ref[...]).astype(BF16)

    h = normed(x_ref[...])

    def mm(c0, c1):
        return jnp.dot(h, w_ref[:, c0:c1], preferred_element_type=F32)

    cos = cos_ref[...]
    sa = sa_ref[...]
    sb = sb_ref[...]

    def rope(zs):
        up = pltpu.roll(zs, LANES - HEAD_DIM // 2, 1)
        dn = pltpu.roll(zs, HEAD_DIM // 2, 1)
        return zs * cos + up * sa + dn * sb

    def emit_dilated(z, refs, rotary, scale):
        ref1, ref4, ref16 = refs
        for s in range(DIL_SLABS):
            zs = z[:, s * LANES:(s + 1) * LANES]
            if rotary:
                zs = rope(zs)
            if scale != 1.0:
                zs = zs * scale
            perm[s] = zs
            ref1[:, s * LANES:(s + 1) * LANES] = zs.astype(BF16)
        for r1 in range(4):
            for s in range(DIL_SLABS):
                t = perm[s, pl.ds(r1, tm // 4, stride=4), :]
                perm4[s, r1] = t
                ref4[0, r1, :, s * LANES:(s + 1) * LANES] = t.astype(BF16)
        for rho in range(16):
            for s in range(DIL_SLABS):
                ref16[0, rho, :, s * LANES:(s + 1) * LANES] = (
                    perm4[s, rho % 4, pl.ds(rho // 4, tm // 16, stride=4), :].astype(BF16))

    i = pl.program_id(0)
    pos = i % seq_blocks
    h_ext = jnp.concatenate([normed(xp_ref[...]), h, normed(xn_ref[...])], axis=0)
    zc = jnp.dot(h_ext, w_ref[:, C_CA:C_CG], preferred_element_type=F32)
    u = zc[:, 0:CONV_CH] * (1.0 / (1.0 + jnp.exp(-zc[:, CONV_CH:2 * CONV_CH])))
    ubuf[0:CONV_HALO, :] = u[0:CONV_HALO] * jnp.where(pos > 0, 1.0, 0.0).astype(F32)
    ubuf[CONV_HALO:CONV_HALO + tm, :] = u[CONV_HALO:CONV_HALO + tm]
    ubuf[CONV_HALO + tm:, :] = u[CONV_HALO + tm:] * jnp.where(pos < seq_blocks - 1, 1.0, 0.0).astype(F32)

    first_off = CONV_HALO - CONV_PAD
    for c in range(tm // CONV_ROWS):
        r0 = c * CONV_ROWS
        for sl in range(CONV_CH // LANES):
            lanes = slice(sl * LANES, (sl + 1) * LANES)
            y = None
            for s in range(SUBLANES):
                part = None
                for k in range(CONV_WIDTH):
                    off = first_off + k
                    if off % SUBLANES != s:
                        continue
                    win = ubuf[pl.ds(r0 + (off - s), CONV_ROWS + SUBLANES), lanes]
                    term = win * cw_ref[k:k + 1, lanes]
                    part = term if part is None else part + term
                shifted = part[s:s + CONV_ROWS, :]
                y = shifted if y is None else y + shifted
            ybuf[pl.ds(r0, CONV_ROWS), lanes] = y

    za = mm(C_AQ, C_BQ)
    naq_ref[...] = (za[:, 0:NA_W] * Q_SCALE).astype(BF16)
    nak_ref[...] = za[:, NA_W:2 * NA_W].astype(BF16)
    nav_ref[...] = za[:, 2 * NA_W:3 * NA_W].astype(BF16)
    gate_a_ref[...] = _silu(za[:, 3 * NA_W:4 * NA_W]).astype(BF16)

    zqk = mm(C_BQ, C_BV)
    emit_dilated(zqk[:, 0:DIL_W], (q1_ref, q4_ref, q16_ref), True, Q_SCALE)
    emit_dilated(zqk[:, DIL_W:2 * DIL_W], (k1_ref, k4_ref, k16_ref), True, 1.0)

    zvg = mm(C_BV, C_CA)
    emit_dilated(zvg[:, 0:DIL_W], (v1_ref, v4_ref, v16_ref), False, 1.0)
    gate_b_ref[...] = _silu(zvg[:, DIL_W:2 * DIL_W]).astype(BF16)

    y = ybuf[...] + cb_ref[...]
    mu = jnp.mean(y, axis=-1, keepdims=True)
    yc = y - mu
    var = jnp.mean(yc * yc, axis=-1, keepdims=True)
    yn = yc * lax.rsqrt(var + NORM_EPS) * lg_ref[...] + lb_ref[...]
    act = _silu(yn).astype(BF16)
    yc = jnp.dot(act, pw_ref[...], preferred_element_type=F32) + pb_ref[...]
    ycg_ref[...] = (yc * _silu(mm(C_CG, C_END))).astype(BF16)


def _in_proj(x2d, g, w_bf16, cos_t, sa_t, sb_t, conv_params, batch, seq):
    n = x2d.shape[0]
    tm = TOKEN_TILE
    seq_blocks = seq // tm
    halo_per_tile = tm // CONV_HALO
    row = lambda i: (i, 0)
    prev = lambda i: (jnp.maximum(i * halo_per_tile - 1, 0), 0)
    nxt = lambda i: (jnp.minimum((i + 1) * halo_per_tile, n // CONV_HALO - 1), 0)
    tab = lambda i: (i % seq_blocks, 0)
    const = lambda i: (0, 0)
    perm_idx = lambda i: (i // seq_blocks, 0, i % seq_blocks, 0)
    vec = pl.BlockSpec((1, CONV_CH), const)

    def dil_shapes():
        return [jax.ShapeDtypeStruct((n, DIL_W), BF16)] + [
            jax.ShapeDtypeStruct((batch, d, seq // d, DIL_W), BF16) for d in DIL_DILATIONS[1:]]

    def dil_specs():
        return [pl.BlockSpec((tm, DIL_W), row)] + [
            pl.BlockSpec((1, d, tm // d, DIL_W), perm_idx) for d in DIL_DILATIONS[1:]]

    out_shapes = ([jax.ShapeDtypeStruct((n, NA_W), BF16)] * 3 + dil_shapes() + dil_shapes() + dil_shapes()
                  + [jax.ShapeDtypeStruct((n, CONV_CH), BF16), jax.ShapeDtypeStruct((n, NA_W), BF16),
                     jax.ShapeDtypeStruct((n, DIL_W), BF16)])
    out_specs = ([pl.BlockSpec((tm, NA_W), row)] * 3 + dil_specs() + dil_specs() + dil_specs()
                 + [pl.BlockSpec((tm, CONV_CH), row), pl.BlockSpec((tm, NA_W), row),
                    pl.BlockSpec((tm, DIL_W), row)])
    return pl.pallas_call(
        functools.partial(_in_proj_kernel, seq_blocks=seq_blocks),
        out_shape=out_shapes,
        grid=(n // tm,),
        in_specs=[
            pl.BlockSpec((CONV_HALO, D_MODEL), prev),
            pl.BlockSpec((tm, D_MODEL), row),
            pl.BlockSpec((CONV_HALO, D_MODEL), nxt),
            pl.BlockSpec((1, D_MODEL), const),
            pl.BlockSpec((D_MODEL, C_END), const),
            pl.BlockSpec((tm, LANES), tab),
            pl.BlockSpec((tm, LANES), tab),
            pl.BlockSpec((tm, LANES), tab),
            pl.BlockSpec((CONV_WIDTH, CONV_CH), const),
            vec, vec, vec,
            pl.BlockSpec((CONV_CH, CONV_CH), const),
            vec,
        ],
        out_specs=out_specs,
        scratch_shapes=[pltpu.VMEM((DIL_SLABS, tm, LANES), F32),
                        pltpu.VMEM((DIL_SLABS, 4, tm // 4, LANES), F32),
                        pltpu.VMEM((tm + 2 * CONV_HALO, CONV_CH), F32),
                        pltpu.VMEM((tm, CONV_CH), F32)],
        compiler_params=_params("parallel"),
        name="in_proj",
    )(x2d, x2d, x2d, g, w_bf16, cos_t, sa_t, sb_t, *conv_params)


NA_ROWS_PER_STEP = 32
NA_KEYS = NA_WIN_ROWS * GRID_W


def _na_kernel(q_ref, k_ref, v_ref, bias_ref, gate_ref, o_ref, sbuf, ebuf, *, rows):
    blk = pl.program_id(1)
    first = _lane_is_first_head((GRID_W, LANES))
    ones = jnp.ones((NA_KEYS, LANES), BF16)
    n_pairs = NA_W // LANES
    starts, maxes = [], []
    for rr in range(NA_ROWS_PER_STEP):
        r = blk * NA_ROWS_PER_STEP + rr
        r0 = jnp.clip(r - NA_WIN_ROWS // 2, 0, rows - NA_WIN_ROWS)
        delta = r0 - r + (NA_WIN_ROWS - 1)
        start = pl.multiple_of(r0 * GRID_W, GRID_W)
        starts.append(start)
        q_row = q_ref[0, rr * GRID_W:(rr + 1) * GRID_W, :]
        k_win = k_ref[0, pl.ds(start, NA_KEYS), :]
        for p in range(n_pairs):
            qp = q_row[:, p * LANES:(p + 1) * LANES]
            zero = jnp.zeros_like(qp)
            lhs = jnp.concatenate([jnp.where(first, qp, zero), jnp.where(first, zero, qp)], axis=0)
            kp = k_win[:, p * LANES:(p + 1) * LANES]
            s = lax.dot_general(lhs, kp, (((1,), (1,)), ((), ())), preferred_element_type=F32)
            bias = jnp.concatenate([bias_ref[p, delta + 2 * j] for j in range(NA_WIN_ROWS // 2)], axis=1)
            s = s + bias
            sbuf[rr * n_pairs + p] = s
            maxes.append(jnp.max(s, axis=-1, keepdims=True))
    for c in range(NA_ROWS_PER_STEP * n_pairs):
        ebuf[c] = jnp.exp2(sbuf[c] - maxes[c]).astype(BF16)
    for rr in range(NA_ROWS_PER_STEP):
        v_win = v_ref[0, pl.ds(starts[rr], NA_KEYS), :]
        for p in range(n_pairs):
            v_aug = jnp.concatenate([v_win[:, p * LANES:(p + 1) * LANES], ones], axis=1)
            o = jnp.dot(ebuf[rr * n_pairs + p], v_aug, preferred_element_type=F32)
            o = o[:, 0:LANES] * (1.0 / o[:, LANES:2 * LANES])
            out = jnp.where(first, o[0:GRID_W], o[GRID_W:2 * GRID_W])
            tile = (slice(rr * GRID_W, (rr + 1) * GRID_W), slice(p * LANES, (p + 1) * LANES))
            o_ref[0, tile[0], tile[1]] = (out * gate_ref[0, tile[0], tile[1]].astype(F32)).astype(BF16)


def _na_bias_table(rpb):
    cols = np.arange(GRID_W)
    col_start = np.clip(cols - NA_WIN_COLS // 2, 0, GRID_W - NA_WIN_COLS)
    kc = np.arange(GRID_W)
    in_win = (kc[None, :] >= col_start[:, None]) & (kc[None, :] < col_start[:, None] + NA_WIN_COLS)
    col_off = kc[None, :] - cols[:, None] + (NA_WIN_COLS - 1)
    n_off = 2 * NA_WIN_COLS - 1
    onehot = (col_off[None] == np.arange(n_off)[:, None, None]) & in_win[None]
    t = jnp.einsum('hrj,jck->hrck', rpb.astype(F32), jnp.asarray(onehot, F32),
                   precision=lax.Precision.HIGHEST)
    t = jnp.where(jnp.asarray(in_win)[None, None], t * LOG2E, NEG_INF)
    n_ro = 2 * NA_WIN_ROWS - 2
    t = jnp.concatenate([t[:, 0:n_ro], t[:, 1:n_ro + 1]], axis=-1)
    t = t.reshape(NA_HEADS // 2, 2, n_ro, GRID_W, 2 * GRID_W).transpose(0, 2, 1, 3, 4)
    return t.reshape(NA_HEADS // 2, n_ro, 2 * GRID_W, 2 * GRID_W)


def _na_attention(q, k, v, bias, gate, batch, seq):
    rows = seq // GRID_W
    tq = NA_ROWS_PER_STEP * GRID_W
    q3 = q.reshape(batch, seq, NA_W)
    k3 = k.reshape(batch, seq, NA_W)
    v3 = v.reshape(batch, seq, NA_W)
    g3 = gate.reshape(batch, seq, NA_W)
    out = pl.pallas_call(
        functools.partial(_na_kernel, rows=rows),
        out_shape=jax.ShapeDtypeStruct((batch, seq, NA_W), BF16),
        grid=(batch, seq // tq),
        in_specs=[
            pl.BlockSpec((1, tq, NA_W), lambda b, i: (b, i, 0)),
            pl.BlockSpec((1, seq, NA_W), lambda b, i: (b, 0, 0)),
            pl.BlockSpec((1, seq, NA_W), lambda b, i: (b, 0, 0)),
            pl.BlockSpec(bias.shape, lambda b, i: (0, 0, 0, 0)),
            pl.BlockSpec((1, tq, NA_W), lambda b, i: (b, i, 0)),
        ],
        out_specs=pl.BlockSpec((1, tq, NA_W), lambda b, i: (b, i, 0)),
        scratch_shapes=[pltpu.VMEM((NA_ROWS_PER_STEP * NA_W // LANES, 2 * GRID_W, NA_KEYS), F32),
                        pltpu.VMEM((NA_ROWS_PER_STEP * NA_W // LANES, 2 * GRID_W, NA_KEYS), BF16)],
        compiler_params=_params("parallel", "arbitrary"),
        name="na_attention",
    )(q3, k3, v3, bias, g3)
    return out.reshape(batch * seq, NA_W)


BAND_CHUNK = 128
BAND_KEYS = BAND_CHUNK + 2 * DIL_HALF
BAND_STEP_TOKENS = 4096
STAT_SUM_LANE = 8


def _band_kernel(q_ref, kp_ref, kc_ref, kn_ref, vp_ref, vc_ref, vn_ref, o_ref, stat_ref, *, length):
    blk = pl.program_id(1)
    BAND_TQ = q_ref.shape[1]
    n_chunks = BAND_TQ // BAND_CHUNK
    first = _lane_is_first_head((BAND_CHUNK, LANES))
    lane = lax.broadcasted_iota(jnp.int32, (BAND_CHUNK, LANES), 1)
    qi = lax.broadcasted_iota(jnp.int32, (BAND_CHUNK, BAND_KEYS), 0)
    kj = lax.broadcasted_iota(jnp.int32, (BAND_CHUNK, BAND_KEYS), 1)
    rel = kj - DIL_HALF - qi
    band = (rel >= -DIL_HALF) & (rel <= DIL_HALF)
    ones = jnp.ones((BAND_KEYS, LANES), BF16)
    for gc in range(q_ref.shape[0] * n_chunks):
        gb, c = divmod(gc, n_chunks)
        a0 = blk * BAND_TQ + c * BAND_CHUNK
        kpos = a0 - DIL_HALF + kj
        ok = band & (kpos >= 0) & (kpos < length)
        mask_bias = jnp.where(ok, 0.0, NEG_INF).astype(F32)
        mask_bias2 = jnp.concatenate([mask_bias, mask_bias], axis=0)
        lo = c * BAND_CHUNK - DIL_HALF
        hi = lo + BAND_KEYS

        def window(prev_ref, cur_ref, next_ref):
            pieces = []
            if lo < 0:
                pieces.append(prev_ref[gb, :, :])
            pieces.append(cur_ref[gb, max(lo, 0):min(hi, BAND_TQ), :])
            if hi > BAND_TQ:
                pieces.append(next_ref[gb, :, :])
            return pieces[0] if len(pieces) == 1 else jnp.concatenate(pieces, axis=0)

        k_win = window(kp_ref, kc_ref, kn_ref)
        v_win = window(vp_ref, vc_ref, vn_ref)
        q_chunk = q_ref[gb, c * BAND_CHUNK:(c + 1) * BAND_CHUNK, :]
        rows = slice(c * BAND_CHUNK, (c + 1) * BAND_CHUNK)
        stat_tile = jnp.zeros((BAND_CHUNK, LANES), F32)
        for p in range(DIL_SLABS):
            qp = q_chunk[:, p * LANES:(p + 1) * LANES]
            zero = jnp.zeros_like(qp)
            lhs = jnp.concatenate([jnp.where(first, qp, zero), jnp.where(first, zero, qp)], axis=0)
            kp = k_win[:, p * LANES:(p + 1) * LANES]
            vp = v_win[:, p * LANES:(p + 1) * LANES]
            s = lax.dot_general(lhs, kp, (((1,), (1,)), ((), ())), preferred_element_type=F32)
            s = s + mask_bias2
            m = jnp.max(s, axis=-1, keepdims=True)
            e = jnp.exp2(s - m).astype(BF16)
            o = jnp.dot(e, jnp.concatenate([vp, ones], axis=1), preferred_element_type=F32)
            l = o[:, LANES:2 * LANES]
            o_ref[gb, rows, p * LANES:(p + 1) * LANES] = jnp.where(
                first, o[0:BAND_CHUNK, 0:LANES], o[BAND_CHUNK:, 0:LANES]).astype(BF16)
            for hh in range(2):
                half = slice(hh * BAND_CHUNK, (hh + 1) * BAND_CHUNK)
                stat_tile = jnp.where(lane == 2 * p + hh, m[half], stat_tile)
                stat_tile = jnp.where(lane == STAT_SUM_LANE + 2 * p + hh, l[half], stat_tile)
        stat_ref[gb, rows, :] = stat_tile


def _band_attention(q, k, v):
    groups, length, _ = q.shape
    tq = min(BAND_STEP_TOKENS, length)
    gb = BAND_STEP_TOKENS // tq
    halo_per_blk = tq // DIL_HALF
    n_halo = length // DIL_HALF
    cur = lambda g, i: (g, i, 0)
    prev = lambda g, i: (g, jnp.maximum(i * halo_per_blk - 1, 0), 0)
    nxt = lambda g, i: (g, jnp.minimum((i + 1) * halo_per_blk, n_halo - 1), 0)
    blk = pl.BlockSpec((gb, tq, DIL_W), cur)
    halo_p = pl.BlockSpec((gb, DIL_HALF, DIL_W), prev)
    halo_n = pl.BlockSpec((gb, DIL_HALF, DIL_W), nxt)
    return pl.pallas_call(
        functools.partial(_band_kernel, length=length),
        out_shape=[jax.ShapeDtypeStruct((groups, length, DIL_W), BF16),
                   jax.ShapeDtypeStruct((groups, length, LANES), F32)],
        grid=(groups // gb, length // tq),
        in_specs=[blk, halo_p, blk, halo_n, halo_p, blk, halo_n],
        out_specs=[pl.BlockSpec((gb, tq, DIL_W), cur), pl.BlockSpec((gb, tq, LANES), cur)],
        compiler_params=_params("parallel", "arbitrary"),
        name="band_attention",
    )(q, k, k, k, v, v, v)


def _out_proj_kernel(ya_ref, o1_ref, o4_ref, o16_ref, l1_ref, l4_ref, l16_ref, ycg_ref, gate_ref,
                     x_ref, w_ref, fg_ref, expand_ref, out_ref, onat, lnat, *, final_norm):
    tm = x_ref.shape[0]
    lane = lax.broadcasted_iota(jnp.int32, (tm, LANES), 1)

    def natural_stats(l_ref, d, slot):
        if d == 1:
            return l_ref[...]
        for rho in range(d):
            lnat[slot, pl.ds(rho, tm // d, stride=d), :] = l_ref[0, rho]
        return lnat[slot]

    stats = (natural_stats(l1_ref, 1, 0), natural_stats(l4_ref, 4, 0), natural_stats(l16_ref, 16, 1))
    m = jnp.maximum(jnp.maximum(stats[0], stats[1]), stats[2])
    scales = [jnp.exp2(st - m) for st in stats]
    denom = None
    for st, sc in zip(stats, scales):
        row_sum = pltpu.roll(st, LANES - STAT_SUM_LANE, 1)
        denom = sc * row_sum if denom is None else denom + sc * row_sum
    inv = 1.0 / denom

    def per_channel(sc):
        w = jnp.where(lane < DIL_HEADS, sc * inv, 0.0)
        hi = w.astype(BF16)
        lo = (w - hi.astype(F32)).astype(BF16)
        return jnp.dot(jnp.concatenate([hi, lo], axis=1), expand_ref[...], preferred_element_type=F32)

    weights = [per_channel(sc) for sc in scales]

    for slot, (d, o_ref) in enumerate(((4, o4_ref), (16, o16_ref))):
        for rho in range(d):
            for s in range(DIL_SLABS):
                onat[slot, s, pl.ds(rho, tm // d, stride=d), :] = (
                    o_ref[0, rho, :, s * LANES:(s + 1) * LANES].astype(F32))

    gate = gate_ref[...].astype(F32)
    pieces = [ya_ref[...]]
    for s in range(DIL_SLABS):
        outs = (o1_ref[:, s * LANES:(s + 1) * LANES].astype(F32), onat[0, s], onat[1, s])
        yb = None
        for w, o in zip(weights, outs):
            wb = w[:, s * LANES:(s + 1) * LANES]
            yb = wb * o if yb is None else yb + wb * o
        pieces.append(yb * gate[:, s * LANES:(s + 1) * LANES])
    mix = jnp.concatenate([piece.astype(BF16) for piece in pieces] + [ycg_ref[...]], axis=1)
    y = x_ref[...] + jnp.dot(mix, w_ref[...], preferred_element_type=F32)
    if final_norm:
        ms = jnp.mean(y * y, axis=-1, keepdims=True)
        y = y * lax.rsqrt(ms + NORM_EPS) * fg_ref[...]
    out_ref[...] = y


def _head_expand_matrix():
    e = np.zeros((2 * LANES, DIL_W), np.float32)
    for h in range(DIL_HEADS):
        e[h, h * HEAD_DIM:(h + 1) * HEAD_DIM] = 1.0
        e[LANES + h, h * HEAD_DIM:(h + 1) * HEAD_DIM] = 1.0
    return jnp.asarray(e, BF16)


def _out_proj(ya, outs, lses, ycg, gate, x2d, w_bf16, final_g, final_norm, batch, seq):
    n = x2d.shape[0]
    tm = OUT_TILE
    seq_blocks = seq // tm
    row = lambda i: (i, 0)
    const = lambda i: (0, 0)
    perm_idx = lambda i: (i // seq_blocks, 0, i % seq_blocks, 0)
    o_specs = [pl.BlockSpec((tm, DIL_W), row)] + [
        pl.BlockSpec((1, d, tm // d, DIL_W), perm_idx) for d in DIL_DILATIONS[1:]]
    l_specs = [pl.BlockSpec((tm, LANES), row)] + [
        pl.BlockSpec((1, d, tm // d, LANES), perm_idx) for d in DIL_DILATIONS[1:]]
    return pl.pallas_call(
        functools.partial(_out_proj_kernel, final_norm=final_norm),
        out_shape=jax.ShapeDtypeStruct((n, D_MODEL), F32),
        grid=(n // tm,),
        in_specs=[pl.BlockSpec((tm, NA_W), row)] + o_specs + l_specs + [
            pl.BlockSpec((tm, CONV_CH), row),
            pl.BlockSpec((tm, DIL_W), row),
            pl.BlockSpec((tm, D_MODEL), row),
            pl.BlockSpec((D_MIX, D_MODEL), const),
            pl.BlockSpec((1, D_MODEL), const),
            pl.BlockSpec((2 * LANES, DIL_W), const),
        ],
        out_specs=pl.BlockSpec((tm, D_MODEL), row),
        scratch_shapes=[pltpu.VMEM((2, DIL_SLABS, tm, LANES), F32),
                        pltpu.VMEM((2, tm, LANES), F32)],
        compiler_params=_params("parallel"),
        name="out_proj",
    )(ya, *outs, *lses, ycg, gate, x2d, w_bf16, final_g, _head_expand_matrix())


def _rotary_tables(seq):
    half = HEAD_DIM // 2
    inv_freq = jnp.power(ROPE_THETA, -jnp.arange(half, dtype=F32) * 2.0 / HEAD_DIM)
    ang = jnp.arange(seq).astype(F32)[:, None] * inv_freq[None, :]
    cos = jnp.cos(ang)
    sin = jnp.sin(ang)
    zero = jnp.zeros_like(sin)
    reps = LANES // HEAD_DIM
    cos_t = jnp.tile(jnp.concatenate([cos, cos], axis=1), (1, reps))
    sa_t = jnp.tile(jnp.concatenate([-sin, zero], axis=1), (1, reps))
    sb_t = jnp.tile(jnp.concatenate([zero, sin], axis=1), (1, reps))
    return cos_t, sa_t, sb_t


def kernel(x, norm_g, w_in, na_rpb, conv_w, conv_b, conv_ln_g, conv_ln_b, pw_w, pw_b, w_out, final_g):
    batch, seq, _ = x.shape
    depth = w_in.shape[0]
    n = batch * seq
    cos_t, sa_t, sb_t = _rotary_tables(seq)
    h = x.reshape(n, D_MODEL)
    fg = final_g.reshape(1, D_MODEL)
    for l in range(depth):
        conv_params = (conv_w[l], conv_b[l].reshape(1, CONV_CH), conv_ln_g[l].reshape(1, CONV_CH),
                       conv_ln_b[l].reshape(1, CONV_CH), pw_w[l].astype(BF16), pw_b[l].reshape(1, CONV_CH))
        (naq, nak, nav, q1, q4, q16, k1, k4, k16, v1, v4, v16, ycg, gate_a, gate) = _in_proj(
            h, norm_g[l].reshape(1, D_MODEL), w_in[l].astype(BF16), cos_t, sa_t, sb_t, conv_params, batch, seq)
        ya = _na_attention(naq, nak, nav, _na_bias_table(na_rpb[l]), gate_a, batch, seq)
        outs, lses = [], []
        for d, (q, k, v) in zip(DIL_DILATIONS, ((q1, k1, v1), (q4, k4, v4), (q16, k16, v16))):
            groups = (batch * d, seq // d)
            o, lse = _band_attention(q.reshape(*groups, DIL_W), k.reshape(*groups, DIL_W),
                                     v.reshape(*groups, DIL_W))
            if d == 1:
                outs.append(o.reshape(n, DIL_W))
                lses.append(lse.reshape(n, LANES))
            else:
                outs.append(o.reshape(batch, d, seq // d, DIL_W))
                lses.append(lse.reshape(batch, d, seq // d, LANES))
        h = _out_proj(ya, outs, lses, ycg, gate, h, w_out[l].astype(BF16), fg,
                      final_norm=(l == depth - 1), batch=batch, seq=seq)
    return h.reshape(batch, seq, D_MODEL)
```

```python
import functools

import numpy as np
import jax
import jax.numpy as jnp
from jax import lax
from jax.experimental import pallas as pl
from jax.experimental.pallas import tpu as pltpu

D_MODEL = 1024
HEAD_DIM = 64
GRID_W = 64
NA_HEADS = 4
NA_WIN_ROWS = 8
NA_WIN_COLS = 16
DIL_HEADS = 6
DIL_DILATIONS = (1, 4, 16)
DIL_HALF = 64
CONV_CH = 384
CONV_WIDTH = 31
ROPE_THETA = 10000.0
NORM_EPS = 1e-6
NEG_INF = -1e30
NA_W = NA_HEADS * HEAD_DIM
DIL_W = DIL_HEADS * HEAD_DIM
D_MIX = NA_W + DIL_W + CONV_CH
SCALE = HEAD_DIM ** -0.5
LOG2E = 1.4426950408889634
Q_SCALE = SCALE * LOG2E

LANES = 128
SUBLANES = 8
VMEM_LIMIT = 56 * 1024 * 1024
DIL_SLABS = DIL_W // LANES

_C = np.cumsum([0] + [NA_W] * 4 + [DIL_W] * 4 + [CONV_CH] * 3)
(C_AQ, C_AK, C_AV, C_AG, C_BQ, C_BK, C_BV, C_BG, C_CA, C_CB, C_CG, C_END) = [int(c) for c in _C]

BF16 = jnp.bfloat16
F32 = jnp.float32

TOKEN_TILE = 1024
OUT_TILE = 1024
CONV_HALO = 16
CONV_PAD = CONV_WIDTH // 2
CONV_ROWS = 128


def _params(*sem):
    return pltpu.CompilerParams(dimension_semantics=sem, vmem_limit_bytes=VMEM_LIMIT)


def _silu(v):
    return v * (1.0 / (1.0 + jnp.exp(-v)))


def _lane_is_first_head(shape):
    return lax.broadcasted_iota(jnp.int32, shape, len(shape) - 1) < HEAD_DIM


def _in_proj_kernel(xp_ref, x_ref, xn_ref, g_ref, w_ref, cos_ref, sa_ref, sb_ref,
                    cw_ref, cb_ref, lg_ref, lb_ref, pw_ref, pb_ref,
                    naq_ref, nak_ref, nav_ref,
                    q1_ref, q4_ref, q16_ref, k1_ref, k4_ref, k16_ref, v1_ref, v4_ref, v16_ref,
                    ycg_ref, gate_a_ref, gate_b_ref, perm, perm4, ubuf, ybuf, *, seq_blocks):
    tm = x_ref.shape[0]

    def normed(x):
        ms = jnp.mean(x * x, axis=-1, keepdims=True)
        return (x * lax.rsqrt(ms + NORM_EPS) * g_ref[...]).astype(BF16)

    h = normed(x_ref[...])

    def mm(c0, c1):
        return jnp.dot(h, w_ref[:, c0:c1], preferred_element_type=F32)

    cos = cos_ref[...]
    sa = sa_ref[...]
    sb = sb_ref[...]

    def rope(zs):
        up = pltpu.roll(zs, LANES - HEAD_DIM // 2, 1)
        dn = pltpu.roll(zs, HEAD_DIM // 2, 1)
        return zs * cos + up * sa + dn * sb

    def emit_dilated(z, refs, rotary, scale):
        ref1, ref4, ref16 = refs
        for s in range(DIL_SLABS):
            zs = z[:, s * LANES:(s + 1) * LANES]
            if rotary:
                zs = rope(zs)
            if scale != 1.0:
                zs = zs * scale
            perm[s] = zs
            ref1[:, s * LANES:(s + 1) * LANES] = zs.astype(BF16)
        for r1 in range(4):
            for s in range(DIL_SLABS):
                t = perm[s, pl.ds(r1, tm // 4, stride=4), :]
                perm4[s, r1] = t
                ref4[0, r1, :, s * LANES:(s + 1) * LANES] = t.astype(BF16)
        for rho in range(16):
            for s in range(DIL_SLABS):
                ref16[0, rho, :, s * LANES:(s + 1) * LANES] = (
                    perm4[s, rho % 4, pl.ds(rho // 4, tm // 16, stride=4), :].astype(BF16))

    i = pl.program_id(0)
    pos = i % seq_blocks
    h_ext = jnp.concatenate([normed(xp_ref[...]), h, normed(xn_ref[...])], axis=0)
    zc = jnp.dot(h_ext, w_ref[:, C_CA:C_CG], preferred_element_type=F32)
    u = zc[:, 0:CONV_CH] * (1.0 / (1.0 + jnp.exp(-zc[:, CONV_CH:2 * CONV_CH])))
    ubuf[0:CONV_HALO, :] = u[0:CONV_HALO] * jnp.where(pos > 0, 1.0, 0.0).astype(F32)
    ubuf[CONV_HALO:CONV_HALO + tm, :] = u[CONV_HALO:CONV_HALO + tm]
    ubuf[CONV_HALO + tm:, :] = u[CONV_HALO + tm:] * jnp.where(pos < seq_blocks - 1, 1.0, 0.0).astype(F32)

    first_off = CONV_HALO - CONV_PAD
    for c in range(tm // CONV_ROWS):
        r0 = c * CONV_ROWS
        for sl in range(CONV_CH // LANES):
            lanes = slice(sl * LANES, (sl + 1) * LANES)
            y = None
            for s in range(SUBLANES):
                part = None
                for k in range(CONV_WIDTH):
                    off = first_off + k
                    if off % SUBLANES != s:
                        continue
                    win = ubuf[pl.ds(r0 + (off - s), CONV_ROWS + SUBLANES), lanes]
                    term = win * cw_ref[k:k + 1, lanes]
                    part = term if part is None else part + term
                shifted = part[s:s + CONV_ROWS, :]
                y = shifted if y is None else y + shifted
            ybuf[pl.ds(r0, CONV_ROWS), lanes] = y

    za = mm(C_AQ, C_BQ)
    naq_ref[...] = (za[:, 0:NA_W] * Q_SCALE).astype(BF16)
    nak_ref[...] = za[:, NA_W:2 * NA_W].astype(BF16)
    nav_ref[...] = za[:, 2 * NA_W:3 * NA_W].astype(BF16)
    gate_a_ref[...] = _silu(za[:, 3 * NA_W:4 * NA_W]).astype(BF16)

    zqk = mm(C_BQ, C_BV)
    emit_dilated(zqk[:, 0:DIL_W], (q1_ref, q4_ref, q16_ref), True, Q_SCALE)
    emit_dilated(zqk[:, DIL_W:2 * DIL_W], (k1_ref, k4_ref, k16_ref), True, 1.0)

    zvg = mm(C_BV, C_CA)
    emit_dilated(zvg[:, 0:DIL_W], (v1_ref, v4_ref, v16_ref), False, 1.0)
    gate_b_ref[...] = _silu(zvg[:, DIL_W:2 * DIL_W]).astype(BF16)

    y = ybuf[...] + cb_ref[...]
    mu = jnp.mean(y, axis=-1, keepdims=True)
    yc = y - mu
    var = jnp.mean(yc * yc, axis=-1, keepdims=True)
    yn = yc * lax.rsqrt(var + NORM_EPS) * lg_ref[...] + lb_ref[...]
    act = _silu(yn).astype(BF16)
    yc = jnp.dot(act, pw_ref[...], preferred_element_type=F32) + pb_ref[...]
    ycg_ref[...] = (yc * _silu(mm(C_CG, C_END))).astype(BF16)


def _in_proj(x2d, g, w_bf16, layer, cos_t, sa_t, sb_t, conv_params, batch, seq):
    n = x2d.shape[0]
    tm = TOKEN_TILE
    seq_blocks = seq // tm
    halo_per_tile = tm // CONV_HALO
    row = lambda i: (i, 0)
    prev = lambda i: (jnp.maximum(i * halo_per_tile - 1, 0), 0)
    nxt = lambda i: (jnp.minimum((i + 1) * halo_per_tile, n // CONV_HALO - 1), 0)
    tab = lambda i: (i % seq_blocks, 0)
    const = lambda i: (0, 0)
    perm_idx = lambda i: (i // seq_blocks, 0, i % seq_blocks, 0)
    vec = pl.BlockSpec((1, CONV_CH), const)

    def dil_shapes():
        return [jax.ShapeDtypeStruct((n, DIL_W), BF16)] + [
            jax.ShapeDtypeStruct((batch, d, seq // d, DIL_W), BF16) for d in DIL_DILATIONS[1:]]

    def dil_specs():
        return [pl.BlockSpec((tm, DIL_W), row)] + [
            pl.BlockSpec((1, d, tm // d, DIL_W), perm_idx) for d in DIL_DILATIONS[1:]]

    out_shapes = ([jax.ShapeDtypeStruct((n, NA_W), BF16)] * 3 + dil_shapes() + dil_shapes() + dil_shapes()
                  + [jax.ShapeDtypeStruct((n, CONV_CH), BF16), jax.ShapeDtypeStruct((n, NA_W), BF16),
                     jax.ShapeDtypeStruct((n, DIL_W), BF16)])
    out_specs = ([pl.BlockSpec((tm, NA_W), row)] * 3 + dil_specs() + dil_specs() + dil_specs()
                 + [pl.BlockSpec((tm, CONV_CH), row), pl.BlockSpec((tm, NA_W), row),
                    pl.BlockSpec((tm, DIL_W), row)])
    return pl.pallas_call(
        functools.partial(_in_proj_kernel, seq_blocks=seq_blocks),
        out_shape=out_shapes,
        grid=(n // tm,),
        in_specs=[
            pl.BlockSpec((CONV_HALO, D_MODEL), prev),
            pl.BlockSpec((tm, D_MODEL), row),
            pl.BlockSpec((CONV_HALO, D_MODEL), nxt),
            pl.BlockSpec((1, D_MODEL), const),
            pl.BlockSpec((None, D_MODEL, C_END), lambda i: (layer, 0, 0)),
            pl.BlockSpec((tm, LANES), tab),
            pl.BlockSpec((tm, LANES), tab),
            pl.BlockSpec((tm, LANES), tab),
            pl.BlockSpec((CONV_WIDTH, CONV_CH), const),
            vec, vec, vec,
            pl.BlockSpec((None, CONV_CH, CONV_CH), lambda i: (layer, 0, 0)),
            vec,
        ],
        out_specs=out_specs,
        scratch_shapes=[pltpu.VMEM((DIL_SLABS, tm, LANES), F32),
                        pltpu.VMEM((DIL_SLABS, 4, tm // 4, LANES), F32),
                        pltpu.VMEM((tm + 2 * CONV_HALO, CONV_CH), F32),
                        pltpu.VMEM((tm, CONV_CH), F32)],
        compiler_params=_params("parallel"),
        name="in_proj",
    )(x2d, x2d, x2d, g, w_bf16, cos_t, sa_t, sb_t, *conv_params)


NA_ROWS_PER_STEP = 32
NA_KEYS = NA_WIN_ROWS * GRID_W


def _na_kernel(q_ref, k_ref, v_ref, bias_ref, gate_ref, o_ref, sbuf, ebuf, *, rows):
    blk = pl.program_id(1)
    first = _lane_is_first_head((GRID_W, LANES))
    ones = jnp.ones((NA_KEYS, LANES), BF16)
    n_pairs = NA_W // LANES
    starts, maxes = [], []
    for rr in range(NA_ROWS_PER_STEP):
        r = blk * NA_ROWS_PER_STEP + rr
        r0 = jnp.clip(r - NA_WIN_ROWS // 2, 0, rows - NA_WIN_ROWS)
        delta = r0 - r + (NA_WIN_ROWS - 1)
        start = pl.multiple_of(r0 * GRID_W, GRID_W)
        starts.append(start)
        q_row = q_ref[0, rr * GRID_W:(rr + 1) * GRID_W, :]
        k_win = k_ref[0, pl.ds(start, NA_KEYS), :]
        for p in range(n_pairs):
            qp = q_row[:, p * LANES:(p + 1) * LANES]
            zero = jnp.zeros_like(qp)
            lhs = jnp.concatenate([jnp.where(first, qp, zero), jnp.where(first, zero, qp)], axis=0)
            kp = k_win[:, p * LANES:(p + 1) * LANES]
            s = lax.dot_general(lhs, kp, (((1,), (1,)), ((), ())), preferred_element_type=F32)
            bias = jnp.concatenate([bias_ref[p, delta + 2 * j] for j in range(NA_WIN_ROWS // 2)], axis=1)
            s = s + bias
            sbuf[rr * n_pairs + p] = s
            maxes.append(jnp.max(s, axis=-1, keepdims=True))
    for c in range(NA_ROWS_PER_STEP * n_pairs):
        ebuf[c] = jnp.exp2(sbuf[c] - maxes[c]).astype(BF16)
    for rr in range(NA_ROWS_PER_STEP):
        v_win = v_ref[0, pl.ds(starts[rr], NA_KEYS), :]
        for p in range(n_pairs):
            v_aug = jnp.concatenate([v_win[:, p * LANES:(p + 1) * LANES], ones], axis=1)
            o = jnp.dot(ebuf[rr * n_pairs + p], v_aug, preferred_element_type=F32)
            o = o[:, 0:LANES] * (1.0 / o[:, LANES:2 * LANES])
            out = jnp.where(first, o[0:GRID_W], o[GRID_W:2 * GRID_W])
            tile = (slice(rr * GRID_W, (rr + 1) * GRID_W), slice(p * LANES, (p + 1) * LANES))
            o_ref[0, tile[0], tile[1]] = (out * gate_ref[0, tile[0], tile[1]].astype(F32)).astype(BF16)


def _na_bias_table(rpb):
    cols = np.arange(GRID_W)
    col_start = np.clip(cols - NA_WIN_COLS // 2, 0, GRID_W - NA_WIN_COLS)
    kc = np.arange(GRID_W)
    in_win = (kc[None, :] >= col_start[:, None]) & (kc[None, :] < col_start[:, None] + NA_WIN_COLS)
    col_off = kc[None, :] - cols[:, None] + (NA_WIN_COLS - 1)
    n_off = 2 * NA_WIN_COLS - 1
    onehot = (col_off[None] == np.arange(n_off)[:, None, None]) & in_win[None]
    rpb_pairs = rpb.astype(F32).reshape(NA_HEADS // 2, 2, 2 * NA_WIN_ROWS - 1, n_off)
    t = jnp.einsum('phrj,jck->prhck', rpb_pairs, jnp.asarray(onehot, F32),
                   precision=lax.Precision.HIGHEST)
    t = jnp.where(jnp.asarray(in_win)[None, None, None], t * LOG2E, NEG_INF)
    n_ro = 2 * NA_WIN_ROWS - 2
    t = jnp.concatenate([t[:, 0:n_ro], t[:, 1:n_ro + 1]], axis=-1)
    return t.reshape(NA_HEADS // 2, n_ro, 2 * GRID_W, 2 * GRID_W)


def _na_attention(q, k, v, bias, gate, batch, seq):
    rows = seq // GRID_W
    tq = NA_ROWS_PER_STEP * GRID_W
    q3 = q.reshape(batch, seq, NA_W)
    k3 = k.reshape(batch, seq, NA_W)
    v3 = v.reshape(batch, seq, NA_W)
    g3 = gate.reshape(batch, seq, NA_W)
    out = pl.pallas_call(
        functools.partial(_na_kernel, rows=rows),
        out_shape=jax.ShapeDtypeStruct((batch, seq, NA_W), BF16),
        grid=(batch, seq // tq),
        in_specs=[
            pl.BlockSpec((1, tq, NA_W), lambda b, i: (b, i, 0)),
            pl.BlockSpec((1, seq, NA_W), lambda b, i: (b, 0, 0)),
            pl.BlockSpec((1, seq, NA_W), lambda b, i: (b, 0, 0)),
            pl.BlockSpec(bias.shape, lambda b, i: (0, 0, 0, 0)),
            pl.BlockSpec((1, tq, NA_W), lambda b, i: (b, i, 0)),
        ],
        out_specs=pl.BlockSpec((1, tq, NA_W), lambda b, i: (b, i, 0)),
        scratch_shapes=[pltpu.VMEM((NA_ROWS_PER_STEP * NA_W // LANES, 2 * GRID_W, NA_KEYS), F32),
                        pltpu.VMEM((NA_ROWS_PER_STEP * NA_W // LANES, 2 * GRID_W, NA_KEYS), BF16)],
        compiler_params=_params("parallel", "arbitrary"),
        name="na_attention",
    )(q3, k3, v3, bias, g3)
    return out.reshape(batch * seq, NA_W)


BAND_CHUNK = 128
BAND_KEYS = BAND_CHUNK + 2 * DIL_HALF
BAND_STEP_TOKENS = 4096
STAT_SUM_LANE = 8


def _band_kernel(q_ref, kp_ref, kc_ref, kn_ref, vp_ref, vc_ref, vn_ref, o_ref, stat_ref, *, length):
    blk = pl.program_id(1)
    BAND_TQ = q_ref.shape[1]
    n_chunks = BAND_TQ // BAND_CHUNK
    first = _lane_is_first_head((BAND_CHUNK, LANES))
    lane = lax.broadcasted_iota(jnp.int32, (BAND_CHUNK, LANES), 1)
    qi = lax.broadcasted_iota(jnp.int32, (BAND_CHUNK, BAND_KEYS), 0)
    kj = lax.broadcasted_iota(jnp.int32, (BAND_CHUNK, BAND_KEYS), 1)
    rel = kj - DIL_HALF - qi
    band = (rel >= -DIL_HALF) & (rel <= DIL_HALF)
    ones = jnp.ones((BAND_KEYS, LANES), BF16)
    for gc in range(q_ref.shape[0] * n_chunks):
        gb, c = divmod(gc, n_chunks)
        a0 = blk * BAND_TQ + c * BAND_CHUNK
        kpos = a0 - DIL_HALF + kj
        ok = band & (kpos >= 0) & (kpos < length)
        mask_bias = jnp.where(ok, 0.0, NEG_INF).astype(F32)
        mask_bias2 = jnp.concatenate([mask_bias, mask_bias], axis=0)
        lo = c * BAND_CHUNK - DIL_HALF
        hi = lo + BAND_KEYS

        def window(prev_ref, cur_ref, next_ref):
            pieces = []
            if lo < 0:
                pieces.append(prev_ref[gb, :, :])
            pieces.append(cur_ref[gb, max(lo, 0):min(hi, BAND_TQ), :])
            if hi > BAND_TQ:
                pieces.append(next_ref[gb, :, :])
            return pieces[0] if len(pieces) == 1 else jnp.concatenate(pieces, axis=0)

        k_win = window(kp_ref, kc_ref, kn_ref)
        v_win = window(vp_ref, vc_ref, vn_ref)
        q_chunk = q_ref[gb, c * BAND_CHUNK:(c + 1) * BAND_CHUNK, :]
        rows = slice(c * BAND_CHUNK, (c + 1) * BAND_CHUNK)
        stat_tile = jnp.zeros((BAND_CHUNK, LANES), F32)
        for p in range(DIL_SLABS):
            qp = q_chunk[:, p * LANES:(p + 1) * LANES]
            zero = jnp.zeros_like(qp)
            lhs = jnp.concatenate([jnp.where(first, qp, zero), jnp.where(first, zero, qp)], axis=0)
            kp = k_win[:, p * LANES:(p + 1) * LANES]
            vp = v_win[:, p * LANES:(p + 1) * LANES]
            s = lax.dot_general(lhs, kp, (((1,), (1,)), ((), ())), preferred_element_type=F32)
            s = s + mask_bias2
            m = jnp.max(s, axis=-1, keepdims=True)
            e = jnp.exp2(s - m).astype(BF16)
            o = jnp.dot(e, jnp.concatenate([vp, ones], axis=1), preferred_element_type=F32)
            l = o[:, LANES:2 * LANES]
            o_ref[gb, rows, p * LANES:(p + 1) * LANES] = jnp.where(
                first, o[0:BAND_CHUNK, 0:LANES], o[BAND_CHUNK:, 0:LANES]).astype(BF16)
            for hh in range(2):
                half = slice(hh * BAND_CHUNK, (hh + 1) * BAND_CHUNK)
                stat_tile = jnp.where(lane == 2 * p + hh, m[half], stat_tile)
                stat_tile = jnp.where(lane == STAT_SUM_LANE + 2 * p + hh, l[half], stat_tile)
        stat_ref[gb, rows, :] = stat_tile


def _band_attention(q, k, v):
    groups, length, _ = q.shape
    tq = min(BAND_STEP_TOKENS, length)
    gb = BAND_STEP_TOKENS // tq
    halo_per_blk = tq // DIL_HALF
    n_halo = length // DIL_HALF
    cur = lambda g, i: (g, i, 0)
    prev = lambda g, i: (g, jnp.maximum(i * halo_per_blk - 1, 0), 0)
    nxt = lambda g, i: (g, jnp.minimum((i + 1) * halo_per_blk, n_halo - 1), 0)
    blk = pl.BlockSpec((gb, tq, DIL_W), cur)
    halo_p = pl.BlockSpec((gb, DIL_HALF, DIL_W), prev)
    halo_n = pl.BlockSpec((gb, DIL_HALF, DIL_W), nxt)
    return pl.pallas_call(
        functools.partial(_band_kernel, length=length),
        out_shape=[jax.ShapeDtypeStruct((groups, length, DIL_W), BF16),
                   jax.ShapeDtypeStruct((groups, length, LANES), F32)],
        grid=(groups // gb, length // tq),
        in_specs=[blk, halo_p, blk, halo_n, halo_p, blk, halo_n],
        out_specs=[pl.BlockSpec((gb, tq, DIL_W), cur), pl.BlockSpec((gb, tq, LANES), cur)],
        compiler_params=_params("parallel", "arbitrary"),
        name="band_attention",
    )(q, k, k, k, v, v, v)


def _out_proj_kernel(ya_ref, o1_ref, o4_ref, o16_ref, l1_ref, l4_ref, l16_ref, ycg_ref, gate_ref,
                     x_ref, w_ref, fg_ref, expand_ref, out_ref, onat, lnat, *, final_norm):
    tm = x_ref.shape[0]
    lane = lax.broadcasted_iota(jnp.int32, (tm, LANES), 1)

    def natural_stats(l_ref, d, slot):
        if d == 1:
            return l_ref[...]
        for rho in range(d):
            lnat[slot, pl.ds(rho, tm // d, stride=d), :] = l_ref[0, rho]
        return lnat[slot]

    stats = (natural_stats(l1_ref, 1, 0), natural_stats(l4_ref, 4, 0), natural_stats(l16_ref, 16, 1))
    m = jnp.maximum(jnp.maximum(stats[0], stats[1]), stats[2])
    scales = [jnp.exp2(st - m) for st in stats]
    denom = None
    for st, sc in zip(stats, scales):
        row_sum = pltpu.roll(st, LANES - STAT_SUM_LANE, 1)
        denom = sc * row_sum if denom is None else denom + sc * row_sum
    inv = 1.0 / denom

    def per_channel(sc):
        w = jnp.where(lane < DIL_HEADS, sc * inv, 0.0)
        hi = w.astype(BF16)
        lo = (w - hi.astype(F32)).astype(BF16)
        return jnp.dot(jnp.concatenate([hi, lo], axis=1), expand_ref[...], preferred_element_type=F32)

    weights = [per_channel(sc) for sc in scales]

    for slot, (d, o_ref) in enumerate(((4, o4_ref), (16, o16_ref))):
        for rho in range(d):
            for s in range(DIL_SLABS):
                onat[slot, s, pl.ds(rho, tm // d, stride=d), :] = (
                    o_ref[0, rho, :, s * LANES:(s + 1) * LANES].astype(F32))

    gate = gate_ref[...].astype(F32)
    pieces = [ya_ref[...]]
    for s in range(DIL_SLABS):
        outs = (o1_ref[:, s * LANES:(s + 1) * LANES].astype(F32), onat[0, s], onat[1, s])
        yb = None
        for w, o in zip(weights, outs):
            wb = w[:, s * LANES:(s + 1) * LANES]
            yb = wb * o if yb is None else yb + wb * o
        pieces.append(yb * gate[:, s * LANES:(s + 1) * LANES])
    mix = jnp.concatenate([piece.astype(BF16) for piece in pieces] + [ycg_ref[...]], axis=1)
    y = x_ref[...] + jnp.dot(mix, w_ref[...], preferred_element_type=F32)
    if final_norm:
        ms = jnp.mean(y * y, axis=-1, keepdims=True)
        y = y * lax.rsqrt(ms + NORM_EPS) * fg_ref[...]
    out_ref[...] = y


def _head_expand_matrix():
    e = np.zeros((2 * LANES, DIL_W), np.float32)
    for h in range(DIL_HEADS):
        e[h, h * HEAD_DIM:(h + 1) * HEAD_DIM] = 1.0
        e[LANES + h, h * HEAD_DIM:(h + 1) * HEAD_DIM] = 1.0
    return jnp.asarray(e, BF16)


def _out_proj(ya, outs, lses, ycg, gate, x2d, w_bf16, layer, final_g, final_norm, batch, seq):
    n = x2d.shape[0]
    tm = OUT_TILE
    seq_blocks = seq // tm
    row = lambda i: (i, 0)
    const = lambda i: (0, 0)
    perm_idx = lambda i: (i // seq_blocks, 0, i % seq_blocks, 0)
    o_specs = [pl.BlockSpec((tm, DIL_W), row)] + [
        pl.BlockSpec((1, d, tm // d, DIL_W), perm_idx) for d in DIL_DILATIONS[1:]]
    l_specs = [pl.BlockSpec((tm, LANES), row)] + [
        pl.BlockSpec((1, d, tm // d, LANES), perm_idx) for d in DIL_DILATIONS[1:]]
    return pl.pallas_call(
        functools.partial(_out_proj_kernel, final_norm=final_norm),
        out_shape=jax.ShapeDtypeStruct((n, D_MODEL), F32),
        grid=(n // tm,),
        in_specs=[pl.BlockSpec((tm, NA_W), row)] + o_specs + l_specs + [
            pl.BlockSpec((tm, CONV_CH), row),
            pl.BlockSpec((tm, DIL_W), row),
            pl.BlockSpec((tm, D_MODEL), row),
            pl.BlockSpec((None, D_MIX, D_MODEL), lambda i: (layer, 0, 0)),
            pl.BlockSpec((1, D_MODEL), const),
            pl.BlockSpec((2 * LANES, DIL_W), const),
        ],
        out_specs=pl.BlockSpec((tm, D_MODEL), row),
        scratch_shapes=[pltpu.VMEM((2, DIL_SLABS, tm, LANES), F32),
                        pltpu.VMEM((2, tm, LANES), F32)],
        compiler_params=_params("parallel"),
        name="out_proj",
    )(ya, *outs, *lses, ycg, gate, x2d, w_bf16, final_g, _head_expand_matrix())


def _rotary_tables(seq):
    half = HEAD_DIM // 2
    inv_freq = jnp.power(ROPE_THETA, -jnp.arange(half, dtype=F32) * 2.0 / HEAD_DIM)
    lane = np.arange(LANES)
    ang = jnp.arange(seq).astype(F32)[:, None] * inv_freq[lane % half][None, :]
    first_half = jnp.asarray((lane % HEAD_DIM) < half)[None, :]
    sin = jnp.sin(ang)
    cos_t = jnp.cos(ang)
    sa_t = jnp.where(first_half, -sin, 0.0)
    sb_t = jnp.where(first_half, 0.0, sin)
    return cos_t, sa_t, sb_t


def kernel(x, norm_g, w_in, na_rpb, conv_w, conv_b, conv_ln_g, conv_ln_b, pw_w, pw_b, w_out, final_g):
    batch, seq, _ = x.shape
    depth = w_in.shape[0]
    n = batch * seq
    cos_t, sa_t, sb_t = _rotary_tables(seq)
    h = x.reshape(n, D_MODEL)
    fg = final_g.reshape(1, D_MODEL)
    w_in_bf, w_out_bf, pw_bf = w_in.astype(BF16), w_out.astype(BF16), pw_w.astype(BF16)
    for l in range(depth):
        conv_params = (conv_w[l], conv_b[l].reshape(1, CONV_CH), conv_ln_g[l].reshape(1, CONV_CH),
                       conv_ln_b[l].reshape(1, CONV_CH), pw_bf, pw_b[l].reshape(1, CONV_CH))
        (naq, nak, nav, q1, q4, q16, k1, k4, k16, v1, v4, v16, ycg, gate_a, gate) = _in_proj(
            h, norm_g[l].reshape(1, D_MODEL), w_in_bf, l, cos_t, sa_t, sb_t, conv_params, batch, seq)
        ya = _na_attention(naq, nak, nav, _na_bias_table(na_rpb[l]), gate_a, batch, seq)
        outs, lses = [], []
        for d, (q, k, v) in zip(DIL_DILATIONS, ((q1, k1, v1), (q4, k4, v4), (q16, k16, v16))):
            groups = (batch * d, seq // d)
            o, lse = _band_attention(q.reshape(*groups, DIL_W), k.reshape(*groups, DIL_W),
                                     v.reshape(*groups, DIL_W))
            if d == 1:
                outs.append(o.reshape(n, DIL_W))
                lses.append(lse.reshape(n, LANES))
            else:
                outs.append(o.reshape(batch, d, seq // d, DIL_W))
                lses.append(lse.reshape(batch, d, seq // d, LANES))
        h = _out_proj(ya, outs, lses, ycg, gate, h, w_out_bf, l, fg,
                      final_norm=(l == depth - 1), batch=batch, seq=seq)
    return h.reshape(batch, seq, D_MODEL)
```

```python
import functools

import numpy as np
import jax
import jax.numpy as jnp
from jax import lax
from jax.experimental import pallas as pl
from jax.experimental.pallas import tpu as pltpu

D_MODEL = 1024
HEAD_DIM = 64
GRID_W = 64
NA_HEADS = 4
NA_WIN_ROWS = 8
NA_WIN_COLS = 16
DIL_HEADS = 6
DIL_DILATIONS = (1, 4, 16)
DIL_HALF = 64
CONV_CH = 384
CONV_WIDTH = 31
ROPE_THETA = 10000.0
NORM_EPS = 1e-6
NEG_INF = -1e30
NA_W = NA_HEADS * HEAD_DIM
DIL_W = DIL_HEADS * HEAD_DIM
D_MIX = NA_W + DIL_W + CONV_CH
SCALE = HEAD_DIM ** -0.5
LOG2E = 1.4426950408889634
Q_SCALE = SCALE * LOG2E

LANES = 128
SUBLANES = 8
VMEM_LIMIT = 56 * 1024 * 1024
DIL_SLABS = DIL_W // LANES

_C = np.cumsum([0] + [NA_W] * 4 + [DIL_W] * 4 + [CONV_CH] * 3)
(C_AQ, C_AK, C_AV, C_AG, C_BQ, C_BK, C_BV, C_BG, C_CA, C_CB, C_CG, C_END) = [int(c) for c in _C]

BF16 = jnp.bfloat16
F32 = jnp.float32

TOKEN_TILE = 1024
OUT_TILE = 1024
X_RING = 3
CONV_HALO = 16
CONV_PAD = CONV_WIDTH // 2
CONV_ROWS = 128


def _params(*sem):
    return pltpu.CompilerParams(dimension_semantics=sem, vmem_limit_bytes=VMEM_LIMIT)


def _silu(v):
    return v * (1.0 / (1.0 + jnp.exp(-v)))


def _lane_is_first_head(shape):
    return lax.broadcasted_iota(jnp.int32, shape, len(shape) - 1) < HEAD_DIM


def _in_proj_kernel(xp_ref, x_ref, xn_ref, g_ref, w_ref, cos_ref, sa_ref, sb_ref,
                    cw_ref, cb_ref, lg_ref, lb_ref, pw_ref, pb_ref,
                    naq_ref, nak_ref, nav_ref,
                    q1_ref, q4_ref, q16_ref, k1_ref, k4_ref, k16_ref, v1_ref, v4_ref, v16_ref,
                    ycg_ref, gate_a_ref, gate_b_ref, perm, perm4, ubuf, ybuf, *, seq_blocks):
    tm = x_ref.shape[0]

    def normed(x):
        ms = jnp.mean(x * x, axis=-1, keepdims=True)
        return (x * lax.rsqrt(ms + NORM_EPS) * g_ref[...]).astype(BF16)

    h = normed(x_ref[...])

    def mm(c0, c1):
        return jnp.dot(h, w_ref[:, c0:c1], preferred_element_type=F32)

    cos = cos_ref[...]
    sa = sa_ref[...]
    sb = sb_ref[...]

    def rope(zs):
        up = pltpu.roll(zs, LANES - HEAD_DIM // 2, 1)
        dn = pltpu.roll(zs, HEAD_DIM // 2, 1)
        return zs * cos + up * sa + dn * sb

    def emit_dilated(z, refs, rotary, scale):
        ref1, ref4, ref16 = refs
        for s in range(DIL_SLABS):
            zs = z[:, s * LANES:(s + 1) * LANES]
            if rotary:
                zs = rope(zs)
            if scale != 1.0:
                zs = zs * scale
            perm[s] = zs
            ref1[:, s * LANES:(s + 1) * LANES] = zs.astype(BF16)
        for r1 in range(4):
            for s in range(DIL_SLABS):
                t = perm[s, pl.ds(r1, tm // 4, stride=4), :]
                perm4[s, r1] = t
                ref4[0, r1, :, s * LANES:(s + 1) * LANES] = t.astype(BF16)
        for rho in range(16):
            for s in range(DIL_SLABS):
                ref16[0, rho, :, s * LANES:(s + 1) * LANES] = (
                    perm4[s, rho % 4, pl.ds(rho // 4, tm // 16, stride=4), :].astype(BF16))

    i = pl.program_id(0)
    pos = i % seq_blocks
    h_ext = jnp.concatenate([normed(xp_ref[...]), h, normed(xn_ref[...])], axis=0)
    zc = jnp.dot(h_ext, w_ref[:, C_CA:C_CG], preferred_element_type=F32)
    u = zc[:, 0:CONV_CH] * (1.0 / (1.0 + jnp.exp(-zc[:, CONV_CH:2 * CONV_CH])))
    ubuf[0:CONV_HALO, :] = u[0:CONV_HALO] * jnp.where(pos > 0, 1.0, 0.0).astype(F32)
    ubuf[CONV_HALO:CONV_HALO + tm, :] = u[CONV_HALO:CONV_HALO + tm]
    ubuf[CONV_HALO + tm:, :] = u[CONV_HALO + tm:] * jnp.where(pos < seq_blocks - 1, 1.0, 0.0).astype(F32)

    first_off = CONV_HALO - CONV_PAD
    for c in range(tm // CONV_ROWS):
        r0 = c * CONV_ROWS
        for sl in range(CONV_CH // LANES):
            lanes = slice(sl * LANES, (sl + 1) * LANES)
            y = None
            for s in range(SUBLANES):
                part = None
                for k in range(CONV_WIDTH):
                    off = first_off + k
                    if off % SUBLANES != s:
                        continue
                    win = ubuf[pl.ds(r0 + (off - s), CONV_ROWS + SUBLANES), lanes]
                    term = win * cw_ref[k:k + 1, lanes]
                    part = term if part is None else part + term
                shifted = part[s:s + CONV_ROWS, :]
                y = shifted if y is None else y + shifted
            ybuf[pl.ds(r0, CONV_ROWS), lanes] = y

    za = mm(C_AQ, C_BQ)
    naq_ref[...] = (za[:, 0:NA_W] * Q_SCALE).astype(BF16)
    nak_ref[...] = za[:, NA_W:2 * NA_W].astype(BF16)
    nav_ref[...] = za[:, 2 * NA_W:3 * NA_W].astype(BF16)
    gate_a_ref[...] = _silu(za[:, 3 * NA_W:4 * NA_W]).astype(BF16)

    zqk = mm(C_BQ, C_BV)
    emit_dilated(zqk[:, 0:DIL_W], (q1_ref, q4_ref, q16_ref), True, Q_SCALE)
    emit_dilated(zqk[:, DIL_W:2 * DIL_W], (k1_ref, k4_ref, k16_ref), True, 1.0)

    zvg = mm(C_BV, C_CA)
    emit_dilated(zvg[:, 0:DIL_W], (v1_ref, v4_ref, v16_ref), False, 1.0)
    gate_b_ref[...] = _silu(zvg[:, DIL_W:2 * DIL_W]).astype(BF16)

    y = ybuf[...] + cb_ref[...]
    mu = jnp.mean(y, axis=-1, keepdims=True)
    yc = y - mu
    var = jnp.mean(yc * yc, axis=-1, keepdims=True)
    yn = yc * lax.rsqrt(var + NORM_EPS) * lg_ref[...] + lb_ref[...]
    act = _silu(yn).astype(BF16)
    yc = jnp.dot(act, pw_ref[...], preferred_element_type=F32) + pb_ref[...]
    ycg_ref[...] = (yc * _silu(mm(C_CG, C_END))).astype(BF16)


def _in_proj(x2d, g, w_bf16, cos_t, sa_t, sb_t, conv_params, batch, seq):
    n = x2d.shape[0]
    tm = TOKEN_TILE
    seq_blocks = seq // tm
    halo_per_tile = tm // CONV_HALO
    row = lambda i: (i, 0)
    prev = lambda i: (jnp.maximum(i * halo_per_tile - 1, 0), 0)
    nxt = lambda i: (jnp.minimum((i + 1) * halo_per_tile, n // CONV_HALO - 1), 0)
    tab = lambda i: (i % seq_blocks, 0)
    const = lambda i: (0, 0)
    perm_idx = lambda i: (i // seq_blocks, 0, i % seq_blocks, 0)
    vec = pl.BlockSpec((1, CONV_CH), const)

    def dil_shapes():
        return [jax.ShapeDtypeStruct((n, DIL_W), BF16)] + [
            jax.ShapeDtypeStruct((batch, d, seq // d, DIL_W), BF16) for d in DIL_DILATIONS[1:]]

    def dil_specs():
        return [pl.BlockSpec((tm, DIL_W), row)] + [
            pl.BlockSpec((1, d, tm // d, DIL_W), perm_idx) for d in DIL_DILATIONS[1:]]

    out_shapes = ([jax.ShapeDtypeStruct((n, NA_W), BF16)] * 3 + dil_shapes() + dil_shapes() + dil_shapes()
                  + [jax.ShapeDtypeStruct((n, CONV_CH), BF16), jax.ShapeDtypeStruct((n, NA_W), BF16),
                     jax.ShapeDtypeStruct((n, DIL_W), BF16)])
    out_specs = ([pl.BlockSpec((tm, NA_W), row)] * 3 + dil_specs() + dil_specs() + dil_specs()
                 + [pl.BlockSpec((tm, CONV_CH), row), pl.BlockSpec((tm, NA_W), row),
                    pl.BlockSpec((tm, DIL_W), row)])
    return pl.pallas_call(
        functools.partial(_in_proj_kernel, seq_blocks=seq_blocks),
        out_shape=out_shapes,
        grid=(n // tm,),
        in_specs=[
            pl.BlockSpec((CONV_HALO, D_MODEL), prev),
            pl.BlockSpec((tm, D_MODEL), row),
            pl.BlockSpec((CONV_HALO, D_MODEL), nxt),
            pl.BlockSpec((1, D_MODEL), const),
            pl.BlockSpec((D_MODEL, C_END), const),
            pl.BlockSpec((tm, LANES), tab),
            pl.BlockSpec((tm, LANES), tab),
            pl.BlockSpec((tm, LANES), tab),
            pl.BlockSpec((CONV_WIDTH, CONV_CH), const),
            vec, vec, vec,
            pl.BlockSpec((CONV_CH, CONV_CH), const),
            vec,
        ],
        out_specs=out_specs,
        scratch_shapes=[pltpu.VMEM((DIL_SLABS, tm, LANES), F32),
                        pltpu.VMEM((DIL_SLABS, 4, tm // 4, LANES), F32),
                        pltpu.VMEM((tm + 2 * CONV_HALO, CONV_CH), F32),
                        pltpu.VMEM((tm, CONV_CH), F32)],
        compiler_params=_params("parallel"),
        name="in_proj",
    )(x2d, x2d, x2d, g, w_bf16, cos_t, sa_t, sb_t, *conv_params)


NA_ROWS_PER_STEP = 32
NA_KEYS = NA_WIN_ROWS * GRID_W


def _na_kernel(q_ref, k_ref, v_ref, bias_ref, gate_ref, o_ref, sbuf, ebuf, *, rows):
    blk = pl.program_id(1)
    first = _lane_is_first_head((GRID_W, LANES))
    ones = jnp.ones((NA_KEYS, LANES), BF16)
    n_pairs = NA_W // LANES
    starts, maxes = [], []
    for rr in range(NA_ROWS_PER_STEP):
        r = blk * NA_ROWS_PER_STEP + rr
        r0 = jnp.clip(r - NA_WIN_ROWS // 2, 0, rows - NA_WIN_ROWS)
        delta = r0 - r + (NA_WIN_ROWS - 1)
        start = pl.multiple_of(r0 * GRID_W, GRID_W)
        starts.append(start)
        q_row = q_ref[0, rr * GRID_W:(rr + 1) * GRID_W, :]
        k_win = k_ref[0, pl.ds(start, NA_KEYS), :]
        for p in range(n_pairs):
            qp = q_row[:, p * LANES:(p + 1) * LANES]
            zero = jnp.zeros_like(qp)
            lhs = jnp.concatenate([jnp.where(first, qp, zero), jnp.where(first, zero, qp)], axis=0)
            kp = k_win[:, p * LANES:(p + 1) * LANES]
            s = lax.dot_general(lhs, kp, (((1,), (1,)), ((), ())), preferred_element_type=F32)
            bias = jnp.concatenate([bias_ref[p, delta + 2 * j] for j in range(NA_WIN_ROWS // 2)], axis=1)
            s = s + bias
            sbuf[rr * n_pairs + p] = s
            maxes.append(jnp.max(s, axis=-1, keepdims=True))
    for c in range(NA_ROWS_PER_STEP * n_pairs):
        ebuf[c] = jnp.exp2(sbuf[c] - maxes[c]).astype(BF16)
    for rr in range(NA_ROWS_PER_STEP):
        v_win = v_ref[0, pl.ds(starts[rr], NA_KEYS), :]
        for p in range(n_pairs):
            v_aug = jnp.concatenate([v_win[:, p * LANES:(p + 1) * LANES], ones], axis=1)
            o = jnp.dot(ebuf[rr * n_pairs + p], v_aug, preferred_element_type=F32)
            o = o[:, 0:LANES] * (1.0 / o[:, LANES:2 * LANES])
            out = jnp.where(first, o[0:GRID_W], o[GRID_W:2 * GRID_W])
            tile = (slice(rr * GRID_W, (rr + 1) * GRID_W), slice(p * LANES, (p + 1) * LANES))
            o_ref[0, tile[0], tile[1]] = (out * gate_ref[0, tile[0], tile[1]].astype(F32)).astype(BF16)


def _na_bias_table(rpb):
    cols = np.arange(GRID_W)
    col_start = np.clip(cols - NA_WIN_COLS // 2, 0, GRID_W - NA_WIN_COLS)
    kc = np.arange(GRID_W)
    in_win = (kc[None, :] >= col_start[:, None]) & (kc[None, :] < col_start[:, None] + NA_WIN_COLS)
    col_off = kc[None, :] - cols[:, None] + (NA_WIN_COLS - 1)
    n_off = 2 * NA_WIN_COLS - 1
    onehot = (col_off[None] == np.arange(n_off)[:, None, None]) & in_win[None]
    t = jnp.einsum('hrj,jck->hrck', rpb.astype(F32), jnp.asarray(onehot, F32),
                   precision=lax.Precision.HIGHEST)
    t = jnp.where(jnp.asarray(in_win)[None, None], t * LOG2E, NEG_INF)
    n_ro = 2 * NA_WIN_ROWS - 2
    t = jnp.concatenate([t[:, 0:n_ro], t[:, 1:n_ro + 1]], axis=-1)
    t = t.reshape(NA_HEADS // 2, 2, n_ro, GRID_W, 2 * GRID_W).transpose(0, 2, 1, 3, 4)
    return t.reshape(NA_HEADS // 2, n_ro, 2 * GRID_W, 2 * GRID_W)


def _na_attention(q, k, v, bias, gate, batch, seq):
    rows = seq // GRID_W
    tq = NA_ROWS_PER_STEP * GRID_W
    q3 = q.reshape(batch, seq, NA_W)
    k3 = k.reshape(batch, seq, NA_W)
    v3 = v.reshape(batch, seq, NA_W)
    g3 = gate.reshape(batch, seq, NA_W)
    out = pl.pallas_call(
        functools.partial(_na_kernel, rows=rows),
        out_shape=jax.ShapeDtypeStruct((batch, seq, NA_W), BF16),
        grid=(batch, seq // tq),
        in_specs=[
            pl.BlockSpec((1, tq, NA_W), lambda b, i: (b, i, 0)),
            pl.BlockSpec((1, seq, NA_W), lambda b, i: (b, 0, 0)),
            pl.BlockSpec((1, seq, NA_W), lambda b, i: (b, 0, 0)),
            pl.BlockSpec(bias.shape, lambda b, i: (0, 0, 0, 0)),
            pl.BlockSpec((1, tq, NA_W), lambda b, i: (b, i, 0)),
        ],
        out_specs=pl.BlockSpec((1, tq, NA_W), lambda b, i: (b, i, 0)),
        scratch_shapes=[pltpu.VMEM((NA_ROWS_PER_STEP * NA_W // LANES, 2 * GRID_W, NA_KEYS), F32),
                        pltpu.VMEM((NA_ROWS_PER_STEP * NA_W // LANES, 2 * GRID_W, NA_KEYS), BF16)],
        compiler_params=_params("parallel", "arbitrary"),
        name="na_attention",
    )(q3, k3, v3, bias, g3)
    return out.reshape(batch * seq, NA_W)


BAND_CHUNK = 128
BAND_KEYS = BAND_CHUNK + 2 * DIL_HALF
BAND_STEP_TOKENS = 4096
STAT_SUM_LANE = 8


def _band_kernel(q_ref, kp_ref, kc_ref, kn_ref, vp_ref, vc_ref, vn_ref, o_ref, stat_ref, *, length):
    blk = pl.program_id(1)
    BAND_TQ = q_ref.shape[1]
    n_chunks = BAND_TQ // BAND_CHUNK
    first = _lane_is_first_head((BAND_CHUNK, LANES))
    lane = lax.broadcasted_iota(jnp.int32, (BAND_CHUNK, LANES), 1)
    qi = lax.broadcasted_iota(jnp.int32, (BAND_CHUNK, BAND_KEYS), 0)
    kj = lax.broadcasted_iota(jnp.int32, (BAND_CHUNK, BAND_KEYS), 1)
    rel = kj - DIL_HALF - qi
    band = (rel >= -DIL_HALF) & (rel <= DIL_HALF)
    ones = jnp.ones((BAND_KEYS, LANES), BF16)
    for gc in range(q_ref.shape[0] * n_chunks):
        gb, c = divmod(gc, n_chunks)
        a0 = blk * BAND_TQ + c * BAND_CHUNK
        kpos = a0 - DIL_HALF + kj
        ok = band & (kpos >= 0) & (kpos < length)
        mask_bias = jnp.where(ok, 0.0, NEG_INF).astype(F32)
        mask_bias2 = jnp.concatenate([mask_bias, mask_bias], axis=0)
        lo = c * BAND_CHUNK - DIL_HALF
        hi = lo + BAND_KEYS

        def window(prev_ref, cur_ref, next_ref):
            pieces = []
            if lo < 0:
                pieces.append(prev_ref[gb, :, :])
            pieces.append(cur_ref[gb, max(lo, 0):min(hi, BAND_TQ), :])
            if hi > BAND_TQ:
                pieces.append(next_ref[gb, :, :])
            return pieces[0] if len(pieces) == 1 else jnp.concatenate(pieces, axis=0)

        k_win = window(kp_ref, kc_ref, kn_ref)
        v_win = window(vp_ref, vc_ref, vn_ref)
        q_chunk = q_ref[gb, c * BAND_CHUNK:(c + 1) * BAND_CHUNK, :]
        rows = slice(c * BAND_CHUNK, (c + 1) * BAND_CHUNK)
        stat_tile = jnp.zeros((BAND_CHUNK, LANES), F32)
        for p in range(DIL_SLABS):
            qp = q_chunk[:, p * LANES:(p + 1) * LANES]
            zero = jnp.zeros_like(qp)
            lhs = jnp.concatenate([jnp.where(first, qp, zero), jnp.where(first, zero, qp)], axis=0)
            kp = k_win[:, p * LANES:(p + 1) * LANES]
            vp = v_win[:, p * LANES:(p + 1) * LANES]
            s = lax.dot_general(lhs, kp, (((1,), (1,)), ((), ())), preferred_element_type=F32)
            s = s + mask_bias2
            m = jnp.max(s, axis=-1, keepdims=True)
            e = jnp.exp2(s - m).astype(BF16)
            o = jnp.dot(e, jnp.concatenate([vp, ones], axis=1), preferred_element_type=F32)
            l = o[:, LANES:2 * LANES]
            o_ref[gb, rows, p * LANES:(p + 1) * LANES] = jnp.where(
                first, o[0:BAND_CHUNK, 0:LANES], o[BAND_CHUNK:, 0:LANES]).astype(BF16)
            for hh in range(2):
                half = slice(hh * BAND_CHUNK, (hh + 1) * BAND_CHUNK)
                stat_tile = jnp.where(lane == 2 * p + hh, m[half], stat_tile)
                stat_tile = jnp.where(lane == STAT_SUM_LANE + 2 * p + hh, l[half], stat_tile)
        stat_ref[gb, rows, :] = stat_tile


def _band_attention(q, k, v):
    groups, length, _ = q.shape
    tq = min(BAND_STEP_TOKENS, length)
    gb = BAND_STEP_TOKENS // tq
    halo_per_blk = tq // DIL_HALF
    n_halo = length // DIL_HALF
    cur = lambda g, i: (g, i, 0)
    prev = lambda g, i: (g, jnp.maximum(i * halo_per_blk - 1, 0), 0)
    nxt = lambda g, i: (g, jnp.minimum((i + 1) * halo_per_blk, n_halo - 1), 0)
    blk = pl.BlockSpec((gb, tq, DIL_W), cur)
    halo_p = pl.BlockSpec((gb, DIL_HALF, DIL_W), prev)
    halo_n = pl.BlockSpec((gb, DIL_HALF, DIL_W), nxt)
    return pl.pallas_call(
        functools.partial(_band_kernel, length=length),
        out_shape=[jax.ShapeDtypeStruct((groups, length, DIL_W), BF16),
                   jax.ShapeDtypeStruct((groups, length, LANES), F32)],
        grid=(groups // gb, length // tq),
        in_specs=[blk, halo_p, blk, halo_n, halo_p, blk, halo_n],
        out_specs=[pl.BlockSpec((gb, tq, DIL_W), cur), pl.BlockSpec((gb, tq, LANES), cur)],
        compiler_params=_params("parallel", "arbitrary"),
        name="band_attention",
    )(q, k, k, k, v, v, v)


def _out_proj_kernel(ya_ref, o1_ref, o4_ref, o16_ref, l1_ref, l4_ref, l16_ref, ycg_ref, gate_ref,
                     x_hbm, w_ref, fg_ref, expand_ref, out_ref, onat, lnat, xbuf, xsem, *, final_norm):
    tm = out_ref.shape[0]
    lane = lax.broadcasted_iota(jnp.int32, (tm, LANES), 1)

    step = pl.program_id(0)
    n_steps = pl.num_programs(0)

    def x_copy(st):
        slot = st % X_RING
        return pltpu.make_async_copy(x_hbm.at[pl.ds(st * tm, tm), :], xbuf.at[slot], xsem.at[slot])

    @pl.when(step == 0)
    def _():
        for st in range(X_RING - 1):
            x_copy(st).start()

    @pl.when(step + (X_RING - 1) < n_steps)
    def _():
        x_copy(step + (X_RING - 1)).start()

    def natural_stats(l_ref, d, slot):
        if d == 1:
            return l_ref[...]
        for rho in range(d):
            lnat[slot, pl.ds(rho, tm // d, stride=d), :] = l_ref[0, rho]
        return lnat[slot]

    stats = (natural_stats(l1_ref, 1, 0), natural_stats(l4_ref, 4, 0), natural_stats(l16_ref, 16, 1))
    m = jnp.maximum(jnp.maximum(stats[0], stats[1]), stats[2])
    scales = [jnp.exp2(st - m) for st in stats]
    denom = None
    for st, sc in zip(stats, scales):
        row_sum = pltpu.roll(st, LANES - STAT_SUM_LANE, 1)
        denom = sc * row_sum if denom is None else denom + sc * row_sum
    inv = 1.0 / denom

    def per_channel(sc):
        w = jnp.where(lane < DIL_HEADS, sc * inv, 0.0)
        hi = w.astype(BF16)
        lo = (w - hi.astype(F32)).astype(BF16)
        return jnp.dot(jnp.concatenate([hi, lo], axis=1), expand_ref[...], preferred_element_type=F32)

    weights = [per_channel(sc) for sc in scales]

    for slot, (d, o_ref) in enumerate(((4, o4_ref), (16, o16_ref))):
        for rho in range(d):
            for s in range(DIL_SLABS):
                onat[slot, s, pl.ds(rho, tm // d, stride=d), :] = (
                    o_ref[0, rho, :, s * LANES:(s + 1) * LANES].astype(F32))

    gate = gate_ref[...].astype(F32)
    pieces = [ya_ref[...]]
    for s in range(DIL_SLABS):
        outs = (o1_ref[:, s * LANES:(s + 1) * LANES].astype(F32), onat[0, s], onat[1, s])
        yb = None
        for w, o in zip(weights, outs):
            wb = w[:, s * LANES:(s + 1) * LANES]
            yb = wb * o if yb is None else yb + wb * o
        pieces.append(yb * gate[:, s * LANES:(s + 1) * LANES])
    mix = jnp.concatenate([piece.astype(BF16) for piece in pieces] + [ycg_ref[...]], axis=1)
    x_copy(step).wait()
    y = xbuf[step % X_RING] + jnp.dot(mix, w_ref[...], preferred_element_type=F32)
    if final_norm:
        ms = jnp.mean(y * y, axis=-1, keepdims=True)
        y = y * lax.rsqrt(ms + NORM_EPS) * fg_ref[...]
    out_ref[...] = y


def _head_expand_matrix():
    e = np.zeros((2 * LANES, DIL_W), np.float32)
    for h in range(DIL_HEADS):
        e[h, h * HEAD_DIM:(h + 1) * HEAD_DIM] = 1.0
        e[LANES + h, h * HEAD_DIM:(h + 1) * HEAD_DIM] = 1.0
    return jnp.asarray(e, BF16)


def _out_proj(ya, outs, lses, ycg, gate, x2d, w_bf16, final_g, final_norm, batch, seq):
    n = x2d.shape[0]
    tm = OUT_TILE
    seq_blocks = seq // tm
    row = lambda i: (i, 0)
    const = lambda i: (0, 0)
    perm_idx = lambda i: (i // seq_blocks, 0, i % seq_blocks, 0)
    o_specs = [pl.BlockSpec((tm, DIL_W), row)] + [
        pl.BlockSpec((1, d, tm // d, DIL_W), perm_idx) for d in DIL_DILATIONS[1:]]
    l_specs = [pl.BlockSpec((tm, LANES), row)] + [
        pl.BlockSpec((1, d, tm // d, LANES), perm_idx) for d in DIL_DILATIONS[1:]]
    return pl.pallas_call(
        functools.partial(_out_proj_kernel, final_norm=final_norm),
        out_shape=jax.ShapeDtypeStruct((n, D_MODEL), F32),
        grid=(n // tm,),
        in_specs=[pl.BlockSpec((tm, NA_W), row)] + o_specs + l_specs + [
            pl.BlockSpec((tm, CONV_CH), row),
            pl.BlockSpec((tm, DIL_W), row),
            pl.BlockSpec(memory_space=pl.ANY),
            pl.BlockSpec((D_MIX, D_MODEL), const),
            pl.BlockSpec((1, D_MODEL), const),
            pl.BlockSpec((2 * LANES, DIL_W), const),
        ],
        out_specs=pl.BlockSpec((tm, D_MODEL), row),
        scratch_shapes=[pltpu.VMEM((2, DIL_SLABS, tm, LANES), F32),
                        pltpu.VMEM((2, tm, LANES), F32),
                        pltpu.VMEM((X_RING, tm, D_MODEL), F32),
                        pltpu.SemaphoreType.DMA((X_RING,))],
        compiler_params=_params("arbitrary"),
        name="out_proj",
    )(ya, *outs, *lses, ycg, gate, x2d, w_bf16, final_g, _head_expand_matrix())


def _rotary_tables(seq):
    half = HEAD_DIM // 2
    inv_freq = jnp.power(ROPE_THETA, -jnp.arange(half, dtype=F32) * 2.0 / HEAD_DIM)
    ang = jnp.arange(seq).astype(F32)[:, None] * inv_freq[None, :]
    cos = jnp.cos(ang)
    sin = jnp.sin(ang)
    zero = jnp.zeros_like(sin)
    reps = LANES // HEAD_DIM
    cos_t = jnp.tile(jnp.concatenate([cos, cos], axis=1), (1, reps))
    sa_t = jnp.tile(jnp.concatenate([-sin, zero], axis=1), (1, reps))
    sb_t = jnp.tile(jnp.concatenate([zero, sin], axis=1), (1, reps))
    return cos_t, sa_t, sb_t


def kernel(x, norm_g, w_in, na_rpb, conv_w, conv_b, conv_ln_g, conv_ln_b, pw_w, pw_b, w_out, final_g):
    batch, seq, _ = x.shape
    depth = w_in.shape[0]
    n = batch * seq
    cos_t, sa_t, sb_t = _rotary_tables(seq)
    h = x.reshape(n, D_MODEL)
    fg = final_g.reshape(1, D_MODEL)
    for l in range(depth):
        conv_params = (conv_w[l], conv_b[l].reshape(1, CONV_CH), conv_ln_g[l].reshape(1, CONV_CH),
                       conv_ln_b[l].reshape(1, CONV_CH), pw_w[l].astype(BF16), pw_b[l].reshape(1, CONV_CH))
        (naq, nak, nav, q1, q4, q16, k1, k4, k16, v1, v4, v16, ycg, gate_a, gate) = _in_proj(
            h, norm_g[l].reshape(1, D_MODEL), w_in[l].astype(BF16), cos_t, sa_t, sb_t, conv_params, batch, seq)
        ya = _na_attention(naq, nak, nav, _na_bias_table(na_rpb[l]), gate_a, batch, seq)
        outs, lses = [], []
        for d, (q, k, v) in zip(DIL_DILATIONS, ((q1, k1, v1), (q4, k4, v4), (q16, k16, v16))):
            groups = (batch * d, seq // d)
            o, lse = _band_attention(q.reshape(*groups, DIL_W), k.reshape(*groups, DIL_W),
                                     v.reshape(*groups, DIL_W))
            if d == 1:
                outs.append(o.reshape(n, DIL_W))
                lses.append(lse.reshape(n, LANES))
            else:
                outs.append(o.reshape(batch, d, seq // d, DIL_W))
                lses.append(lse.reshape(batch, d, seq // d, LANES))
        h = _out_proj(ya, outs, lses, ycg, gate, h, w_out[l].astype(BF16), fg,
                      final_norm=(l == depth - 1), batch=batch, seq=seq)
    return h.reshape(batch, seq, D_MODEL)
```
